```python
import jax, jax.numpy as jnp
from jax import lax
import numpy as np

D_MODEL = 1024
BATCH = 4
SEQ = 4096
DEPTH = 4

HEAD_DIM = 64
MIX_WIDTH = D_MODEL
N_HEADS = MIX_WIDTH // HEAD_DIM
A_HEADS = (3 * N_HEADS) // 8
B_HEADS = N_HEADS // 4
C_HEADS = N_HEADS - A_HEADS - B_HEADS
A_WIDTH = A_HEADS * HEAD_DIM
B_WIDTH = B_HEADS * HEAD_DIM
C_WIDTH = C_HEADS * HEAD_DIM
DECAY_LORA = 64
ICLR_LORA = 64
A_PROJ = 3 * A_WIDTH + DECAY_LORA + ICLR_LORA
B_PROJ = 3 * B_WIDTH
C_PROJ = 3 * C_WIDTH
GATE_PROJ = MIX_WIDTH
PROJ_WIDTH = A_PROJ + B_PROJ + C_PROJ + GATE_PROJ
MOBA_BLOCK = 256
MOBA_TOPK = 3
MOBA_QBLOCK = 128
DILATED_PAIRS = ((128, 1), (512, 4), (2048, 16))
RMS_EPS = 1e-6
LNX_EPS = HEAD_DIM * 1e-5
ATTN_SCALE = HEAD_DIM ** -0.5

kernel_name = "hymba_rwkv7_moba_dilated_trunk"


def _rmsnorm(t, g):
    t32 = t.astype(jnp.float32)
    return t32 * lax.rsqrt(jnp.mean(t32 * t32, axis=-1, keepdims=True) + RMS_EPS) * g.astype(jnp.float32)


def _delta_scan(r, w, k, v, kk, a):
    bsz, _, nh, n = r.shape

    def step(state, inp):
        r_t, w_t, k_t, v_t, kk_t, a_t = inp
        sa = jnp.einsum('bhvk,bhk->bhv', state, -kk_t)
        state = (state * w_t[:, :, None, :]
                 + sa[..., None] * (kk_t * a_t)[:, :, None, :]
                 + v_t[..., None] * k_t[:, :, None, :])
        return state, jnp.einsum('bhvk,bhk->bhv', state, r_t)

    xs = tuple(jnp.moveaxis(t, 1, 0) for t in (r, w, k, v, kk, a))
    s0 = jnp.zeros((bsz, nh, n, n), jnp.float32)
    _, y = lax.scan(step, s0, xs)
    return jnp.moveaxis(y, 0, 1)


def _rwkv7(pa, mu, w0, w_up, a0, a_up, k_k, k_a, r_k, lnx_g, lnx_b):
    pa = pa.astype(jnp.float32)
    bsz, seq, _ = pa.shape
    prev = jnp.pad(pa, ((0, 0), (1, 0), (0, 0)))[:, :-1]
    pa = pa + (prev - pa) * mu.astype(jnp.float32)
    r, k, v, wd, ad = jnp.split(pa, [A_WIDTH, 2 * A_WIDTH, 3 * A_WIDTH, 3 * A_WIDTH + DECAY_LORA], axis=-1)
    w = -jax.nn.softplus(-(w0.astype(jnp.float32) + jnp.tanh(wd) @ w_up.astype(jnp.float32))) - 0.5
    decay = jnp.exp(-jnp.exp(w))
    a = jax.nn.sigmoid(a0.astype(jnp.float32) + ad @ a_up.astype(jnp.float32))
    heads = lambda t: t.reshape(bsz, seq, A_HEADS, HEAD_DIM)
    kk = heads(k * k_k.astype(jnp.float32))
    kk = kk * lax.rsqrt(jnp.sum(kk * kk, axis=-1, keepdims=True) + 1e-12)
    k = k * (1.0 + (a - 1.0) * k_a.astype(jnp.float32))
    r, k, v, decay, a = (heads(t) for t in (r, k, v, decay, a))
    y = _delta_scan(r, decay, k, v, kk, a)
    mean = jnp.mean(y, axis=-1, keepdims=True)
    var = jnp.mean(jnp.square(y - mean), axis=-1, keepdims=True)
    g_h = lnx_g.astype(jnp.float32).reshape(A_HEADS, HEAD_DIM)
    b_h = lnx_b.astype(jnp.float32).reshape(A_HEADS, HEAD_DIM)
    y = (y - mean) * lax.rsqrt(var + LNX_EPS) * g_h + b_h
    y = y + jnp.sum(r * k * r_k.astype(jnp.float32), axis=-1, keepdims=True) * v
    return y.reshape(bsz, seq, A_WIDTH)


def _split_heads_qk_norm(pq, n_heads, q_g, k_g):
    bsz, seq, _ = pq.shape
    q, k, v = jnp.split(pq.astype(jnp.float32), 3, axis=-1)
    q, k, v = (t.reshape(bsz, seq, n_heads, HEAD_DIM) for t in (q, k, v))
    return _rmsnorm(q, q_g), _rmsnorm(k, k_g), v


def _moba(q, k, v):
    bsz, seq, nh, dh = q.shape
    nb = -(-seq // MOBA_BLOCK)
    sp = nb * MOBA_BLOCK
    pad = ((0, 0), (0, sp - seq), (0, 0), (0, 0))
    q, k, v = (jnp.pad(t, pad) for t in (q, k, v))
    kb = k.reshape(bsz, nb, MOBA_BLOCK, nh, dh).transpose(0, 3, 1, 2, 4)
    vb = v.reshape(bsz, nb, MOBA_BLOCK, nh, dh).transpose(0, 3, 1, 2, 4)
    k_mean = jnp.mean(kb, axis=3)
    n_q = sp // MOBA_QBLOCK
    qc = q.reshape(bsz, n_q, MOBA_QBLOCK, nh, dh).transpose(1, 0, 3, 2, 4)
    top_k = min(MOBA_TOPK, nb - 1)
    bi = jnp.arange(bsz)[:, None, None, None]
    hi = jnp.arange(nh)[None, :, None, None]

    def one_chunk(args):
        ci, qblk = args
        q0 = ci * MOBA_QBLOCK
        b_own = q0 // MOBA_BLOCK
        qpos = q0 + jnp.arange(MOBA_QBLOCK)
        kpos = b_own * MOBA_BLOCK + jnp.arange(MOBA_BLOCK)
        k_own = lax.dynamic_index_in_dim(kb, b_own, axis=2, keepdims=False)
        v_own = lax.dynamic_index_in_dim(vb, b_own, axis=2, keepdims=False)
        s_own = jnp.einsum('bhqd,bhkd->bhqk', qblk, k_own) * ATTN_SCALE
        s_own = jnp.where(kpos[None, :] <= qpos[:, None], s_own, -jnp.inf)
        if top_k > 0:
            gate = jnp.einsum('bhqd,bhnd->bhqn', qblk, k_mean)
            gate = jnp.where(jnp.arange(nb) < b_own, gate, -jnp.inf)
            _, idx = lax.top_k(gate, top_k)
            k_sel = kb[bi, hi, idx]
            v_sel = vb[bi, hi, idx]
            s_sel = jnp.einsum('bhqd,bhqnkd->bhqnk', qblk, k_sel) * ATTN_SCALE
            s_sel = jnp.where((idx < b_own)[..., None], s_sel, -jnp.inf)
            s_all = jnp.concatenate(
                [s_own, s_sel.reshape(bsz, nh, MOBA_QBLOCK, top_k * MOBA_BLOCK)], axis=-1)
            p = jax.nn.softmax(s_all, axis=-1)
            p_own = p[..., :MOBA_BLOCK]
            p_sel = p[..., MOBA_BLOCK:].reshape(bsz, nh, MOBA_QBLOCK, top_k, MOBA_BLOCK)
            return (jnp.einsum('bhqk,bhkd->bhqd', p_own, v_own)
                    + jnp.einsum('bhqnk,bhqnkd->bhqd', p_sel, v_sel))
        p = jax.nn.softmax(s_own, axis=-1)
        return jnp.einsum('bhqk,bhkd->bhqd', p, v_own)

    o = lax.map(one_chunk, (jnp.arange(n_q), qc))
    return o.transpose(1, 0, 3, 2, 4).reshape(bsz, sp, nh, dh)[:, :seq]


def _band_attention(q, k, v, span):
    n, nh, length, dh = q.shape
    nblk = -(-length // span)
    lp = nblk * span
    pad = ((0, 0), (0, 0), (0, lp - length), (0, 0))
    qb, kb, vb = (jnp.pad(t, pad).reshape(n, nh, nblk, span, dh) for t in (q, k, v))
    prev = lambda t: jnp.pad(t, ((0, 0), (0, 0), (1, 0), (0, 0), (0, 0)))[:, :, :-1]
    kw = jnp.concatenate([prev(kb), kb], axis=3)
    vw = jnp.concatenate([prev(vb), vb], axis=3)
    s = jnp.einsum('nhcqd,nhckd->nhcqk', qb, kw) * ATTN_SCALE
    qi = jnp.arange(span)[:, None]
    kj = jnp.arange(2 * span)[None, :] - span
    band = (qi - kj >= 0) & (qi - kj <= span)
    first_ok = (jnp.arange(nblk)[:, None, None] > 0) | (kj[None] >= 0)
    s = jnp.where(band[None] & first_ok, s, -jnp.inf)
    m = jnp.max(s, axis=-1, keepdims=True)
    p = jnp.exp(s - m)
    den = jnp.sum(p, axis=-1)
    o = jnp.einsum('nhcqk,nhckd->nhcqd', p, vw) / den[..., None]
    lse = m[..., 0] + jnp.log(den)
    return (o.reshape(n, nh, lp, dh)[:, :, :length], lse.reshape(n, nh, lp)[:, :, :length])


def _dilated_mixture(q, k, v):
    bsz, seq, nh, dh = q.shape
    outs, lses = [], []
    for window, dil in DILATED_PAIRS:
        length = seq // dil
        to_sub = lambda t: t.reshape(bsz, length, dil, nh, dh).transpose(0, 2, 3, 1, 4).reshape(bsz * dil, nh, length, dh)
        o, lse = _band_attention(to_sub(q), to_sub(k), to_sub(v), window // dil)
        outs.append(o.reshape(bsz, dil, nh, length, dh).transpose(0, 3, 1, 2, 4).reshape(bsz, seq, nh, dh))
        lses.append(lse.reshape(bsz, dil, nh, length).transpose(0, 3, 1, 2).reshape(bsz, seq, nh))
    wts = jax.nn.softmax(jnp.stack(lses), axis=0)
    return jnp.einsum('gbsh,gbshd->bshd', wts, jnp.stack(outs))


def setup_inputs(seed: int = 0) -> dict:
    key = jax.random.key(seed)
    ks = jax.random.split(key, 20)
    f32 = jnp.float32
    nrm = lambda k, shape, s: jax.random.normal(k, shape, f32) * s
    return {
        "x": nrm(ks[0], (BATCH, SEQ, D_MODEL), 1.0),
        "norm_g": 1.0 + nrm(ks[1], (DEPTH, D_MODEL), 0.02),
        "w_in": nrm(ks[2], (DEPTH, D_MODEL, PROJ_WIDTH), D_MODEL ** -0.5),
        "w_out": nrm(ks[3], (DEPTH, MIX_WIDTH, D_MODEL), MIX_WIDTH ** -0.5),
        "tshift_mu": jax.random.uniform(ks[4], (DEPTH, A_PROJ), f32),
        "decay_w0": jax.random.uniform(ks[5], (DEPTH, A_WIDTH), f32, minval=-6.0, maxval=-1.0),
        "decay_up": nrm(ks[6], (DEPTH, DECAY_LORA, A_WIDTH), 0.5 * DECAY_LORA ** -0.5),
        "iclr_a0": nrm(ks[7], (DEPTH, A_WIDTH), 0.1),
        "iclr_up": nrm(ks[8], (DEPTH, ICLR_LORA, A_WIDTH), 0.5 * ICLR_LORA ** -0.5),
        "k_k": 0.85 + nrm(ks[9], (DEPTH, A_WIDTH), 0.05),
        "k_a": 1.0 + nrm(ks[10], (DEPTH, A_WIDTH), 0.05),
        "r_k": nrm(ks[11], (DEPTH, A_HEADS, HEAD_DIM), 0.1),
        "lnx_g": 1.0 + nrm(ks[12], (DEPTH, A_WIDTH), 0.02),
        "lnx_b": nrm(ks[13], (DEPTH, A_WIDTH), 0.02),
        "moba_q_g": 1.0 + nrm(ks[14], (DEPTH, HEAD_DIM), 0.02),
        "moba_k_g": 1.0 + nrm(ks[15], (DEPTH, HEAD_DIM), 0.02),
        "dil_q_g": 1.0 + nrm(ks[16], (DEPTH, HEAD_DIM), 0.02),
        "dil_k_g": 1.0 + nrm(ks[17], (DEPTH, HEAD_DIM), 0.02),
    }


def reference(x, norm_g, w_in, w_out, tshift_mu, decay_w0, decay_up, iclr_a0, iclr_up,
              k_k, k_a, r_k, lnx_g, lnx_b, moba_q_g, moba_k_g, dil_q_g, dil_k_g):
    bsz, seq, _ = x.shape
    cuts = [A_PROJ, A_PROJ + B_PROJ, A_PROJ + B_PROJ + C_PROJ]
    for l in range(DEPTH):
        h = _rmsnorm(x, norm_g[l]).astype(x.dtype)
        p = h @ w_in[l]
        pa, pb, pc, gate = jnp.split(p, cuts, axis=-1)
        y_a = _rwkv7(pa, tshift_mu[l], decay_w0[l], decay_up[l], iclr_a0[l], iclr_up[l],
                     k_k[l], k_a[l], r_k[l], lnx_g[l], lnx_b[l])
        qb, kb, vb = _split_heads_qk_norm(pb, B_HEADS, moba_q_g[l], moba_k_g[l])
        y_b = _moba(qb, kb, vb).reshape(bsz, seq, B_WIDTH)
        qc, kc, vc = _split_heads_qk_norm(pc, C_HEADS, dil_q_g[l], dil_k_g[l])
        y_c = _dilated_mixture(qc, kc, vc).reshape(bsz, seq, C_WIDTH)
        y = jnp.concatenate([y_a, y_b, y_c], axis=-1) * jax.nn.silu(gate.astype(jnp.float32))
        x = x + y.astype(x.dtype) @ w_out[l]
    return x
```

```python
import functools

import jax
import jax.numpy as jnp
from jax import lax
from jax.experimental import pallas as pl
from jax.experimental.pallas import tpu as pltpu

F32 = jnp.float32
BF16 = jnp.bfloat16

HEAD_DIM = 64
LANES = 128
A_HEADS, B_HEADS, C_HEADS = 6, 4, 6
A_WIDTH, B_WIDTH, C_WIDTH = A_HEADS * HEAD_DIM, B_HEADS * HEAD_DIM, C_HEADS * HEAD_DIM
LORA = 64
A_PROJ = 3 * A_WIDTH + 2 * LORA
B_PROJ = 3 * B_WIDTH
C_PROJ = 3 * C_WIDTH
MIX_WIDTH = A_WIDTH + B_WIDTH + C_WIDTH
PROJ_WIDTH = A_PROJ + B_PROJ + C_PROJ + MIX_WIDTH
MOBA_BLOCK = 256
MOBA_TOPK = 3
DILATIONS = (1, 4, 16)
BAND = 128
RMS_EPS = 1e-6
LNX_EPS = HEAD_DIM * 1e-5
ATTN_SCALE = HEAD_DIM ** -0.5
RWKV_CHUNK = 64
NEG = -1e30
VMEM_LIMIT = 56 * 1024 * 1024


def _params(n_axes):
    return pltpu.CompilerParams(dimension_semantics=("arbitrary",) * n_axes,
                                vmem_limit_bytes=VMEM_LIMIT)


_NN = (((1,), (0,)), ((), ()))
_NT = (((1,), (1,)), ((), ()))
_TN = (((0,), (0,)), ((), ()))


def _mm(a, b, dims=_NN):
    return lax.dot_general(a, b, dims, preferred_element_type=F32)


def _split(x):
    hi = x.astype(BF16)
    lo = (x - hi.astype(F32)).astype(BF16)
    return hi, lo


def _mm3(a, b, dims=_NN):
    ah, al = _split(a)
    bh, bl = _split(b)
    return _mm(ah, bh, dims) + (_mm(ah, bl, dims) + _mm(al, bh, dims))


def _mm1(a, b, dims=_NN):
    return _mm(a.astype(BF16), b.astype(BF16), dims)


def _half_sum(x, is_a):
    sa = jnp.sum(jnp.where(is_a, x, 0.0), axis=-1, keepdims=True)
    sb = jnp.sum(jnp.where(is_a, 0.0, x), axis=-1, keepdims=True)
    return jnp.where(is_a, sa, sb)


def _lane_is_a(shape):
    return lax.broadcasted_iota(jnp.int32, shape, len(shape) - 1) < HEAD_DIM


def _inproj_kernel(x_ref, g_ref, w_ref, pa_ref, pb_ref, pc_ref, gt_ref):
    x = x_ref[...]
    ms = jnp.mean(x * x, axis=-1, keepdims=True)
    h = (x * lax.rsqrt(ms + RMS_EPS) * g_ref[...]).astype(BF16)
    lo = 0
    for ref in (pa_ref, pb_ref, pc_ref, gt_ref):
        hi = lo + ref.shape[-1]
        ref[...] = _mm(h, w_ref[:, lo:hi])
        lo = hi


def _inproj(x2d, g, w_bf16, tm=256):
    m, d = x2d.shape
    widths = (A_PROJ, B_PROJ, C_PROJ, MIX_WIDTH)
    return pl.pallas_call(
        _inproj_kernel,
        grid=(m // tm,),
        in_specs=[pl.BlockSpec((tm, d), lambda i: (i, 0)),
                  pl.BlockSpec((1, d), lambda i: (0, 0)),
                  pl.BlockSpec((d, PROJ_WIDTH), lambda i: (0, 0))],
        out_specs=[pl.BlockSpec((tm, w), lambda i: (i, 0)) for w in widths],
        out_shape=[jax.ShapeDtypeStruct((m, w), F32) for w in widths],
        compiler_params=_params(1),
        name="inproj",
    )(x2d, g.reshape(1, d), w_bf16)


def _outproj_kernel(x_ref, ya_ref, yb_ref, yc_ref, gt_ref, w_ref, o_ref):
    acc = x_ref[...]
    lo = 0
    for y_ref in (ya_ref, yb_ref, yc_ref):
        hi = lo + y_ref.shape[-1]
        g = gt_ref[:, lo:hi]
        y = y_ref[...] * (g * jax.nn.sigmoid(g))
        acc = acc + _mm(y.astype(BF16), w_ref[lo:hi, :])
        lo = hi
    o_ref[...] = acc


def _outproj(x2d, ya, yb, yc, gate, w_bf16, tm=512):
    m, d = x2d.shape
    row = lambda w: pl.BlockSpec((tm, w), lambda i: (i, 0))
    return pl.pallas_call(
        _outproj_kernel,
        grid=(m // tm,),
        in_specs=[row(d), row(A_WIDTH), row(B_WIDTH), row(C_WIDTH), row(MIX_WIDTH),
                  pl.BlockSpec((MIX_WIDTH, d), lambda i: (0, 0))],
        out_specs=row(d),
        out_shape=jax.ShapeDtypeStruct((m, d), F32),
        compiler_params=_params(1),
        name="outproj",
    )(x2d, ya, yb, yc, gate, w_bf16)


_P_MU_R, _P_MU_K, _P_MU_V, _P_MU_Z, _P_W0, _P_A0, _P_KK, _P_KA, _P_RK, _P_LNG, _P_LNB = range(11)
_P_ROWS = 16


def _rwkv_kernel(r_ref, k_ref, v_ref, z_ref, pp_ref, wz_ref, o_ref, st_ref):
    seq = r_ref.shape[1]
    c_len = RWKV_CHUNK
    two_c = 2 * c_len
    n_chunks = seq // c_len
    is_a = _lane_is_a((c_len, LANES))

    pp = pp_ref[0]
    prow = lambda i: pp[i:i + 1, :]
    mu_r, mu_k, mu_v, mu_z = prow(_P_MU_R), prow(_P_MU_K), prow(_P_MU_V), prow(_P_MU_Z)
    w0, a0, k_k, k_a, r_k = prow(_P_W0), prow(_P_A0), prow(_P_KK), prow(_P_KA), prow(_P_RK)
    ln_g, ln_b = prow(_P_LNG), prow(_P_LNB)
    wz = wz_ref[0]

    row_c = lax.broadcasted_iota(jnp.int32, (c_len, LANES), 0)
    ri = lax.broadcasted_iota(jnp.int32, (c_len, c_len), 0)
    ci = lax.broadcasted_iota(jnp.int32, (c_len, c_len), 1)
    tril_c = (ri >= ci).astype(BF16)
    r2 = lax.broadcasted_iota(jnp.int32, (two_c, two_c), 0)
    c2 = lax.broadcasted_iota(jnp.int32, (two_c, two_c), 1)
    same = (r2 // c_len) == (c2 // c_len)
    m_strict = same & ((r2 % c_len) > (c2 % c_len))
    m_incl = same & ((r2 % c_len) >= (c2 % c_len))
    eye2 = (r2 == c2).astype(F32)

    def stack(x):
        return jnp.concatenate([jnp.where(is_a, x, 0.0), jnp.where(is_a, 0.0, x)], axis=0)

    st_ref[...] = jnp.zeros_like(st_ref)

    def body(c, carry):
        pr, pk, pv, pz = carry
        t0 = pl.multiple_of(c * c_len, c_len)

        def shifted(ref, prev_row, mu):
            x = ref[0, pl.ds(t0, c_len), :]
            prev = jnp.where(row_c == 0, prev_row, pltpu.roll(x, 1, axis=0))
            return x + (prev - x) * mu, x[c_len - 1:c_len, :]

        r, nr = shifted(r_ref, pr, mu_r)
        k, nk = shifted(k_ref, pk, mu_k)
        v, nv = shifted(v_ref, pv, mu_v)
        z, nz = shifted(z_ref, pz, mu_z)

        lora = _mm3(jnp.where(is_a, jnp.tanh(z), z), wz)
        w = -jax.nn.softplus(-(w0 + lora[:, :LANES])) - 0.5
        lw = -jnp.exp(w)
        a = jax.nn.sigmoid(a0 + lora[:, LANES:])
        kk = k * k_k
        kk = kk * lax.rsqrt(_half_sum(kk * kk, is_a) + 1e-12)
        k2 = k * (1.0 + (a - 1.0) * k_a)
        kka = kk * a

        l1 = lw.astype(BF16)
        rem = lw - l1.astype(F32)
        l2 = rem.astype(BF16)
        l3 = (rem - l2.astype(F32)).astype(BF16)
        g = _mm(tril_c, l1) + (_mm(tril_c, l2) + _mm(tril_c, l3))
        g_end = g[c_len - 1:c_len, :]
        e_pos = jnp.exp(g)
        e_neg = jnp.exp(-g)
        e_prev = jnp.exp(g - lw)
        e_tail = jnp.exp(g_end - g)

        ab2 = stack(-kk * e_prev)
        rb2 = stack(r * e_pos)
        bt2 = stack(kka * e_neg)
        kt2 = stack(k2 * e_neg)
        bp2 = stack(kka * e_tail)
        kp2 = stack(k2 * e_tail)
        v2 = stack(v)

        mm = _mm3(jnp.concatenate([ab2, rb2], axis=0), jnp.concatenate([bt2, kt2], axis=0), _NT)
        l_b = jnp.where(m_strict, mm[:two_c, :two_c], 0.0)
        l_k = jnp.where(m_strict, mm[:two_c, two_c:], 0.0)
        r_b = jnp.where(m_incl, mm[two_c:, :two_c], 0.0)
        r_k2 = jnp.where(m_incl, mm[two_c:, two_c:], 0.0)

        t_inv = eye2 + l_b
        p = _mm3(l_b, l_b)
        steps = c_len.bit_length() - 2
        for i in range(steps):
            if i + 1 < steps:
                tp = _mm3(jnp.concatenate([t_inv, p], axis=0), p)
                t_inv = t_inv + tp[:two_c]
                p = tp[two_c:]
            else:
                t_inv = t_inv + _mm3(t_inv, p)

        kv = _mm3(l_k, v2)
        tw = _mm3(t_inv, jnp.concatenate([ab2, kv], axis=1))
        sv = st_ref[...]
        ws = _mm3(jnp.concatenate([tw[:, :LANES], rb2], axis=0), sv, _NT)
        u2 = ws[:two_c] + tw[:, LANES:]
        uv = jnp.concatenate([u2, v2], axis=0)
        y2 = ws[two_c:] + _mm3(jnp.concatenate([r_b, r_k2], axis=1), uv)
        st_ref[...] = sv * jnp.exp(g_end) + _mm3(uv, jnp.concatenate([bp2, kp2], axis=0), _TN)

        y = y2[:c_len] + y2[c_len:]
        mean = _half_sum(y, is_a) * (1.0 / HEAD_DIM)
        yc = y - mean
        var = _half_sum(yc * yc, is_a) * (1.0 / HEAD_DIM)
        y = yc * lax.rsqrt(var + LNX_EPS) * ln_g + ln_b
        y = y + _half_sum(r * k2 * r_k, is_a) * v
        o_ref[0, pl.ds(t0, c_len), :] = y
        return nr, nk, nv, nz

    zero_row = jnp.zeros((1, LANES), F32)
    lax.fori_loop(0, n_chunks, body, (zero_row, zero_row, zero_row, zero_row))


def _rwkv(pa, pp, wz):
    bsz, seq, _ = pa.shape
    n_pairs = A_HEADS // 2
    slab = lambda off: pl.BlockSpec((1, seq, LANES), lambda b, j: (b, 0, off + j))
    z_spec = pl.BlockSpec((1, seq, LANES), lambda b, j: (b, 0, 3 * n_pairs))
    return pl.pallas_call(
        _rwkv_kernel,
        grid=(bsz, n_pairs),
        in_specs=[slab(0), slab(n_pairs), slab(2 * n_pairs), z_spec,
                  pl.BlockSpec((1, _P_ROWS, LANES), lambda b, j: (j, 0, 0)),
                  pl.BlockSpec((1, LANES, 2 * LANES), lambda b, j: (j, 0, 0))],
        out_specs=pl.BlockSpec((1, seq, LANES), lambda b, j: (b, 0, j)),
        out_shape=jax.ShapeDtypeStruct((bsz, seq, A_WIDTH), F32),
        scratch_shapes=[pltpu.VMEM((LANES, LANES), F32)],
        compiler_params=_params(2),
        name="rwkv",
    )(pa, pa, pa, pa, pp, wz)


def _rwkv_pack_params(mu, w0, w_up, a0, a_up, k_k, k_a, r_k, ln_g, ln_b):
    n_pairs = A_HEADS // 2
    pair = lambda t, j: t[j * LANES:(j + 1) * LANES]
    pps, wzs = [], []
    zeros = jnp.zeros((LORA, LANES), F32)
    for j in range(n_pairs):
        rows = [pair(mu[:A_WIDTH], j), pair(mu[A_WIDTH:2 * A_WIDTH], j), pair(mu[2 * A_WIDTH:3 * A_WIDTH], j),
                mu[3 * A_WIDTH:], pair(w0, j), pair(a0, j), pair(k_k, j), pair(k_a, j),
                pair(r_k.reshape(-1), j), pair(ln_g, j), pair(ln_b, j)]
        rows += [jnp.zeros((LANES,), F32)] * (_P_ROWS - len(rows))
        pps.append(jnp.stack(rows))
        w_cols = jnp.concatenate([w_up[:, j * LANES:(j + 1) * LANES], zeros], axis=0)
        a_cols = jnp.concatenate([zeros, a_up[:, j * LANES:(j + 1) * LANES]], axis=0)
        wzs.append(jnp.concatenate([w_cols, a_cols], axis=1))
    return jnp.stack(pps), jnp.stack(wzs)


def _qk_norm_into(src_ref, dst_ref, gain, scale, tile=512):
    seq = dst_ref.shape[0]
    is_a = _lane_is_a((tile, LANES))

    def body(i, _):
        t0 = pl.multiple_of(i * tile, tile)
        x = src_ref[0, pl.ds(t0, tile), :]
        ms = _half_sum(x * x, is_a) * (1.0 / HEAD_DIM)
        dst_ref[pl.ds(t0, tile), :] = x * lax.rsqrt(ms + RMS_EPS) * (gain * scale)
        return 0

    lax.fori_loop(0, seq // tile, body, 0)


def _moba_kernel(q_ref, k_ref, v_ref, gq_ref, gk_ref, o_ref, qn_ref, kn_ref, km_ref, bias_ref):
    seq = q_ref.shape[1]
    blk = MOBA_BLOCK
    nb = seq // blk
    _qk_norm_into(q_ref, qn_ref, gq_ref[...], 1.0)
    _qk_norm_into(k_ref, kn_ref, gk_ref[...], 1.0)

    for n in range(nb):
        km_ref[n:n + 1, :] = jnp.mean(kn_ref[n * blk:(n + 1) * blk, :], axis=0, keepdims=True)

    is_a = _lane_is_a((blk, LANES))
    col = lax.broadcasted_iota(jnp.int32, (blk, nb), 1)
    qpos = lax.broadcasted_iota(jnp.int32, (blk, blk), 0)
    kpos = lax.broadcasted_iota(jnp.int32, (blk, blk), 1)
    causal = kpos <= qpos

    def q_tile(qt, _):
        t0 = pl.multiple_of(qt * blk, blk)
        qn = qn_ref[pl.ds(t0, blk), :]
        k_own = kn_ref[pl.ds(t0, blk), :].astype(BF16)
        v_own = v_ref[0, pl.ds(t0, blk), :].astype(BF16)
        outs = []
        for head_a in (True, False):
            sel_lane = is_a if head_a else jnp.logical_not(is_a)
            qh = jnp.where(sel_lane, qn, 0.0)

            gate = _mm3(qh, km_ref[...], _NT)
            gate = jnp.where(col < qt, gate, -jnp.inf)
            sel = jnp.zeros((blk, nb), jnp.bool_)
            for _ in range(MOBA_TOPK):
                top = jnp.max(gate, axis=-1, keepdims=True)
                hit = (gate == top) & (top > -jnp.inf)
                first = jnp.min(jnp.where(hit, col, nb), axis=-1, keepdims=True)
                pick = col == first
                sel = sel | pick
                gate = jnp.where(pick, -jnp.inf, gate)
            sel_bias = jnp.where(sel, 0.0, NEG)
            for n in range(nb):
                bias_ref[n] = jnp.broadcast_to(sel_bias[:, n:n + 1], (blk, LANES))

            qs = (qh * ATTN_SCALE).astype(BF16)
            s = jnp.where(causal, _mm(qs, k_own, _NT), NEG)
            m = jnp.max(s, axis=-1, keepdims=True)
            p = jnp.exp(s - m)
            l = jnp.sum(p, axis=-1, keepdims=True)
            acc = _mm(p.astype(BF16), v_own)

            def kv_block(n, carry):
                m, l, acc = carry
                k0 = pl.multiple_of(n * blk, blk)
                kb = kn_ref[pl.ds(k0, blk), :].astype(BF16)
                vb = v_ref[0, pl.ds(k0, blk), :].astype(BF16)
                bias = bias_ref[n]
                s = _mm(qs, kb, _NT) + jnp.concatenate([bias, bias], axis=1)
                m_new = jnp.maximum(m, jnp.max(s, axis=-1, keepdims=True))
                alpha = jnp.exp(m - m_new)
                p = jnp.exp(s - m_new)
                l = alpha * l + jnp.sum(p, axis=-1, keepdims=True)
                acc = alpha * acc + _mm(p.astype(BF16), vb)
                return m_new, l, acc

            m, l, acc = lax.fori_loop(0, qt, kv_block, (m, l, acc))
            outs.append(acc / l)
        o_ref[0, pl.ds(t0, blk), :] = jnp.where(is_a, outs[0], outs[1])
        return 0

    lax.fori_loop(0, nb, q_tile, 0)


def _moba(pb, gq, gk):
    bsz, seq, _ = pb.shape
    n_pairs = B_HEADS // 2
    nb = seq // MOBA_BLOCK
    slab = lambda off: pl.BlockSpec((1, seq, LANES), lambda b, j: (b, 0, off + j))
    gain = pl.BlockSpec((1, LANES), lambda b, j: (0, 0))
    return pl.pallas_call(
        _moba_kernel,
        grid=(bsz, n_pairs),
        in_specs=[slab(0), slab(n_pairs), slab(2 * n_pairs), gain, gain],
        out_specs=pl.BlockSpec((1, seq, LANES), lambda b, j: (b, 0, j)),
        out_shape=jax.ShapeDtypeStruct((bsz, seq, B_WIDTH), F32),
        scratch_shapes=[pltpu.VMEM((seq, LANES), F32), pltpu.VMEM((seq, LANES), F32),
                        pltpu.VMEM((nb, LANES), F32), pltpu.VMEM((nb, MOBA_BLOCK, LANES), F32)],
        compiler_params=_params(2),
        name="moba",
    )(pb, pb, pb, jnp.tile(gq, 2).reshape(1, LANES), jnp.tile(gk, 2).reshape(1, LANES))


def _dilated_kernel(q_ref, k_ref, v_ref, gq_ref, gk_ref, o_ref,
                    qn_ref, kn_ref, vv_ref, acc_ref, ma_ref, mb_ref, la_ref, lb_ref):
    seq = q_ref.shape[1]
    _qk_norm_into(q_ref, qn_ref, gq_ref[...], ATTN_SCALE)
    _qk_norm_into(k_ref, kn_ref, gk_ref[...], 1.0)
    vv_ref[...] = v_ref[0]
    acc_ref[...] = jnp.zeros_like(acc_ref)
    for m_ref, l_ref in ((ma_ref, la_ref), (mb_ref, lb_ref)):
        m_ref[...] = jnp.full_like(m_ref, NEG)
        l_ref[...] = jnp.zeros_like(l_ref)

    is_a = _lane_is_a((BAND, LANES))
    qi = lax.broadcasted_iota(jnp.int32, (BAND, 2 * BAND), 0)
    kj = lax.broadcasted_iota(jnp.int32, (BAND, 2 * BAND), 1)
    band_two = (kj >= qi) & (kj <= qi + BAND)
    q1 = lax.broadcasted_iota(jnp.int32, (BAND, BAND), 0)
    k1 = lax.broadcasted_iota(jnp.int32, (BAND, BAND), 1)
    band_one = k1 <= q1

    def block(q_rows, k_rows, mask):
        q = qn_ref[q_rows, :]
        kb = kn_ref[k_rows, :].astype(BF16)
        vb = vv_ref[k_rows, :].astype(BF16)
        n_rep = mask.shape[1] // LANES
        alphas, pvs = [], []
        for sel_lane, m_ref, l_ref in ((is_a, ma_ref, la_ref), (jnp.logical_not(is_a), mb_ref, lb_ref)):
            qh = jnp.where(sel_lane, q, 0.0).astype(BF16)
            s = jnp.where(mask, _mm(qh, kb, _NT), NEG)
            m_old = m_ref[q_rows, :]
            m_new = jnp.maximum(m_old, jnp.max(s, axis=-1, keepdims=True))
            alpha = jnp.exp(m_old - m_new)
            p = jnp.exp(s - jnp.concatenate([m_new] * n_rep, axis=1))
            l_ref[q_rows, :] = alpha * l_ref[q_rows, :] + jnp.sum(p, axis=-1, keepdims=True)
            m_ref[q_rows, :] = m_new
            alphas.append(alpha)
            pvs.append(_mm(p.astype(BF16), vb))
        acc_ref[q_rows, :] = (jnp.where(is_a, alphas[0], alphas[1]) * acc_ref[q_rows, :]
                              + jnp.where(is_a, pvs[0], pvs[1]))

    for dil in DILATIONS:
        n_blocks = seq // (dil * BAND)
        stride = None if dil == 1 else dil

        def residue(r, _, dil=dil, n_blocks=n_blocks, stride=stride):
            block(pl.ds(r, BAND, stride=stride), pl.ds(r, BAND, stride=stride), band_one)

            def later(c, _):
                base = r + c * (BAND * dil)
                block(pl.ds(base, BAND, stride=stride),
                      pl.ds(base - BAND * dil, 2 * BAND, stride=stride), band_two)
                return 0

            lax.fori_loop(1, n_blocks, later, 0)
            return 0

        lax.fori_loop(0, dil, residue, 0)

    o_ref[0] = acc_ref[...] / jnp.where(_lane_is_a((seq, LANES)), la_ref[...], lb_ref[...])


def _dilated(pc, gq, gk):
    bsz, seq, _ = pc.shape
    n_pairs = C_HEADS // 2
    slab = lambda off: pl.BlockSpec((1, seq, LANES), lambda b, j: (b, 0, off + j))
    gain = pl.BlockSpec((1, LANES), lambda b, j: (0, 0))
    full = pltpu.VMEM((seq, LANES), F32)
    return pl.pallas_call(
        _dilated_kernel,
        grid=(bsz, n_pairs),
        in_specs=[slab(0), slab(n_pairs), slab(2 * n_pairs), gain, gain],
        out_specs=pl.BlockSpec((1, seq, LANES), lambda b, j: (b, 0, j)),
        out_shape=jax.ShapeDtypeStruct((bsz, seq, C_WIDTH), F32),
        scratch_shapes=[full] * 8,
        compiler_params=_params(2),
        name="dilated",
    )(pc, pc, pc, jnp.tile(gq, 2).reshape(1, LANES), jnp.tile(gk, 2).reshape(1, LANES))


def kernel(x, norm_g, w_in, w_out, tshift_mu, decay_w0, decay_up, iclr_a0, iclr_up,
           k_k, k_a, r_k, lnx_g, lnx_b, moba_q_g, moba_k_g, dil_q_g, dil_k_g):
    bsz, seq, d_model = x.shape
    depth = norm_g.shape[0]
    assert seq % (max(DILATIONS) * BAND) == 0 and seq % MOBA_BLOCK == 0 and seq % RWKV_CHUNK == 0
    x2d = x.reshape(bsz * seq, d_model)
    w_in_bf = w_in.astype(BF16)
    w_out_bf = w_out.astype(BF16)
    for l in range(depth):
        pa, pb, pc, gate = _inproj(x2d, norm_g[l], w_in_bf[l])
        pp, wz = _rwkv_pack_params(tshift_mu[l], decay_w0[l], decay_up[l], iclr_a0[l], iclr_up[l],
                                   k_k[l], k_a[l], r_k[l], lnx_g[l], lnx_b[l])
        ya = _rwkv(pa.reshape(bsz, seq, A_PROJ), pp, wz)
        yb = _moba(pb.reshape(bsz, seq, B_PROJ), moba_q_g[l], moba_k_g[l])
        yc = _dilated(pc.reshape(bsz, seq, C_PROJ), dil_q_g[l], dil_k_g[l])
        x2d = _outproj(x2d, ya.reshape(bsz * seq, A_WIDTH), yb.reshape(bsz * seq, B_WIDTH),
                       yc.reshape(bsz * seq, C_WIDTH), gate, w_out_bf[l])
    return x2d.reshape(bsz, seq, d_model)
```

```python
import functools

import jax
import jax.numpy as jnp
from jax import lax
from jax.experimental import pallas as pl
from jax.experimental.pallas import tpu as pltpu

F32 = jnp.float32
BF16 = jnp.bfloat16

HEAD_DIM = 64
LANES = 128
A_HEADS, B_HEADS, C_HEADS = 6, 4, 6
A_WIDTH, B_WIDTH, C_WIDTH = A_HEADS * HEAD_DIM, B_HEADS * HEAD_DIM, C_HEADS * HEAD_DIM
LORA = 64
A_PROJ = 3 * A_WIDTH + 2 * LORA
B_PROJ = 3 * B_WIDTH
C_PROJ = 3 * C_WIDTH
MIX_WIDTH = A_WIDTH + B_WIDTH + C_WIDTH
PROJ_WIDTH = A_PROJ + B_PROJ + C_PROJ + MIX_WIDTH
MOBA_BLOCK = 256
MOBA_TOPK = 3
DILATIONS = (1, 4, 16)
BAND = 128
RMS_EPS = 1e-6
LNX_EPS = HEAD_DIM * 1e-5
ATTN_SCALE = HEAD_DIM ** -0.5
RWKV_CHUNK = 64
NEG = -1e30
VMEM_LIMIT = 56 * 1024 * 1024


def _params(n_axes):
    return pltpu.CompilerParams(dimension_semantics=("arbitrary",) * n_axes,
                                vmem_limit_bytes=VMEM_LIMIT)


_NN = (((1,), (0,)), ((), ()))
_NT = (((1,), (1,)), ((), ()))
_TN = (((0,), (0,)), ((), ()))


def _mm(a, b, dims=_NN):
    return lax.dot_general(a, b, dims, preferred_element_type=F32)


def _split(x):
    hi = x.astype(BF16)
    lo = (x - hi.astype(F32)).astype(BF16)
    return hi, lo


def _mm3(a, b, dims=_NN):
    ah, al = _split(a)
    bh, bl = _split(b)
    return _mm(ah, bh, dims) + (_mm(ah, bl, dims) + _mm(al, bh, dims))


def _mm1(a, b, dims=_NN):
    return _mm(a.astype(BF16), b.astype(BF16), dims)


def _half_sum(x, is_a):
    sa = jnp.sum(jnp.where(is_a, x, 0.0), axis=-1, keepdims=True)
    sb = jnp.sum(jnp.where(is_a, 0.0, x), axis=-1, keepdims=True)
    return jnp.where(is_a, sa, sb)


def _round_robin(chains):
    results = [None] * len(chains)
    live = list(range(len(chains)))
    while live:
        for i in list(live):
            try:
                next(chains[i])
            except StopIteration as done:
                results[i] = done.value
                live.remove(i)
    return results


def _lane_is_a(shape):
    return lax.broadcasted_iota(jnp.int32, shape, len(shape) - 1) < HEAD_DIM


def _inproj_kernel(x_ref, g_ref, w_ref, pa_ref, pb_ref, pc_ref, gt_ref):
    x = x_ref[...]
    ms = jnp.mean(x * x, axis=-1, keepdims=True)
    h = (x * lax.rsqrt(ms + RMS_EPS) * g_ref[...]).astype(BF16)
    lo = 0
    for ref in (pa_ref, pb_ref, pc_ref, gt_ref):
        hi = lo + ref.shape[-1]
        ref[...] = _mm(h, w_ref[:, lo:hi])
        lo = hi


def _inproj(x2d, g, w_bf16, tm=256):
    m, d = x2d.shape
    widths = (A_PROJ, B_PROJ, C_PROJ, MIX_WIDTH)
    return pl.pallas_call(
        _inproj_kernel,
        grid=(m // tm,),
        in_specs=[pl.BlockSpec((tm, d), lambda i: (i, 0)),
                  pl.BlockSpec((1, d), lambda i: (0, 0)),
                  pl.BlockSpec((d, PROJ_WIDTH), lambda i: (0, 0))],
        out_specs=[pl.BlockSpec((tm, w), lambda i: (i, 0)) for w in widths],
        out_shape=[jax.ShapeDtypeStruct((m, w), F32) for w in widths],
        compiler_params=_params(1),
        name="inproj",
    )(x2d, g.reshape(1, d), w_bf16)


def _outproj_kernel(x_ref, ya_ref, yb_ref, yc_ref, gt_ref, w_ref, o_ref):
    acc = x_ref[...]
    lo = 0
    for y_ref in (ya_ref, yb_ref, yc_ref):
        hi = lo + y_ref.shape[-1]
        g = gt_ref[:, lo:hi]
        y = y_ref[...] * (g * jax.nn.sigmoid(g))
        acc = acc + _mm(y.astype(BF16), w_ref[lo:hi, :])
        lo = hi
    o_ref[...] = acc


def _outproj(x2d, ya, yb, yc, gate, w_bf16, tm=512):
    m, d = x2d.shape
    row = lambda w: pl.BlockSpec((tm, w), lambda i: (i, 0))
    return pl.pallas_call(
        _outproj_kernel,
        grid=(m // tm,),
        in_specs=[row(d), row(A_WIDTH), row(B_WIDTH), row(C_WIDTH), row(MIX_WIDTH),
                  pl.BlockSpec((MIX_WIDTH, d), lambda i: (0, 0))],
        out_specs=row(d),
        out_shape=jax.ShapeDtypeStruct((m, d), F32),
        compiler_params=_params(1),
        name="outproj",
    )(x2d, ya, yb, yc, gate, w_bf16)


_P_W0, _P_A0, _P_KK, _P_KA, _P_RK, _P_LNG, _P_LNB = range(7)
_P_ROWS = 8


def _rwkv_kernel(pa_ref, mu_ref, pp_ref, wz_ref, o_ref, st_ref, prev_ref):
    tile = pa_ref.shape[1]
    c_len = RWKV_CHUNK
    two_c = 2 * c_len
    n_chunks = tile // c_len
    n_pairs = A_HEADS // 2
    is_a = _lane_is_a((c_len, LANES))

    row_c = lax.broadcasted_iota(jnp.int32, (c_len, A_PROJ), 0)
    ri = lax.broadcasted_iota(jnp.int32, (c_len, c_len), 0)
    ci = lax.broadcasted_iota(jnp.int32, (c_len, c_len), 1)
    tril_c = (ri >= ci).astype(BF16)
    r2 = lax.broadcasted_iota(jnp.int32, (two_c, two_c), 0)
    c2 = lax.broadcasted_iota(jnp.int32, (two_c, two_c), 1)
    same = (r2 // c_len) == (c2 // c_len)
    m_strict = same & ((r2 % c_len) > (c2 % c_len))
    m_incl = same & ((r2 % c_len) >= (c2 % c_len))
    eye2 = (r2 == c2).astype(F32)

    def stack(x):
        return jnp.concatenate([jnp.where(is_a, x, 0.0), jnp.where(is_a, 0.0, x)], axis=0)

    @pl.when(pl.program_id(1) == 0)
    def _():
        st_ref[...] = jnp.zeros_like(st_ref)
        prev_ref[...] = jnp.zeros_like(prev_ref)

    def pair_chunk(j, r, k, v, lora_w, lora_a):
        pp = pp_ref[j]
        prow = lambda i: pp[i:i + 1, :]
        w0, a0, k_k, k_a, r_k = prow(_P_W0), prow(_P_A0), prow(_P_KK), prow(_P_KA), prow(_P_RK)
        ln_g, ln_b = prow(_P_LNG), prow(_P_LNB)
        w = -jax.nn.softplus(-(w0 + lora_w)) - 0.5
        lw = -jnp.exp(w)
        a = jax.nn.sigmoid(a0 + lora_a)
        kk = k * k_k
        kk = kk * lax.rsqrt(_half_sum(kk * kk, is_a) + 1e-12)
        k2 = k * (1.0 + (a - 1.0) * k_a)
        kka = kk * a

        l1 = lw.astype(BF16)
        rem = lw - l1.astype(F32)
        l2 = rem.astype(BF16)
        l3 = (rem - l2.astype(F32)).astype(BF16)
        g = _mm(tril_c, l1) + (_mm(tril_c, l2) + _mm(tril_c, l3))
        yield
        g_end = g[c_len - 1:c_len, :]
        e_pos = jnp.exp(g)
        e_neg = jnp.exp(-g)
        e_prev = jnp.exp(g - lw)
        e_tail = jnp.exp(g_end - g)

        ab2 = stack(-kk * e_prev)
        rb2 = stack(r * e_pos)
        bt2 = stack(kka * e_neg)
        kt2 = stack(k2 * e_neg)
        bp2 = stack(kka * e_tail)
        kp2 = stack(k2 * e_tail)
        v2 = stack(v)

        mm = _mm3(jnp.concatenate([ab2, rb2], axis=0), jnp.concatenate([bt2, kt2], axis=0), _NT)
        yield
        l_b = jnp.where(m_strict, mm[:two_c, :two_c], 0.0)
        l_k = jnp.where(m_strict, mm[:two_c, two_c:], 0.0)
        r_b = jnp.where(m_incl, mm[two_c:, :two_c], 0.0)
        r_k2 = jnp.where(m_incl, mm[two_c:, two_c:], 0.0)

        t_inv = eye2 + l_b
        p = _mm3(l_b, l_b)
        kv = _mm3(l_k, v2)
        yield
        steps = c_len.bit_length() - 2
        for i in range(steps):
            if i + 1 < steps:
                tp = _mm3(jnp.concatenate([t_inv, p], axis=0), p)
                yield
                t_inv = t_inv + tp[:two_c]
                p = tp[two_c:]
            else:
                tp = _mm3(t_inv, p)
                yield
                t_inv = t_inv + tp

        tw = _mm3(t_inv, jnp.concatenate([ab2, kv], axis=1))
        yield
        sv = st_ref[j]
        ws = _mm3(jnp.concatenate([tw[:, :LANES], rb2], axis=0), sv, _NT)
        yield
        u2 = ws[:two_c] + tw[:, LANES:]
        uv = jnp.concatenate([u2, v2], axis=0)
        y2 = ws[two_c:] + _mm3(jnp.concatenate([r_b, r_k2], axis=1), uv)
        st_ref[j] = sv * jnp.exp(g_end) + _mm3(uv, jnp.concatenate([bp2, kp2], axis=0), _TN)
        yield

        y = y2[:c_len] + y2[c_len:]
        mean = _half_sum(y, is_a) * (1.0 / HEAD_DIM)
        yc = y - mean
        var = _half_sum(yc * yc, is_a) * (1.0 / HEAD_DIM)
        y = yc * lax.rsqrt(var + LNX_EPS) * ln_g + ln_b
        return y + _half_sum(r * k2 * r_k, is_a) * v

    mu = mu_ref[...]
    wz = wz_ref[...]

    def body(c, prev_row):
        t0 = pl.multiple_of(c * c_len, c_len)
        x = pa_ref[0, pl.ds(t0, c_len), :]
        prev = jnp.where(row_c == 0, prev_row, pltpu.roll(x, 1, axis=0))
        xs = x + (prev - x) * mu
        slab = lambda i: xs[:, i * LANES:(i + 1) * LANES]
        z = slab(3 * n_pairs)
        lora = _mm3(jnp.where(is_a, jnp.tanh(z), z), wz)
        chains = [pair_chunk(j, slab(j), slab(n_pairs + j), slab(2 * n_pairs + j),
                             lora[:, 2 * j * LANES:(2 * j + 1) * LANES],
                             lora[:, (2 * j + 1) * LANES:(2 * j + 2) * LANES])
                  for j in range(n_pairs)]
        for j, y in enumerate(_round_robin(chains)):
            o_ref[0, pl.ds(t0, c_len), j * LANES:(j + 1) * LANES] = y
        return x[c_len - 1:c_len, :]

    prev_ref[0:1, :] = lax.fori_loop(0, n_chunks, body, prev_ref[0:1, :])


def _rwkv(pa, mu, pp, wz, tile=1024):
    bsz, seq, _ = pa.shape
    n_pairs = A_HEADS // 2
    whole = lambda shape: pl.BlockSpec(shape, lambda b, s: (0,) * len(shape))
    return pl.pallas_call(
        _rwkv_kernel,
        grid=(bsz, seq // tile),
        in_specs=[pl.BlockSpec((1, tile, A_PROJ), lambda b, s: (b, s, 0)),
                  whole((1, A_PROJ)), whole((n_pairs, _P_ROWS, LANES)), whole((LANES, n_pairs * 2 * LANES))],
        out_specs=pl.BlockSpec((1, tile, A_WIDTH), lambda b, s: (b, s, 0)),
        out_shape=jax.ShapeDtypeStruct((bsz, seq, A_WIDTH), F32),
        scratch_shapes=[pltpu.VMEM((n_pairs, LANES, LANES), F32), pltpu.VMEM((8, A_PROJ), F32)],
        compiler_params=_params(2),
        name="rwkv",
    )(pa, mu.reshape(1, A_PROJ), pp, wz)


def _rwkv_pack_params(w0, w_up, a0, a_up, k_k, k_a, r_k, ln_g, ln_b):
    n_pairs = A_HEADS // 2
    pair = lambda t, j: t[j * LANES:(j + 1) * LANES]
    pps, wzs = [], []
    zeros = jnp.zeros((LORA, LANES), F32)
    for j in range(n_pairs):
        rows = {_P_W0: w0, _P_A0: a0, _P_KK: k_k, _P_KA: k_a, _P_RK: r_k.reshape(-1), _P_LNG: ln_g, _P_LNB: ln_b}
        pps.append(jnp.stack([pair(rows[i], j) if i in rows else jnp.zeros((LANES,), F32)
                              for i in range(_P_ROWS)]))
        wzs.append(jnp.concatenate([w_up[:, j * LANES:(j + 1) * LANES], zeros], axis=0))
        wzs.append(jnp.concatenate([zeros, a_up[:, j * LANES:(j + 1) * LANES]], axis=0))
    return jnp.stack(pps), jnp.concatenate(wzs, axis=1)


def _qk_norm_into(src_ref, dst_ref, gain, scale, tile=512):
    seq = dst_ref.shape[0]
    is_a = _lane_is_a((tile, LANES))

    def body(i, _):
        t0 = pl.multiple_of(i * tile, tile)
        x = src_ref[0, pl.ds(t0, tile), :]
        ms = _half_sum(x * x, is_a) * (1.0 / HEAD_DIM)
        dst_ref[pl.ds(t0, tile), :] = x * lax.rsqrt(ms + RMS_EPS) * (gain * scale)
        return 0

    lax.fori_loop(0, seq // tile, body, 0)


def _moba_kernel(q_ref, k_ref, v_ref, gq_ref, gk_ref, o_ref,
                 qn_ref, kb_ref, qs_ref, km_ref, vt_ref, bias_ref):
    seq = q_ref.shape[1]
    blk = MOBA_BLOCK
    nb = seq // blk
    ones_rows = vt_ref.shape[2] - HEAD_DIM
    is_a = _lane_is_a((blk, LANES))
    gq, gk = gq_ref[...], gk_ref[...]

    def prepare(n, _):
        t0 = pl.multiple_of(n * blk, blk)
        rows = pl.ds(t0, blk)
        q = q_ref[0, rows, :]
        qn = q * lax.rsqrt(_half_sum(q * q, is_a) * (1.0 / HEAD_DIM) + RMS_EPS) * gq
        qn_ref[rows, :] = qn
        qs = qn * ATTN_SCALE
        qs_ref[0, rows, :] = jnp.where(is_a, qs, 0.0).astype(BF16)
        qs_ref[1, rows, :] = jnp.where(is_a, 0.0, qs).astype(BF16)
        k = k_ref[0, rows, :]
        kn = k * lax.rsqrt(_half_sum(k * k, is_a) * (1.0 / HEAD_DIM) + RMS_EPS) * gk
        kb_ref[rows, :] = kn.astype(BF16)
        km = jnp.mean(kn, axis=0, keepdims=True)
        km_ref[0, pl.ds(n, 1), :] = jnp.where(is_a[:1], km, 0.0)
        km_ref[1, pl.ds(n, 1), :] = jnp.where(is_a[:1], 0.0, km)
        vt = v_ref[0, rows, :].T.astype(BF16)
        ones = jnp.ones((ones_rows, blk), BF16)
        for h in range(2):
            vt_ref[h, n, :HEAD_DIM, :] = vt[h * HEAD_DIM:(h + 1) * HEAD_DIM]
            vt_ref[h, n, HEAD_DIM:, :] = ones
        return 0

    lax.fori_loop(0, nb, prepare, 0)

    blk_row = lax.broadcasted_iota(jnp.int32, (nb, blk), 0)
    key_pos = lax.broadcasted_iota(jnp.int32, (blk, blk), 0)
    qry_pos = lax.broadcasted_iota(jnp.int32, (blk, blk), 1)
    causal = key_pos <= qry_pos

    def scores(n, h, qs):
        kb = kb_ref[pl.ds(pl.multiple_of(n * blk, blk), blk), :]
        return _mm(kb, qs[h], _NT) + bias_ref[h, pl.ds(n, 1), :]

    def q_tile(qt, _):
        t0 = pl.multiple_of(qt * blk, blk)
        rows = pl.ds(t0, blk)
        qn = qn_ref[rows, :]
        for h in range(2):
            gate = _mm3(km_ref[h], qn, _NT)
            gate = jnp.where(blk_row < qt, gate, -jnp.inf)
            bias = jnp.full((nb, blk), NEG, F32)
            for _ in range(MOBA_TOPK):
                top = jnp.max(gate, axis=0, keepdims=True)
                hit = (gate == top) & (top > -jnp.inf)
                first = jnp.min(jnp.where(hit, blk_row, nb), axis=0, keepdims=True)
                pick = blk_row == first
                bias = jnp.where(pick, 0.0, bias)
                gate = jnp.where(pick, -jnp.inf, gate)
            bias_ref[h] = bias

        qs = [qs_ref[h, rows, :] for h in range(2)]
        n_pairs = (qt + 1) // 2
        s_own = [jnp.where(causal, _mm(kb_ref[rows, :], qs[h], _NT), NEG) for h in range(2)]
        m_own = tuple(jnp.max(s, axis=0, keepdims=True) for s in s_own)

        def max_pair(i, m):
            s = [[scores(n, h, qs) for h in range(2)] for n in (2 * i, 2 * i + 1)]
            return tuple(jnp.maximum(m[h], jnp.max(jnp.maximum(s[0][h], s[1][h]), axis=0, keepdims=True))
                         for h in range(2))

        m = lax.fori_loop(0, n_pairs, max_pair, m_own)

        def pv_pair(i, pv):
            s = [[scores(n, h, qs) for h in range(2)] for n in (2 * i, 2 * i + 1)]
            return tuple(pv[h] + sum(_mm(vt_ref[h, n], jnp.exp(s[j][h] - m[h]).astype(BF16))
                                     for j, n in enumerate((2 * i, 2 * i + 1)))
                         for h in range(2))

        pv_own = tuple(_mm(vt_ref[h, qt], jnp.exp(s_own[h] - m[h]).astype(BF16)) for h in range(2))
        pv = lax.fori_loop(0, n_pairs, pv_pair, pv_own)
        o_t = jnp.concatenate([x[:HEAD_DIM] / x[HEAD_DIM:HEAD_DIM + 1] for x in pv], axis=0)
        o_ref[0, rows, :] = o_t.T
        return 0

    lax.fori_loop(0, nb, q_tile, 0)


def _moba(pb, gq, gk):
    bsz, seq, _ = pb.shape
    n_pairs = B_HEADS // 2
    nb = seq // MOBA_BLOCK
    ones_rows = 16
    slab = lambda off: pl.BlockSpec((1, seq, LANES), lambda b, j: (b, 0, off + j))
    gain = pl.BlockSpec((1, LANES), lambda b, j: (0, 0))
    return pl.pallas_call(
        _moba_kernel,
        grid=(bsz, n_pairs),
        in_specs=[slab(0), slab(n_pairs), slab(2 * n_pairs), gain, gain],
        out_specs=pl.BlockSpec((1, seq, LANES), lambda b, j: (b, 0, j)),
        out_shape=jax.ShapeDtypeStruct((bsz, seq, B_WIDTH), F32),
        scratch_shapes=[pltpu.VMEM((seq, LANES), F32), pltpu.VMEM((seq, LANES), BF16),
                        pltpu.VMEM((2, seq, LANES), BF16), pltpu.VMEM((2, nb, LANES), F32),
                        pltpu.VMEM((2, nb, HEAD_DIM + ones_rows, MOBA_BLOCK), BF16),
                        pltpu.VMEM((2, nb, MOBA_BLOCK), F32)],
        compiler_params=_params(2),
        name="moba",
    )(pb, pb, pb, jnp.tile(gq, 2).reshape(1, LANES), jnp.tile(gk, 2).reshape(1, LANES))


def _dilated_kernel(q_ref, k_ref, v_ref, gq_ref, gk_ref, o_ref,
                    qn_ref, kn_ref, vv_ref, acc_ref, ma_ref, mb_ref, la_ref, lb_ref):
    seq = q_ref.shape[1]
    _qk_norm_into(q_ref, qn_ref, gq_ref[...], ATTN_SCALE)
    _qk_norm_into(k_ref, kn_ref, gk_ref[...], 1.0)
    vv_ref[...] = v_ref[0]
    acc_ref[...] = jnp.zeros_like(acc_ref)
    for m_ref, l_ref in ((ma_ref, la_ref), (mb_ref, lb_ref)):
        m_ref[...] = jnp.full_like(m_ref, NEG)
        l_ref[...] = jnp.zeros_like(l_ref)

    is_a = _lane_is_a((BAND, LANES))
    qi = lax.broadcasted_iota(jnp.int32, (BAND, 2 * BAND), 0)
    kj = lax.broadcasted_iota(jnp.int32, (BAND, 2 * BAND), 1)
    band_two = (kj >= qi) & (kj <= qi + BAND)
    q1 = lax.broadcasted_iota(jnp.int32, (BAND, BAND), 0)
    k1 = lax.broadcasted_iota(jnp.int32, (BAND, BAND), 1)
    band_one = k1 <= q1

    def block(q_rows, k_rows, mask):
        q = qn_ref[q_rows, :]
        kb = kn_ref[k_rows, :].astype(BF16)
        vb = vv_ref[k_rows, :].astype(BF16)
        n_rep = mask.shape[1] // LANES
        alphas, pvs = [], []
        for sel_lane, m_ref, l_ref in ((is_a, ma_ref, la_ref), (jnp.logical_not(is_a), mb_ref, lb_ref)):
            qh = jnp.where(sel_lane, q, 0.0).astype(BF16)
            s = jnp.where(mask, _mm(qh, kb, _NT), NEG)
            m_old = m_ref[q_rows, :]
            m_new = jnp.maximum(m_old, jnp.max(s, axis=-1, keepdims=True))
            alpha = jnp.exp(m_old - m_new)
            p = jnp.exp(s - jnp.concatenate([m_new] * n_rep, axis=1))
            l_ref[q_rows, :] = alpha * l_ref[q_rows, :] + jnp.sum(p, axis=-1, keepdims=True)
            m_ref[q_rows, :] = m_new
            alphas.append(alpha)
            pvs.append(_mm(p.astype(BF16), vb))
        acc_ref[q_rows, :] = (jnp.where(is_a, alphas[0], alphas[1]) * acc_ref[q_rows, :]
                              + jnp.where(is_a, pvs[0], pvs[1]))

    for dil in DILATIONS:
        n_blocks = seq // (dil * BAND)
        stride = None if dil == 1 else dil

        def residue(r, _, dil=dil, n_blocks=n_blocks, stride=stride):
            block(pl.ds(r, BAND, stride=stride), pl.ds(r, BAND, stride=stride), band_one)

            def later(c, _):
                base = r + c * (BAND * dil)
                block(pl.ds(base, BAND, stride=stride),
                      pl.ds(base - BAND * dil, 2 * BAND, stride=stride), band_two)
                return 0

            lax.fori_loop(1, n_blocks, later, 0)
            return 0

        lax.fori_loop(0, dil, residue, 0)

    o_ref[0] = acc_ref[...] / jnp.where(_lane_is_a((seq, LANES)), la_ref[...], lb_ref[...])


def _dilated(pc, gq, gk):
    bsz, seq, _ = pc.shape
    n_pairs = C_HEADS // 2
    slab = lambda off: pl.BlockSpec((1, seq, LANES), lambda b, j: (b, 0, off + j))
    gain = pl.BlockSpec((1, LANES), lambda b, j: (0, 0))
    full = pltpu.VMEM((seq, LANES), F32)
    return pl.pallas_call(
        _dilated_kernel,
        grid=(bsz, n_pairs),
        in_specs=[slab(0), slab(n_pairs), slab(2 * n_pairs), gain, gain],
        out_specs=pl.BlockSpec((1, seq, LANES), lambda b, j: (b, 0, j)),
        out_shape=jax.ShapeDtypeStruct((bsz, seq, C_WIDTH), F32),
        scratch_shapes=[full] * 8,
        compiler_params=_params(2),
        name="dilated",
    )(pc, pc, pc, jnp.tile(gq, 2).reshape(1, LANES), jnp.tile(gk, 2).reshape(1, LANES))


def kernel(x, norm_g, w_in, w_out, tshift_mu, decay_w0, decay_up, iclr_a0, iclr_up,
           k_k, k_a, r_k, lnx_g, lnx_b, moba_q_g, moba_k_g, dil_q_g, dil_k_g):
    bsz, seq, d_model = x.shape
    depth = norm_g.shape[0]
    assert seq % (max(DILATIONS) * BAND) == 0 and seq % MOBA_BLOCK == 0 and seq % RWKV_CHUNK == 0
    x2d = x.reshape(bsz * seq, d_model)
    w_in_bf = w_in.astype(BF16)
    w_out_bf = w_out.astype(BF16)
    for l in range(depth):
        pa, pb, pc, gate = _inproj(x2d, norm_g[l], w_in_bf[l])
        pp, wz = _rwkv_pack_params(decay_w0[l], decay_up[l], iclr_a0[l], iclr_up[l],
                                   k_k[l], k_a[l], r_k[l], lnx_g[l], lnx_b[l])
        ya = _rwkv(pa.reshape(bsz, seq, A_PROJ), tshift_mu[l], pp, wz)
        yb = _moba(pb.reshape(bsz, seq, B_PROJ), moba_q_g[l], moba_k_g[l])
        yc = _dilated(pc.reshape(bsz, seq, C_PROJ), dil_q_g[l], dil_k_g[l])
        x2d = _outproj(x2d, ya.reshape(bsz * seq, A_WIDTH), yb.reshape(bsz * seq, B_WIDTH),
                       yc.reshape(bsz * seq, C_WIDTH), gate, w_out_bf[l])
    return x2d.reshape(bsz, seq, d_model)
```

```python
import functools

import jax
import jax.numpy as jnp
from jax import lax
from jax.experimental import pallas as pl
from jax.experimental.pallas import tpu as pltpu

F32 = jnp.float32
BF16 = jnp.bfloat16

HEAD_DIM = 64
LANES = 128
A_HEADS, B_HEADS, C_HEADS = 6, 4, 6
A_WIDTH, B_WIDTH, C_WIDTH = A_HEADS * HEAD_DIM, B_HEADS * HEAD_DIM, C_HEADS * HEAD_DIM
LORA = 64
A_PROJ = 3 * A_WIDTH + 2 * LORA
B_PROJ = 3 * B_WIDTH
C_PROJ = 3 * C_WIDTH
MIX_WIDTH = A_WIDTH + B_WIDTH + C_WIDTH
PROJ_WIDTH = A_PROJ + B_PROJ + C_PROJ + MIX_WIDTH
MOBA_BLOCK = 256
MOBA_TOPK = 3
DILATIONS = (1, 4, 16)
BAND = 128
DIL_GROUP = 8
RMS_EPS = 1e-6
LNX_EPS = HEAD_DIM * 1e-5
ATTN_SCALE = HEAD_DIM ** -0.5
RWKV_CHUNK = 64
NEG = -1e30
VMEM_LIMIT = 56 * 1024 * 1024


def _params(n_axes):
    return pltpu.CompilerParams(dimension_semantics=("arbitrary",) * n_axes,
                                vmem_limit_bytes=VMEM_LIMIT)


_NN = (((1,), (0,)), ((), ()))
_NT = (((1,), (1,)), ((), ()))
_TN = (((0,), (0,)), ((), ()))


def _mm(a, b, dims=_NN):
    return lax.dot_general(a, b, dims, preferred_element_type=F32)


def _split(x):
    hi = x.astype(BF16)
    lo = (x - hi.astype(F32)).astype(BF16)
    return hi, lo


def _mm3(a, b, dims=_NN):
    ah, al = _split(a)
    bh, bl = _split(b)
    return _mm(ah, bh, dims) + (_mm(ah, bl, dims) + _mm(al, bh, dims))


def _half_sum(x, is_a):
    sa = jnp.sum(jnp.where(is_a, x, 0.0), axis=-1, keepdims=True)
    sb = jnp.sum(jnp.where(is_a, 0.0, x), axis=-1, keepdims=True)
    return jnp.where(is_a, sa, sb)


def _half_sum_mxu(x):
    row = lax.broadcasted_iota(jnp.int32, (LANES, LANES), 0) // HEAD_DIM
    col = lax.broadcasted_iota(jnp.int32, (LANES, LANES), 1) // HEAD_DIM
    ones_bd = (row == col).astype(BF16)
    hi, lo = _split(x)
    return _mm(hi, ones_bd) + _mm(lo, ones_bd)


def _round_robin(chains):
    results = [None] * len(chains)
    live = list(range(len(chains)))
    while live:
        for i in list(live):
            try:
                next(chains[i])
            except StopIteration as done:
                results[i] = done.value
                live.remove(i)
    return results


def _lane_is_a(shape):
    return lax.broadcasted_iota(jnp.int32, shape, len(shape) - 1) < HEAD_DIM


def _inproj_kernel(x_ref, g_ref, w_ref, pa_ref, pb_ref, pc_ref, gt_ref):
    x = x_ref[...]
    ms = jnp.mean(x * x, axis=-1, keepdims=True)
    h = (x * lax.rsqrt(ms + RMS_EPS) * g_ref[...]).astype(BF16)
    lo = 0
    for ref in (pa_ref, pb_ref, pc_ref, gt_ref):
        hi = lo + ref.shape[-1]
        ref[...] = _mm(h, w_ref[:, lo:hi])
        lo = hi


def _inproj(x2d, g, w_bf16, tm=256):
    m, d = x2d.shape
    widths = (A_PROJ, B_PROJ, C_PROJ, MIX_WIDTH)
    return pl.pallas_call(
        _inproj_kernel,
        grid=(m // tm,),
        in_specs=[pl.BlockSpec((tm, d), lambda i: (i, 0)),
                  pl.BlockSpec((1, d), lambda i: (0, 0)),
                  pl.BlockSpec((d, PROJ_WIDTH), lambda i: (0, 0))],
        out_specs=[pl.BlockSpec((tm, w), lambda i: (i, 0)) for w in widths],
        out_shape=[jax.ShapeDtypeStruct((m, w), F32) for w in widths],
        compiler_params=_params(1),
        name="inproj",
    )(x2d, g.reshape(1, d), w_bf16)


def _outproj_kernel(x_ref, ya_ref, yb_ref, yc_ref, gt_ref, w_ref, o_ref):
    acc = x_ref[...]
    lo = 0
    for y_ref in (ya_ref, yb_ref, yc_ref):
        hi = lo + y_ref.shape[-1]
        g = gt_ref[:, lo:hi]
        y = y_ref[...] * (g * jax.nn.sigmoid(g))
        acc = acc + _mm(y.astype(BF16), w_ref[lo:hi, :])
        lo = hi
    o_ref[...] = acc


def _outproj(x2d, ya, yb, yc, gate, w_bf16, tm=512):
    m, d = x2d.shape
    row = lambda w: pl.BlockSpec((tm, w), lambda i: (i, 0))
    return pl.pallas_call(
        _outproj_kernel,
        grid=(m // tm,),
        in_specs=[row(d), row(A_WIDTH), row(B_WIDTH), row(C_WIDTH), row(MIX_WIDTH),
                  pl.BlockSpec((MIX_WIDTH, d), lambda i: (0, 0))],
        out_specs=row(d),
        out_shape=jax.ShapeDtypeStruct((m, d), F32),
        compiler_params=_params(1),
        name="outproj",
    )(x2d, ya, yb, yc, gate, w_bf16)


_P_W0, _P_A0, _P_KK, _P_KA, _P_RK, _P_LNG, _P_LNB = range(7)
_P_ROWS = 8


def _rwkv_kernel(pa_ref, mu_ref, pp_ref, wz_ref, o_ref, st_ref, prev_ref):
    tile = pa_ref.shape[1]
    c_len = RWKV_CHUNK
    two_c = 2 * c_len
    n_chunks = tile // c_len
    n_pairs = A_HEADS // 2
    is_a = _lane_is_a((c_len, LANES))

    row_c = lax.broadcasted_iota(jnp.int32, (c_len, A_PROJ), 0)
    ri = lax.broadcasted_iota(jnp.int32, (c_len, c_len), 0)
    ci = lax.broadcasted_iota(jnp.int32, (c_len, c_len), 1)
    tril_c = (ri >= ci).astype(BF16)
    r2 = lax.broadcasted_iota(jnp.int32, (two_c, two_c), 0)
    c2 = lax.broadcasted_iota(jnp.int32, (two_c, two_c), 1)
    same = (r2 // c_len) == (c2 // c_len)
    m_strict = same & ((r2 % c_len) > (c2 % c_len))
    m_incl = same & ((r2 % c_len) >= (c2 % c_len))
    eye2 = (r2 == c2).astype(F32)

    def stack(x):
        return jnp.concatenate([jnp.where(is_a, x, 0.0), jnp.where(is_a, 0.0, x)], axis=0)

    @pl.when(pl.program_id(1) == 0)
    def _():
        st_ref[...] = jnp.zeros_like(st_ref)
        prev_ref[...] = jnp.zeros_like(prev_ref)

    def pair_chunk(j, r, k, v, lora_w, lora_a):
        pp = pp_ref[j]
        prow = lambda i: pp[i:i + 1, :]
        w0, a0, k_k, k_a, r_k = prow(_P_W0), prow(_P_A0), prow(_P_KK), prow(_P_KA), prow(_P_RK)
        ln_g, ln_b = prow(_P_LNG), prow(_P_LNB)
        w = -jax.nn.softplus(-(w0 + lora_w)) - 0.5
        lw = -jnp.exp(w)
        a = jax.nn.sigmoid(a0 + lora_a)
        kk = k * k_k
        kk = kk * lax.rsqrt(_half_sum(kk * kk, is_a) + 1e-12)
        k2 = k * (1.0 + (a - 1.0) * k_a)
        kka = kk * a

        l1 = lw.astype(BF16)
        rem = lw - l1.astype(F32)
        l2 = rem.astype(BF16)
        l3 = (rem - l2.astype(F32)).astype(BF16)
        g = _mm(tril_c, l1) + (_mm(tril_c, l2) + _mm(tril_c, l3))
        yield
        g_end = g[c_len - 1:c_len, :]
        e_pos = jnp.exp(g)
        e_neg = jnp.exp(-g)
        e_prev = jnp.exp(g - lw)
        e_tail = jnp.exp(g_end - g)

        ab2 = stack(-kk * e_prev)
        rb2 = stack(r * e_pos)
        bt2 = stack(kka * e_neg)
        kt2 = stack(k2 * e_neg)
        bp2 = stack(kka * e_tail)
        kp2 = stack(k2 * e_tail)
        v2 = stack(v)

        mm = _mm3(jnp.concatenate([ab2, rb2], axis=0), jnp.concatenate([bt2, kt2], axis=0), _NT)
        yield
        l_b = jnp.where(m_strict, mm[:two_c, :two_c], 0.0)
        l_k = jnp.where(m_strict, mm[:two_c, two_c:], 0.0)
        r_b = jnp.where(m_incl, mm[two_c:, :two_c], 0.0)
        r_k2 = jnp.where(m_incl, mm[two_c:, two_c:], 0.0)

        t_inv = eye2 + l_b
        p = _mm3(l_b, l_b)
        kv = _mm3(l_k, v2)
        yield
        steps = c_len.bit_length() - 2
        for i in range(steps):
            if i + 1 < steps:
                tp = _mm3(jnp.concatenate([t_inv, p], axis=0), p)
                yield
                t_inv = t_inv + tp[:two_c]
                p = tp[two_c:]
            else:
                tp = _mm3(t_inv, p)
                yield
                t_inv = t_inv + tp

        tw = _mm3(t_inv, jnp.concatenate([ab2, kv], axis=1))
        yield
        sv = st_ref[j]
        ws = _mm3(jnp.concatenate([tw[:, :LANES], rb2], axis=0), sv, _NT)
        yield
        u2 = ws[:two_c] + tw[:, LANES:]
        uv = jnp.concatenate([u2, v2], axis=0)
        y2 = ws[two_c:] + _mm3(jnp.concatenate([r_b, r_k2], axis=1), uv)
        st_ref[j] = sv * jnp.exp(g_end) + _mm3(uv, jnp.concatenate([bp2, kp2], axis=0), _TN)
        yield

        y = y2[:c_len] + y2[c_len:]
        mean = _half_sum(y, is_a) * (1.0 / HEAD_DIM)
        yc = y - mean
        var = _half_sum(yc * yc, is_a) * (1.0 / HEAD_DIM)
        y = yc * lax.rsqrt(var + LNX_EPS) * ln_g + ln_b
        return y + _half_sum(r * k2 * r_k, is_a) * v

    mu = mu_ref[...]
    wz = wz_ref[...]

    def body(c, prev_row):
        t0 = pl.multiple_of(c * c_len, c_len)
        x = pa_ref[0, pl.ds(t0, c_len), :]
        prev = jnp.where(row_c == 0, prev_row, pltpu.roll(x, 1, axis=0))
        xs = x + (prev - x) * mu
        slab = lambda i: xs[:, i * LANES:(i + 1) * LANES]
        z = slab(3 * n_pairs)
        lora = _mm3(jnp.where(is_a, jnp.tanh(z), z), wz)
        chains = [pair_chunk(j, slab(j), slab(n_pairs + j), slab(2 * n_pairs + j),
                             lora[:, 2 * j * LANES:(2 * j + 1) * LANES],
                             lora[:, (2 * j + 1) * LANES:(2 * j + 2) * LANES])
                  for j in range(n_pairs)]
        for j, y in enumerate(_round_robin(chains)):
            o_ref[0, pl.ds(t0, c_len), j * LANES:(j + 1) * LANES] = y
        return x[c_len - 1:c_len, :]

    prev_ref[0:1, :] = lax.fori_loop(0, n_chunks, body, prev_ref[0:1, :])


def _rwkv(pa, mu, pp, wz, tile=1024):
    bsz, seq, _ = pa.shape
    n_pairs = A_HEADS // 2
    whole = lambda shape: pl.BlockSpec(shape, lambda b, s: (0,) * len(shape))
    return pl.pallas_call(
        _rwkv_kernel,
        grid=(bsz, seq // tile),
        in_specs=[pl.BlockSpec((1, tile, A_PROJ), lambda b, s: (b, s, 0)),
                  whole((1, A_PROJ)), whole((n_pairs, _P_ROWS, LANES)), whole((LANES, n_pairs * 2 * LANES))],
        out_specs=pl.BlockSpec((1, tile, A_WIDTH), lambda b, s: (b, s, 0)),
        out_shape=jax.ShapeDtypeStruct((bsz, seq, A_WIDTH), F32),
        scratch_shapes=[pltpu.VMEM((n_pairs, LANES, LANES), F32), pltpu.VMEM((8, A_PROJ), F32)],
        compiler_params=_params(2),
        name="rwkv",
    )(pa, mu.reshape(1, A_PROJ), pp, wz)


def _rwkv_pack_params(w0, w_up, a0, a_up, k_k, k_a, r_k, ln_g, ln_b):
    n_pairs = A_HEADS // 2
    pair = lambda t, j: t[j * LANES:(j + 1) * LANES]
    pps, wzs = [], []
    zeros = jnp.zeros((LORA, LANES), F32)
    for j in range(n_pairs):
        rows = {_P_W0: w0, _P_A0: a0, _P_KK: k_k, _P_KA: k_a, _P_RK: r_k.reshape(-1), _P_LNG: ln_g, _P_LNB: ln_b}
        pps.append(jnp.stack([pair(rows[i], j) if i in rows else jnp.zeros((LANES,), F32)
                              for i in range(_P_ROWS)]))
        wzs.append(jnp.concatenate([w_up[:, j * LANES:(j + 1) * LANES], zeros], axis=0))
        wzs.append(jnp.concatenate([zeros, a_up[:, j * LANES:(j + 1) * LANES]], axis=0))
    return jnp.stack(pps), jnp.concatenate(wzs, axis=1)


def _qk_norm_into(src_ref, dst_ref, gain, scale, tile=512):
    seq = dst_ref.shape[0]

    def body(i, _):
        t0 = pl.multiple_of(i * tile, tile)
        x = src_ref[0, pl.ds(t0, tile), :]
        ms = _half_sum_mxu(x * x) * (1.0 / HEAD_DIM)
        dst_ref[pl.ds(t0, tile), :] = x * lax.rsqrt(ms + RMS_EPS) * (gain * scale)
        return 0

    lax.fori_loop(0, seq // tile, body, 0)


def _moba_kernel(q_ref, k_ref, v_ref, gq_ref, gk_ref, o_ref,
                 qn_ref, kb_ref, qs_ref, km_ref, vt_ref, bias_ref):
    seq = q_ref.shape[1]
    blk = MOBA_BLOCK
    nb = seq // blk
    ones_rows = vt_ref.shape[2] - HEAD_DIM
    is_a = _lane_is_a((blk, LANES))
    gq, gk = gq_ref[...], gk_ref[...]

    def prepare(n, _):
        t0 = pl.multiple_of(n * blk, blk)
        rows = pl.ds(t0, blk)
        q = q_ref[0, rows, :]
        qn = q * lax.rsqrt(_half_sum_mxu(q * q) * (1.0 / HEAD_DIM) + RMS_EPS) * gq
        qn_ref[rows, :] = qn
        qs = qn * ATTN_SCALE
        qs_ref[0, rows, :] = jnp.where(is_a, qs, 0.0).astype(BF16)
        qs_ref[1, rows, :] = jnp.where(is_a, 0.0, qs).astype(BF16)
        k = k_ref[0, rows, :]
        kn = k * lax.rsqrt(_half_sum_mxu(k * k) * (1.0 / HEAD_DIM) + RMS_EPS) * gk
        kb_ref[rows, :] = kn.astype(BF16)
        km = jnp.mean(kn, axis=0, keepdims=True)
        km_ref[0, pl.ds(n, 1), :] = jnp.where(is_a[:1], km, 0.0)
        km_ref[1, pl.ds(n, 1), :] = jnp.where(is_a[:1], 0.0, km)
        vt = v_ref[0, rows, :].T.astype(BF16)
        ones = jnp.ones((ones_rows, blk), BF16)
        for h in range(2):
            vt_ref[h, n, :HEAD_DIM, :] = vt[h * HEAD_DIM:(h + 1) * HEAD_DIM]
            vt_ref[h, n, HEAD_DIM:, :] = ones
        return 0

    lax.fori_loop(0, nb, prepare, 0)

    blk_row = lax.broadcasted_iota(jnp.int32, (nb, blk), 0)
    key_pos = lax.broadcasted_iota(jnp.int32, (blk, blk), 0)
    qry_pos = lax.broadcasted_iota(jnp.int32, (blk, blk), 1)
    causal = key_pos <= qry_pos

    def scores(n, h, qs):
        kb = kb_ref[pl.ds(pl.multiple_of(n * blk, blk), blk), :]
        return _mm(kb, qs[h], _NT) + bias_ref[h, pl.ds(n, 1), :]

    def q_tile(qt, _):
        t0 = pl.multiple_of(qt * blk, blk)
        rows = pl.ds(t0, blk)
        qn = qn_ref[rows, :]
        for h in range(2):
            gate = _mm3(km_ref[h], qn, _NT)
            gate = jnp.where(blk_row < qt, gate, -jnp.inf)
            bias = jnp.full((nb, blk), NEG, F32)
            for _ in range(MOBA_TOPK):
                top = jnp.max(gate, axis=0, keepdims=True)
                hit = (gate == top) & (top > -jnp.inf)
                first = jnp.min(jnp.where(hit, blk_row, nb), axis=0, keepdims=True)
                pick = blk_row == first
                bias = jnp.where(pick, 0.0, bias)
                gate = jnp.where(pick, -jnp.inf, gate)
            bias_ref[h] = bias

        qs = [qs_ref[h, rows, :] for h in range(2)]
        n_pairs = (qt + 1) // 2
        s_own = [jnp.where(causal, _mm(kb_ref[rows, :], qs[h], _NT), NEG) for h in range(2)]
        m_own = tuple(jnp.max(s, axis=0, keepdims=True) for s in s_own)

        def max_pair(i, m):
            s = [[scores(n, h, qs) for h in range(2)] for n in (2 * i, 2 * i + 1)]
            return tuple(jnp.maximum(m[h], jnp.max(jnp.maximum(s[0][h], s[1][h]), axis=0, keepdims=True))
                         for h in range(2))

        m = lax.fori_loop(0, n_pairs, max_pair, m_own)

        def pv_pair(i, pv):
            s = [[scores(n, h, qs) for h in range(2)] for n in (2 * i, 2 * i + 1)]
            return tuple(pv[h] + sum(_mm(vt_ref[h, n], jnp.exp(s[j][h] - m[h]).astype(BF16))
                                     for j, n in enumerate((2 * i, 2 * i + 1)))
                         for h in range(2))

        pv_own = tuple(_mm(vt_ref[h, qt], jnp.exp(s_own[h] - m[h]).astype(BF16)) for h in range(2))
        pv = lax.fori_loop(0, n_pairs, pv_pair, pv_own)
        o_t = jnp.concatenate([x[:HEAD_DIM] / x[HEAD_DIM:HEAD_DIM + 1] for x in pv], axis=0)
        o_ref[0, rows, :] = o_t.T
        return 0

    lax.fori_loop(0, nb, q_tile, 0)


def _moba(pb, gq, gk):
    bsz, seq, _ = pb.shape
    n_pairs = B_HEADS // 2
    nb = seq // MOBA_BLOCK
    ones_rows = 16
    slab = lambda off: pl.BlockSpec((1, seq, LANES), lambda b, j: (b, 0, off + j))
    gain = pl.BlockSpec((1, LANES), lambda b, j: (0, 0))
    return pl.pallas_call(
        _moba_kernel,
        grid=(bsz, n_pairs),
        in_specs=[slab(0), slab(n_pairs), slab(2 * n_pairs), gain, gain],
        out_specs=pl.BlockSpec((1, seq, LANES), lambda b, j: (b, 0, j)),
        out_shape=jax.ShapeDtypeStruct((bsz, seq, B_WIDTH), F32),
        scratch_shapes=[pltpu.VMEM((seq, LANES), F32), pltpu.VMEM((seq, LANES), BF16),
                        pltpu.VMEM((2, seq, LANES), BF16), pltpu.VMEM((2, nb, LANES), F32),
                        pltpu.VMEM((2, nb, HEAD_DIM + ones_rows, MOBA_BLOCK), BF16),
                        pltpu.VMEM((2, nb, MOBA_BLOCK), F32)],
        compiler_params=_params(2),
        name="moba",
    )(pb, pb, pb, jnp.tile(gq, 2).reshape(1, LANES), jnp.tile(gk, 2).reshape(1, LANES))


def _dilated_kernel(q_ref, k_ref, v_ref, gq_ref, gk_ref, o_ref,
                    qn_ref, kn_ref, qs_ref, kb_ref, vt_ref, og_ref, lse_ref):
    seq = q_ref.shape[1]
    n_blocks = seq // BAND
    ones_rows = vt_ref.shape[2] - HEAD_DIM
    _qk_norm_into(q_ref, qn_ref, gq_ref[...], ATTN_SCALE)
    _qk_norm_into(k_ref, kn_ref, gk_ref[...], 1.0)

    is_a = _lane_is_a((BAND, LANES))
    key_j = lax.broadcasted_iota(jnp.int32, (2 * BAND, BAND), 0)
    qry_i = lax.broadcasted_iota(jnp.int32, (2 * BAND, BAND), 1)
    band_bias = jnp.where((key_j >= qry_i) & (key_j <= qry_i + BAND), 0.0, NEG)
    ones = jnp.ones((ones_rows, BAND), BF16)

    for g, dil in enumerate(DILATIONS):
        per_residue = n_blocks // dil
        stride = None if dil == 1 else dil

        def token_rows(gb, dil=dil, per_residue=per_residue, stride=stride):
            r, c = gb // per_residue, gb % per_residue
            return pl.ds(r + c * (BAND * dil), BAND, stride=stride)

        def gather(i, _, token_rows=token_rows):
            for gb in [DIL_GROUP * i + j for j in range(DIL_GROUP)]:
                rows = token_rows(gb)
                dst = pl.ds(pl.multiple_of(gb * BAND, BAND), BAND)
                q = qn_ref[rows, :]
                qs_ref[0, dst, :] = jnp.where(is_a, q, 0.0).astype(BF16)
                qs_ref[1, dst, :] = jnp.where(is_a, 0.0, q).astype(BF16)
                kb_ref[dst, :] = kn_ref[rows, :].astype(BF16)
                vt = v_ref[0, rows, :].T.astype(BF16)
                for h in range(2):
                    vt_ref[h, gb, :HEAD_DIM, :] = vt[h * HEAD_DIM:(h + 1) * HEAD_DIM]
                    vt_ref[h, gb, HEAD_DIM:, :] = ones
            return 0

        lax.fori_loop(0, n_blocks // DIL_GROUP, gather, 0)

        def band_block(gb, per_residue=per_residue):
            prev = jnp.maximum(gb - 1, 0)
            first = jnp.where(gb % per_residue == 0, NEG, 0.0)
            kb = jnp.concatenate([kb_ref[pl.ds(pl.multiple_of(prev * BAND, BAND), BAND), :],
                                  kb_ref[pl.ds(pl.multiple_of(gb * BAND, BAND), BAND), :]], axis=0)
            q_rows = pl.ds(pl.multiple_of(gb * BAND, BAND), BAND)

            def scores(h):
                s = _mm(kb, qs_ref[h, q_rows, :], _NT) + band_bias
                return jnp.concatenate([s[:BAND] + first, s[BAND:]], axis=0)

            m = [jnp.max(scores(h), axis=0, keepdims=True) for h in range(2)]
            yield
            pv = [_mm(jnp.concatenate([vt_ref[h, prev], vt_ref[h, gb]], axis=1),
                      jnp.exp(scores(h) - m[h]).astype(BF16)) for h in range(2)]
            yield
            l = [x[HEAD_DIM:HEAD_DIM + 1] for x in pv]
            o_t = jnp.concatenate([pv[h][:HEAD_DIM] / l[h] for h in range(2)], axis=0)
            lse_t = jnp.concatenate([jnp.broadcast_to(m[h] + jnp.log(l[h]), (HEAD_DIM, BAND))
                                     for h in range(2)], axis=0)
            return o_t.T, lse_t.T

        def group(i, _, band_block=band_block, token_rows=token_rows, g=g):
            blocks = [DIL_GROUP * i + j for j in range(DIL_GROUP)]
            for gb, (o, lse) in zip(blocks, _round_robin([band_block(gb) for gb in blocks])):
                og_ref[g, token_rows(gb), :] = o
                lse_ref[g, token_rows(gb), :] = lse
            return 0

        lax.fori_loop(0, n_blocks // DIL_GROUP, group, 0)

    tile = 512

    def mix(i, _):
        rows = pl.ds(pl.multiple_of(i * tile, tile), tile)
        lse = [lse_ref[g, rows, :] for g in range(len(DILATIONS))]
        top = functools.reduce(jnp.maximum, lse)
        w = [jnp.exp(x - top) for x in lse]
        o_ref[0, rows, :] = sum(w[g] * og_ref[g, rows, :] for g in range(len(DILATIONS))) / sum(w)
        return 0

    lax.fori_loop(0, seq // tile, mix, 0)


def _dilated(pc, gq, gk):
    bsz, seq, _ = pc.shape
    n_pairs = C_HEADS // 2
    n_pat = len(DILATIONS)
    ones_rows = 16
    slab = lambda off: pl.BlockSpec((1, seq, LANES), lambda b, j: (b, 0, off + j))
    gain = pl.BlockSpec((1, LANES), lambda b, j: (0, 0))
    return pl.pallas_call(
        _dilated_kernel,
        grid=(bsz, n_pairs),
        in_specs=[slab(0), slab(n_pairs), slab(2 * n_pairs), gain, gain],
        out_specs=pl.BlockSpec((1, seq, LANES), lambda b, j: (b, 0, j)),
        out_shape=jax.ShapeDtypeStruct((bsz, seq, C_WIDTH), F32),
        scratch_shapes=[pltpu.VMEM((seq, LANES), F32), pltpu.VMEM((seq, LANES), F32),
                        pltpu.VMEM((2, seq, LANES), BF16), pltpu.VMEM((seq, LANES), BF16),
                        pltpu.VMEM((2, seq // BAND, HEAD_DIM + ones_rows, BAND), BF16),
                        pltpu.VMEM((n_pat, seq, LANES), F32), pltpu.VMEM((n_pat, seq, LANES), F32)],
        compiler_params=_params(2),
        name="dilated",
    )(pc, pc, pc, jnp.tile(gq, 2).reshape(1, LANES), jnp.tile(gk, 2).reshape(1, LANES))


def kernel(x, norm_g, w_in, w_out, tshift_mu, decay_w0, decay_up, iclr_a0, iclr_up,
           k_k, k_a, r_k, lnx_g, lnx_b, moba_q_g, moba_k_g, dil_q_g, dil_k_g):
    bsz, seq, d_model = x.shape
    depth = norm_g.shape[0]
    assert seq % (max(DILATIONS) * BAND) == 0 and seq % MOBA_BLOCK == 0 and seq % RWKV_CHUNK == 0
    x2d = x.reshape(bsz * seq, d_model)
    w_in_bf = w_in.astype(BF16)
    w_out_bf = w_out.astype(BF16)
    for l in range(depth):
        pa, pb, pc, gate = _inproj(x2d, norm_g[l], w_in_bf[l])
        pp, wz = _rwkv_pack_params(decay_w0[l], decay_up[l], iclr_a0[l], iclr_up[l],
                                   k_k[l], k_a[l], r_k[l], lnx_g[l], lnx_b[l])
        ya = _rwkv(pa.reshape(bsz, seq, A_PROJ), tshift_mu[l], pp, wz)
        yb = _moba(pb.reshape(bsz, seq, B_PROJ), moba_q_g[l], moba_k_g[l])
        yc = _dilated(pc.reshape(bsz, seq, C_PROJ), dil_q_g[l], dil_k_g[l])
        x2d = _outproj(x2d, ya.reshape(bsz * seq, A_WIDTH), yb.reshape(bsz * seq, B_WIDTH),
                       yc.reshape(bsz * seq, C_WIDTH), gate, w_out_bf[l])
    return x2d.reshape(bsz, seq, d_model)
```

```python
import functools

import jax
import jax.numpy as jnp
from jax import lax
from jax.experimental import pallas as pl
from jax.experimental.pallas import tpu as pltpu

F32 = jnp.float32
BF16 = jnp.bfloat16

HEAD_DIM = 64
LANES = 128
A_HEADS, B_HEADS, C_HEADS = 6, 4, 6
A_WIDTH, B_WIDTH, C_WIDTH = A_HEADS * HEAD_DIM, B_HEADS * HEAD_DIM, C_HEADS * HEAD_DIM
LORA = 64
A_PROJ = 3 * A_WIDTH + 2 * LORA
B_PROJ = 3 * B_WIDTH
C_PROJ = 3 * C_WIDTH
MIX_WIDTH = A_WIDTH + B_WIDTH + C_WIDTH
PROJ_WIDTH = A_PROJ + B_PROJ + C_PROJ + MIX_WIDTH
MOBA_BLOCK = 256
MOBA_TOPK = 3
DILATIONS = (1, 4, 16)
BAND = 128
DIL_GROUP = 8
RMS_EPS = 1e-6
LNX_EPS = HEAD_DIM * 1e-5
ATTN_SCALE = HEAD_DIM ** -0.5
RWKV_CHUNK = 64
NEG = -1e30
VMEM_LIMIT = 56 * 1024 * 1024


def _params(n_axes):
    return pltpu.CompilerParams(dimension_semantics=("arbitrary",) * n_axes,
                                vmem_limit_bytes=VMEM_LIMIT)


_NN = (((1,), (0,)), ((), ()))
_NT = (((1,), (1,)), ((), ()))
_TN = (((0,), (0,)), ((), ()))


def _mm(a, b, dims=_NN):
    return lax.dot_general(a, b, dims, preferred_element_type=F32)


def _split(x):
    hi = x.astype(BF16)
    lo = (x - hi.astype(F32)).astype(BF16)
    return hi, lo


def _mm3(a, b, dims=_NN):
    ah, al = _split(a)
    bh, bl = _split(b)
    return _mm(ah, bh, dims) + (_mm(ah, bl, dims) + _mm(al, bh, dims))


def _mm1(a, b, dims=_NN):
    return _mm(a.astype(BF16), b.astype(BF16), dims)


def _half_sum(x, is_a):
    sa = jnp.sum(jnp.where(is_a, x, 0.0), axis=-1, keepdims=True)
    sb = jnp.sum(jnp.where(is_a, 0.0, x), axis=-1, keepdims=True)
    return jnp.where(is_a, sa, sb)


def _half_sum_mxu(x):
    row = lax.broadcasted_iota(jnp.int32, (LANES, LANES), 0) // HEAD_DIM
    col = lax.broadcasted_iota(jnp.int32, (LANES, LANES), 1) // HEAD_DIM
    ones_bd = (row == col).astype(BF16)
    hi, lo = _split(x)
    return _mm(hi, ones_bd) + _mm(lo, ones_bd)


def _round_robin(chains):
    results = [None] * len(chains)
    live = list(range(len(chains)))
    while live:
        for i in list(live):
            try:
                next(chains[i])
            except StopIteration as done:
                results[i] = done.value
                live.remove(i)
    return results


def _lane_is_a(shape):
    return lax.broadcasted_iota(jnp.int32, shape, len(shape) - 1) < HEAD_DIM


def _inproj_kernel(x_ref, g_ref, w_ref, pa_ref, pb_ref, pc_ref, gt_ref):
    x = x_ref[...]
    ms = jnp.mean(x * x, axis=-1, keepdims=True)
    h = (x * lax.rsqrt(ms + RMS_EPS) * g_ref[...]).astype(BF16)
    lo = 0
    for ref in (pa_ref, pb_ref, pc_ref, gt_ref):
        hi = lo + ref.shape[-1]
        ref[...] = _mm(h, w_ref[:, lo:hi])
        lo = hi


def _inproj(x2d, g, w_bf16, tm=256):
    m, d = x2d.shape
    widths = (A_PROJ, B_PROJ, C_PROJ, MIX_WIDTH)
    return pl.pallas_call(
        _inproj_kernel,
        grid=(m // tm,),
        in_specs=[pl.BlockSpec((tm, d), lambda i: (i, 0)),
                  pl.BlockSpec((1, d), lambda i: (0, 0)),
                  pl.BlockSpec((d, PROJ_WIDTH), lambda i: (0, 0))],
        out_specs=[pl.BlockSpec((tm, w), lambda i: (i, 0)) for w in widths],
        out_shape=[jax.ShapeDtypeStruct((m, w), F32) for w in widths],
        compiler_params=_params(1),
        name="inproj",
    )(x2d, g.reshape(1, d), w_bf16)


def _outproj_kernel(x_ref, ya_ref, yb_ref, yc_ref, gt_ref, w_ref, o_ref):
    acc = x_ref[...]
    lo = 0
    for y_ref in (ya_ref, yb_ref, yc_ref):
        hi = lo + y_ref.shape[-1]
        g = gt_ref[:, lo:hi]
        y = y_ref[...] * (g * jax.nn.sigmoid(g))
        acc = acc + _mm(y.astype(BF16), w_ref[lo:hi, :])
        lo = hi
    o_ref[...] = acc


def _outproj(x2d, ya, yb, yc, gate, w_bf16, tm=512):
    m, d = x2d.shape
    row = lambda w: pl.BlockSpec((tm, w), lambda i: (i, 0))
    return pl.pallas_call(
        _outproj_kernel,
        grid=(m // tm,),
        in_specs=[row(d), row(A_WIDTH), row(B_WIDTH), row(C_WIDTH), row(MIX_WIDTH),
                  pl.BlockSpec((MIX_WIDTH, d), lambda i: (0, 0))],
        out_specs=row(d),
        out_shape=jax.ShapeDtypeStruct((m, d), F32),
        compiler_params=_params(1),
        name="outproj",
    )(x2d, ya, yb, yc, gate, w_bf16)


_P_W0, _P_A0, _P_KK, _P_KA, _P_RK, _P_LNG, _P_LNB = range(7)
_P_ROWS = 8


def _rwkv_kernel(pa_ref, mu_ref, pp_ref, wz_ref, o_ref, st_ref, prev_ref):
    n_rows, tile = pa_ref.shape[0], pa_ref.shape[1]
    c_len = RWKV_CHUNK
    two_c = 2 * c_len
    n_chunks = tile // c_len
    n_pairs = A_HEADS // 2
    is_a = _lane_is_a((c_len, LANES))

    row_c = lax.broadcasted_iota(jnp.int32, (c_len, A_PROJ), 0)
    ri = lax.broadcasted_iota(jnp.int32, (c_len, c_len), 0)
    ci = lax.broadcasted_iota(jnp.int32, (c_len, c_len), 1)
    tril_c = (ri >= ci).astype(BF16)
    r2 = lax.broadcasted_iota(jnp.int32, (two_c, two_c), 0)
    c2 = lax.broadcasted_iota(jnp.int32, (two_c, two_c), 1)
    same = (r2 // c_len) == (c2 // c_len)
    m_strict = same & ((r2 % c_len) > (c2 % c_len))
    m_incl = same & ((r2 % c_len) >= (c2 % c_len))
    eye2 = (r2 == c2).astype(F32)

    def stack(x):
        return jnp.concatenate([jnp.where(is_a, x, 0.0), jnp.where(is_a, 0.0, x)], axis=0)

    @pl.when(pl.program_id(1) == 0)
    def _():
        st_ref[...] = jnp.zeros_like(st_ref)
        prev_ref[...] = jnp.zeros_like(prev_ref)

    def pair_chunk(b, j, r, k, v, lora_w, lora_a):
        pp = pp_ref[j]
        prow = lambda i: pp[i:i + 1, :]
        w0, a0, k_k, k_a, r_k = prow(_P_W0), prow(_P_A0), prow(_P_KK), prow(_P_KA), prow(_P_RK)
        ln_g, ln_b = prow(_P_LNG), prow(_P_LNB)
        w = -jax.nn.softplus(-(w0 + lora_w)) - 0.5
        lw = -jnp.exp(w)
        a = jax.nn.sigmoid(a0 + lora_a)
        kk = k * k_k
        kk = kk * lax.rsqrt(_half_sum(kk * kk, is_a) + 1e-12)
        k2 = k * (1.0 + (a - 1.0) * k_a)
        kka = kk * a

        l1 = lw.astype(BF16)
        rem = lw - l1.astype(F32)
        l2 = rem.astype(BF16)
        l3 = (rem - l2.astype(F32)).astype(BF16)
        g = _mm(tril_c, l1) + (_mm(tril_c, l2) + _mm(tril_c, l3))
        yield
        g_end = g[c_len - 1:c_len, :]
        e_pos = jnp.exp(g)
        e_neg = jnp.exp(-g)
        e_prev = jnp.exp(g - lw)
        e_tail = jnp.exp(g_end - g)

        ab2 = stack(-kk * e_prev)
        rb2 = stack(r * e_pos)
        bt2 = stack(kka * e_neg)
        kt2 = stack(k2 * e_neg)
        bp2 = stack(kka * e_tail)
        kp2 = stack(k2 * e_tail)
        v2 = stack(v)

        mm = _mm1(jnp.concatenate([ab2, rb2], axis=0), jnp.concatenate([bt2, kt2], axis=0), _NT)
        yield
        l_b = jnp.where(m_strict, mm[:two_c, :two_c], 0.0)
        l_k = jnp.where(m_strict, mm[:two_c, two_c:], 0.0)
        r_b = jnp.where(m_incl, mm[two_c:, :two_c], 0.0)
        r_k2 = jnp.where(m_incl, mm[two_c:, two_c:], 0.0)

        t_inv = eye2 + l_b
        p = _mm1(l_b, l_b)
        kv = _mm1(l_k, v2)
        yield
        steps = c_len.bit_length() - 2
        for i in range(steps):
            if i + 1 < steps:
                tp = _mm1(jnp.concatenate([t_inv, p], axis=0), p)
                yield
                t_inv = t_inv + tp[:two_c]
                p = tp[two_c:]
            else:
                tp = _mm1(t_inv, p)
                yield
                t_inv = t_inv + tp

        tw = _mm1(t_inv, jnp.concatenate([ab2, kv], axis=1))
        yield
        sv = st_ref[b, j]
        ws = _mm1(jnp.concatenate([tw[:, :LANES], rb2], axis=0), sv, _NT)
        yield
        u2 = ws[:two_c] + tw[:, LANES:]
        uv = jnp.concatenate([u2, v2], axis=0)
        y2 = ws[two_c:] + _mm1(jnp.concatenate([r_b, r_k2], axis=1), uv)
        st_ref[b, j] = sv * jnp.exp(g_end) + _mm1(uv, jnp.concatenate([bp2, kp2], axis=0), _TN)
        yield

        y = y2[:c_len] + y2[c_len:]
        mean = _half_sum(y, is_a) * (1.0 / HEAD_DIM)
        yc = y - mean
        var = _half_sum(yc * yc, is_a) * (1.0 / HEAD_DIM)
        y = yc * lax.rsqrt(var + LNX_EPS) * ln_g + ln_b
        return y + _half_sum(r * k2 * r_k, is_a) * v

    mu = mu_ref[...]
    wz = wz_ref[...]

    def body(c, prev_rows):
        t0 = pl.multiple_of(c * c_len, c_len)
        chains, last_rows = [], []
        for b in range(n_rows):
            x = pa_ref[b, pl.ds(t0, c_len), :]
            prev = jnp.where(row_c == 0, prev_rows[b], pltpu.roll(x, 1, axis=0))
            xs = x + (prev - x) * mu
            slab = lambda i, xs=xs: xs[:, i * LANES:(i + 1) * LANES]
            z = slab(3 * n_pairs)
            lora = _mm3(jnp.where(is_a, jnp.tanh(z), z), wz)
            chains += [pair_chunk(b, j, slab(j), slab(n_pairs + j), slab(2 * n_pairs + j),
                                  lora[:, 2 * j * LANES:(2 * j + 1) * LANES],
                                  lora[:, (2 * j + 1) * LANES:(2 * j + 2) * LANES])
                       for j in range(n_pairs)]
            last_rows.append(x[c_len - 1:c_len, :])
        for i, y in enumerate(_round_robin(chains)):
            b, j = divmod(i, n_pairs)
            o_ref[b, pl.ds(t0, c_len), j * LANES:(j + 1) * LANES] = y
        return tuple(last_rows)

    last = lax.fori_loop(0, n_chunks, body, tuple(prev_ref[b, 0:1, :] for b in range(n_rows)))
    for b in range(n_rows):
        prev_ref[b, 0:1, :] = last[b]


def _rwkv(pa, mu, pp, wz, tile=1024, rows=2):
    bsz, seq, _ = pa.shape
    n_pairs = A_HEADS // 2
    whole = lambda shape: pl.BlockSpec(shape, lambda b, s: (0,) * len(shape))
    return pl.pallas_call(
        _rwkv_kernel,
        grid=(bsz // rows, seq // tile),
        in_specs=[pl.BlockSpec((rows, tile, A_PROJ), lambda b, s: (b, s, 0)),
                  whole((1, A_PROJ)), whole((n_pairs, _P_ROWS, LANES)), whole((LANES, n_pairs * 2 * LANES))],
        out_specs=pl.BlockSpec((rows, tile, A_WIDTH), lambda b, s: (b, s, 0)),
        out_shape=jax.ShapeDtypeStruct((bsz, seq, A_WIDTH), F32),
        scratch_shapes=[pltpu.VMEM((rows, n_pairs, LANES, LANES), F32), pltpu.VMEM((rows, 8, A_PROJ), F32)],
        compiler_params=_params(2),
        name="rwkv",
    )(pa, mu.reshape(1, A_PROJ), pp, wz)


def _rwkv_pack_params(w0, w_up, a0, a_up, k_k, k_a, r_k, ln_g, ln_b):
    n_pairs = A_HEADS // 2
    pair = lambda t, j: t[j * LANES:(j + 1) * LANES]
    pps, wzs = [], []
    zeros = jnp.zeros((LORA, LANES), F32)
    for j in range(n_pairs):
        rows = {_P_W0: w0, _P_A0: a0, _P_KK: k_k, _P_KA: k_a, _P_RK: r_k.reshape(-1), _P_LNG: ln_g, _P_LNB: ln_b}
        pps.append(jnp.stack([pair(rows[i], j) if i in rows else jnp.zeros((LANES,), F32)
                              for i in range(_P_ROWS)]))
        wzs.append(jnp.concatenate([w_up[:, j * LANES:(j + 1) * LANES], zeros], axis=0))
        wzs.append(jnp.concatenate([zeros, a_up[:, j * LANES:(j + 1) * LANES]], axis=0))
    return jnp.stack(pps), jnp.concatenate(wzs, axis=1)


def _qk_norm_into(src_ref, dst_ref, gain, scale, tile=512):
    seq = dst_ref.shape[0]

    def body(i, _):
        t0 = pl.multiple_of(i * tile, tile)
        x = src_ref[0, pl.ds(t0, tile), :]
        ms = _half_sum_mxu(x * x) * (1.0 / HEAD_DIM)
        dst_ref[pl.ds(t0, tile), :] = x * lax.rsqrt(ms + RMS_EPS) * (gain * scale)
        return 0

    lax.fori_loop(0, seq // tile, body, 0)


def _moba_kernel(q_ref, k_ref, v_ref, gq_ref, gk_ref, o_ref,
                 qn_ref, kb_ref, qs_ref, km_ref, vt_ref, bias_ref):
    seq = q_ref.shape[1]
    blk = MOBA_BLOCK
    nb = seq // blk
    ones_rows = vt_ref.shape[2] - HEAD_DIM
    is_a = _lane_is_a((blk, LANES))
    gq, gk = gq_ref[...], gk_ref[...]

    def prepare(n, _):
        t0 = pl.multiple_of(n * blk, blk)
        rows = pl.ds(t0, blk)
        q = q_ref[0, rows, :]
        qn = q * lax.rsqrt(_half_sum_mxu(q * q) * (1.0 / HEAD_DIM) + RMS_EPS) * gq
        qn_ref[rows, :] = qn
        qs = qn * ATTN_SCALE
        qs_ref[0, rows, :] = jnp.where(is_a, qs, 0.0).astype(BF16)
        qs_ref[1, rows, :] = jnp.where(is_a, 0.0, qs).astype(BF16)
        k = k_ref[0, rows, :]
        kn = k * lax.rsqrt(_half_sum_mxu(k * k) * (1.0 / HEAD_DIM) + RMS_EPS) * gk
        kb_ref[rows, :] = kn.astype(BF16)
        km = jnp.mean(kn, axis=0, keepdims=True)
        km_ref[0, pl.ds(n, 1), :] = jnp.where(is_a[:1], km, 0.0)
        km_ref[1, pl.ds(n, 1), :] = jnp.where(is_a[:1], 0.0, km)
        vt = v_ref[0, rows, :].T.astype(BF16)
        ones = jnp.ones((ones_rows, blk), BF16)
        for h in range(2):
            vt_ref[h, n, :HEAD_DIM, :] = vt[h * HEAD_DIM:(h + 1) * HEAD_DIM]
            vt_ref[h, n, HEAD_DIM:, :] = ones
        return 0

    lax.fori_loop(0, nb, prepare, 0)

    blk_row = lax.broadcasted_iota(jnp.int32, (nb, blk), 0)
    key_pos = lax.broadcasted_iota(jnp.int32, (blk, blk), 0)
    qry_pos = lax.broadcasted_iota(jnp.int32, (blk, blk), 1)
    causal = key_pos <= qry_pos

    def scores(n, h, qs):
        kb = kb_ref[pl.ds(pl.multiple_of(n * blk, blk), blk), :]
        return _mm(kb, qs[h], _NT) + bias_ref[h, pl.ds(n, 1), :]

    def q_tile(qt, _):
        t0 = pl.multiple_of(qt * blk, blk)
        rows = pl.ds(t0, blk)
        qn = qn_ref[rows, :]
        for h in range(2):
            gate = _mm3(km_ref[h], qn, _NT)
            gate = jnp.where(blk_row < qt, gate, -jnp.inf)
            bias = jnp.full((nb, blk), NEG, F32)
            for _ in range(MOBA_TOPK):
                top = jnp.max(gate, axis=0, keepdims=True)
                hit = (gate == top) & (top > -jnp.inf)
                first = jnp.min(jnp.where(hit, blk_row, nb), axis=0, keepdims=True)
                pick = blk_row == first
                bias = jnp.where(pick, 0.0, bias)
                gate = jnp.where(pick, -jnp.inf, gate)
            bias_ref[h] = bias

        qs = [qs_ref[h, rows, :] for h in range(2)]
        n_pairs = (qt + 1) // 2
        s_own = [jnp.where(causal, _mm(kb_ref[rows, :], qs[h], _NT), NEG) for h in range(2)]
        m_own = tuple(jnp.max(s, axis=0, keepdims=True) for s in s_own)

        def max_pair(i, m):
            s = [[scores(n, h, qs) for h in range(2)] for n in (2 * i, 2 * i + 1)]
            return tuple(jnp.maximum(m[h], jnp.max(jnp.maximum(s[0][h], s[1][h]), axis=0, keepdims=True))
                         for h in range(2))

        m = lax.fori_loop(0, n_pairs, max_pair, m_own)

        def pv_pair(i, pv):
            s = [[scores(n, h, qs) for h in range(2)] for n in (2 * i, 2 * i + 1)]
            return tuple(pv[h] + sum(_mm(vt_ref[h, n], jnp.exp(s[j][h] - m[h]).astype(BF16))
                                     for j, n in enumerate((2 * i, 2 * i + 1)))
                         for h in range(2))

        pv_own = tuple(_mm(vt_ref[h, qt], jnp.exp(s_own[h] - m[h]).astype(BF16)) for h in range(2))
        pv = lax.fori_loop(0, n_pairs, pv_pair, pv_own)
        o_t = jnp.concatenate([x[:HEAD_DIM] / x[HEAD_DIM:HEAD_DIM + 1] for x in pv], axis=0)
        o_ref[0, rows, :] = o_t.T
        return 0

    lax.fori_loop(0, nb, q_tile, 0)


def _moba(pb, gq, gk):
    bsz, seq, _ = pb.shape
    n_pairs = B_HEADS // 2
    nb = seq // MOBA_BLOCK
    ones_rows = 16
    slab = lambda off: pl.BlockSpec((1, seq, LANES), lambda b, j: (b, 0, off + j))
    gain = pl.BlockSpec((1, LANES), lambda b, j: (0, 0))
    return pl.pallas_call(
        _moba_kernel,
        grid=(bsz, n_pairs),
        in_specs=[slab(0), slab(n_pairs), slab(2 * n_pairs), gain, gain],
        out_specs=pl.BlockSpec((1, seq, LANES), lambda b, j: (b, 0, j)),
        out_shape=jax.ShapeDtypeStruct((bsz, seq, B_WIDTH), F32),
        scratch_shapes=[pltpu.VMEM((seq, LANES), F32), pltpu.VMEM((seq, LANES), BF16),
                        pltpu.VMEM((2, seq, LANES), BF16), pltpu.VMEM((2, nb, LANES), F32),
                        pltpu.VMEM((2, nb, HEAD_DIM + ones_rows, MOBA_BLOCK), BF16),
                        pltpu.VMEM((2, nb, MOBA_BLOCK), F32)],
        compiler_params=_params(2),
        name="moba",
    )(pb, pb, pb, jnp.tile(gq, 2).reshape(1, LANES), jnp.tile(gk, 2).reshape(1, LANES))


def _dilated_kernel(q_ref, k_ref, v_ref, gq_ref, gk_ref, o_ref,
                    qn_ref, kn_ref, qs_ref, kb_ref, vt_ref, og_ref, lse_ref):
    seq = q_ref.shape[1]
    n_blocks = seq // BAND
    ones_rows = vt_ref.shape[2] - HEAD_DIM
    _qk_norm_into(q_ref, qn_ref, gq_ref[...], ATTN_SCALE)
    _qk_norm_into(k_ref, kn_ref, gk_ref[...], 1.0)

    is_a = _lane_is_a((BAND, LANES))
    key_j = lax.broadcasted_iota(jnp.int32, (2 * BAND, BAND), 0)
    qry_i = lax.broadcasted_iota(jnp.int32, (2 * BAND, BAND), 1)
    band_bias = jnp.where((key_j >= qry_i) & (key_j <= qry_i + BAND), 0.0, NEG)
    ones = jnp.ones((ones_rows, BAND), BF16)

    for g, dil in enumerate(DILATIONS):
        per_residue = n_blocks // dil
        stride = None if dil == 1 else dil

        def token_rows(gb, dil=dil, per_residue=per_residue, stride=stride):
            r, c = gb // per_residue, gb % per_residue
            return pl.ds(r + c * (BAND * dil), BAND, stride=stride)

        def gather(i, _, token_rows=token_rows):
            for gb in [DIL_GROUP * i + j for j in range(DIL_GROUP)]:
                rows = token_rows(gb)
                dst = pl.ds(pl.multiple_of(gb * BAND, BAND), BAND)
                q = qn_ref[rows, :]
                qs_ref[0, dst, :] = jnp.where(is_a, q, 0.0).astype(BF16)
                qs_ref[1, dst, :] = jnp.where(is_a, 0.0, q).astype(BF16)
                kb_ref[dst, :] = kn_ref[rows, :].astype(BF16)
                vt = v_ref[0, rows, :].T.astype(BF16)
                for h in range(2):
                    vt_ref[h, gb, :HEAD_DIM, :] = vt[h * HEAD_DIM:(h + 1) * HEAD_DIM]
                    vt_ref[h, gb, HEAD_DIM:, :] = ones
            return 0

        lax.fori_loop(0, n_blocks // DIL_GROUP, gather, 0)

        def band_block(gb, per_residue=per_residue):
            prev = jnp.maximum(gb - 1, 0)
            first = jnp.where(gb % per_residue == 0, NEG, 0.0)
            kb = jnp.concatenate([kb_ref[pl.ds(pl.multiple_of(prev * BAND, BAND), BAND), :],
                                  kb_ref[pl.ds(pl.multiple_of(gb * BAND, BAND), BAND), :]], axis=0)
            q_rows = pl.ds(pl.multiple_of(gb * BAND, BAND), BAND)

            def scores(h):
                s = _mm(kb, qs_ref[h, q_rows, :], _NT) + band_bias
                return jnp.concatenate([s[:BAND] + first, s[BAND:]], axis=0)

            m = [jnp.max(scores(h), axis=0, keepdims=True) for h in range(2)]
            yield
            pv = [_mm(jnp.concatenate([vt_ref[h, prev], vt_ref[h, gb]], axis=1),
                      jnp.exp(scores(h) - m[h]).astype(BF16)) for h in range(2)]
            yield
            l = [x[HEAD_DIM:HEAD_DIM + 1] for x in pv]
            o_t = jnp.concatenate([pv[h][:HEAD_DIM] / l[h] for h in range(2)], axis=0)
            lse_t = jnp.concatenate([jnp.broadcast_to(m[h] + jnp.log(l[h]), (HEAD_DIM, BAND))
                                     for h in range(2)], axis=0)
            return o_t.T, lse_t.T

        def group(i, _, band_block=band_block, token_rows=token_rows, g=g):
            blocks = [DIL_GROUP * i + j for j in range(DIL_GROUP)]
            for gb, (o, lse) in zip(blocks, _round_robin([band_block(gb) for gb in blocks])):
                og_ref[g, token_rows(gb), :] = o
                lse_ref[g, token_rows(gb), :] = lse
            return 0

        lax.fori_loop(0, n_blocks // DIL_GROUP, group, 0)

    tile = 512

    def mix(i, _):
        rows = pl.ds(pl.multiple_of(i * tile, tile), tile)
        lse = [lse_ref[g, rows, :] for g in range(len(DILATIONS))]
        top = functools.reduce(jnp.maximum, lse)
        w = [jnp.exp(x - top) for x in lse]
        o_ref[0, rows, :] = sum(w[g] * og_ref[g, rows, :] for g in range(len(DILATIONS))) / sum(w)
        return 0

    lax.fori_loop(0, seq // tile, mix, 0)


def _dilated(pc, gq, gk):
    bsz, seq, _ = pc.shape
    n_pairs = C_HEADS // 2
    n_pat = len(DILATIONS)
    ones_rows = 16
    slab = lambda off: pl.BlockSpec((1, seq, LANES), lambda b, j: (b, 0, off + j))
    gain = pl.BlockSpec((1, LANES), lambda b, j: (0, 0))
    return pl.pallas_call(
        _dilated_kernel,
        grid=(bsz, n_pairs),
        in_specs=[slab(0), slab(n_pairs), slab(2 * n_pairs), gain, gain],
        out_specs=pl.BlockSpec((1, seq, LANES), lambda b, j: (b, 0, j)),
        out_shape=jax.ShapeDtypeStruct((bsz, seq, C_WIDTH), F32),
        scratch_shapes=[pltpu.VMEM((seq, LANES), F32), pltpu.VMEM((seq, LANES), F32),
                        pltpu.VMEM((2, seq, LANES), BF16), pltpu.VMEM((seq, LANES), BF16),
                        pltpu.VMEM((2, seq // BAND, HEAD_DIM + ones_rows, BAND), BF16),
                        pltpu.VMEM((n_pat, seq, LANES), F32), pltpu.VMEM((n_pat, seq, LANES), F32)],
        compiler_params=_params(2),
        name="dilated",
    )(pc, pc, pc, jnp.tile(gq, 2).reshape(1, LANES), jnp.tile(gk, 2).reshape(1, LANES))


def kernel(x, norm_g, w_in, w_out, tshift_mu, decay_w0, decay_up, iclr_a0, iclr_up,
           k_k, k_a, r_k, lnx_g, lnx_b, moba_q_g, moba_k_g, dil_q_g, dil_k_g):
    bsz, seq, d_model = x.shape
    depth = norm_g.shape[0]
    assert seq % (max(DILATIONS) * BAND) == 0 and seq % MOBA_BLOCK == 0 and seq % RWKV_CHUNK == 0
    x2d = x.reshape(bsz * seq, d_model)
    w_in_bf = w_in.astype(BF16)
    w_out_bf = w_out.astype(BF16)
    for l in range(depth):
        pa, pb, pc, gate = _inproj(x2d, norm_g[l], w_in_bf[l])
        pp, wz = _rwkv_pack_params(decay_w0[l], decay_up[l], iclr_a0[l], iclr_up[l],
                                   k_k[l], k_a[l], r_k[l], lnx_g[l], lnx_b[l])
        ya = _rwkv(pa.reshape(bsz, seq, A_PROJ), tshift_mu[l], pp, wz)
        yb = _moba(pb.reshape(bsz, seq, B_PROJ), moba_q_g[l], moba_k_g[l])
        yc = _dilated(pc.reshape(bsz, seq, C_PROJ), dil_q_g[l], dil_k_g[l])
        x2d = _outproj(x2d, ya.reshape(bsz * seq, A_WIDTH), yb.reshape(bsz * seq, B_WIDTH),
                       yc.reshape(bsz * seq, C_WIDTH), gate, w_out_bf[l])
    return x2d.reshape(bsz, seq, d_model)
```

```python
import functools

import jax
import jax.numpy as jnp
from jax import lax
from jax.experimental import pallas as pl
from jax.experimental.pallas import tpu as pltpu

F32 = jnp.float32
BF16 = jnp.bfloat16

HEAD_DIM = 64
LANES = 128
A_HEADS, B_HEADS, C_HEADS = 6, 4, 6
A_WIDTH, B_WIDTH, C_WIDTH = A_HEADS * HEAD_DIM, B_HEADS * HEAD_DIM, C_HEADS * HEAD_DIM
LORA = 64
A_PROJ = 3 * A_WIDTH + 2 * LORA
B_PROJ = 3 * B_WIDTH
C_PROJ = 3 * C_WIDTH
MIX_WIDTH = A_WIDTH + B_WIDTH + C_WIDTH
PROJ_WIDTH = A_PROJ + B_PROJ + C_PROJ + MIX_WIDTH
MOBA_BLOCK = 256
MOBA_TOPK = 3
DILATIONS = (1, 4, 16)
BAND = 128
DIL_GROUP = 8
RMS_EPS = 1e-6
LNX_EPS = HEAD_DIM * 1e-5
ATTN_SCALE = HEAD_DIM ** -0.5
RWKV_CHUNK = 64
NEG = -1e30
MAX_SAFE_BOUND = 40.0
VMEM_LIMIT = 56 * 1024 * 1024


def _params(n_axes):
    return pltpu.CompilerParams(dimension_semantics=("arbitrary",) * n_axes,
                                vmem_limit_bytes=VMEM_LIMIT)


_NN = (((1,), (0,)), ((), ()))
_NT = (((1,), (1,)), ((), ()))
_TN = (((0,), (0,)), ((), ()))


def _mm(a, b, dims=_NN):
    return lax.dot_general(a, b, dims, preferred_element_type=F32)


def _split(x):
    hi = x.astype(BF16)
    lo = (x - hi.astype(F32)).astype(BF16)
    return hi, lo


def _mm3(a, b, dims=_NN):
    ah, al = _split(a)
    bh, bl = _split(b)
    return _mm(ah, bh, dims) + (_mm(ah, bl, dims) + _mm(al, bh, dims))


def _mm1(a, b, dims=_NN):
    return _mm(a.astype(BF16), b.astype(BF16), dims)


def _half_sum(x, is_a):
    sa = jnp.sum(jnp.where(is_a, x, 0.0), axis=-1, keepdims=True)
    sb = jnp.sum(jnp.where(is_a, 0.0, x), axis=-1, keepdims=True)
    return jnp.where(is_a, sa, sb)


def _half_sum_mxu(x):
    row = lax.broadcasted_iota(jnp.int32, (LANES, LANES), 0) // HEAD_DIM
    col = lax.broadcasted_iota(jnp.int32, (LANES, LANES), 1) // HEAD_DIM
    ones_bd = (row == col).astype(BF16)
    hi, lo = _split(x)
    return _mm(hi, ones_bd) + _mm(lo, ones_bd)


def _round_robin(chains):
    results = [None] * len(chains)
    live = list(range(len(chains)))
    while live:
        for i in list(live):
            try:
                next(chains[i])
            except StopIteration as done:
                results[i] = done.value
                live.remove(i)
    return results


def _staggered(chains):
    results = [None] * len(chains)
    live, started = [], 0
    while live or started < len(chains):
        if started < len(chains):
            live.insert(0, started)
            started += 1
        for i in list(live):
            try:
                next(chains[i])
            except StopIteration as done:
                results[i] = done.value
                live.remove(i)
    return results


def _lane_is_a(shape):
    return lax.broadcasted_iota(jnp.int32, shape, len(shape) - 1) < HEAD_DIM


def _inproj_kernel(x_ref, g_ref, w_ref, pa_ref, pb_ref, pc_ref, gt_ref):
    x = x_ref[...]
    ms = jnp.mean(x * x, axis=-1, keepdims=True)
    h = (x * lax.rsqrt(ms + RMS_EPS) * g_ref[...]).astype(BF16)
    lo = 0
    for ref in (pa_ref, pb_ref, pc_ref, gt_ref):
        hi = lo + ref.shape[-1]
        ref[...] = _mm(h, w_ref[:, lo:hi])
        lo = hi


def _inproj(x2d, g, w_bf16, tm=256):
    m, d = x2d.shape
    widths = (A_PROJ, B_PROJ, C_PROJ, MIX_WIDTH)
    return pl.pallas_call(
        _inproj_kernel,
        grid=(m // tm,),
        in_specs=[pl.BlockSpec((tm, d), lambda i: (i, 0)),
                  pl.BlockSpec((1, d), lambda i: (0, 0)),
                  pl.BlockSpec((d, PROJ_WIDTH), lambda i: (0, 0))],
        out_specs=[pl.BlockSpec((tm, w), lambda i: (i, 0)) for w in widths],
        out_shape=[jax.ShapeDtypeStruct((m, w), F32) for w in widths],
        compiler_params=_params(1),
        name="inproj",
    )(x2d, g.reshape(1, d), w_bf16)


def _outproj_kernel(x_ref, ya_ref, yb_ref, yc_ref, gt_ref, w_ref, o_ref):
    acc = x_ref[...]
    lo = 0
    for y_ref in (ya_ref, yb_ref, yc_ref):
        hi = lo + y_ref.shape[-1]
        g = gt_ref[:, lo:hi]
        y = y_ref[...] * (g * jax.nn.sigmoid(g))
        acc = acc + _mm(y.astype(BF16), w_ref[lo:hi, :])
        lo = hi
    o_ref[...] = acc


def _outproj(x2d, ya, yb, yc, gate, w_bf16, tm=512):
    m, d = x2d.shape
    row = lambda w: pl.BlockSpec((tm, w), lambda i: (i, 0))
    return pl.pallas_call(
        _outproj_kernel,
        grid=(m // tm,),
        in_specs=[row(d), row(A_WIDTH), row(B_WIDTH), row(C_WIDTH), row(MIX_WIDTH),
                  pl.BlockSpec((MIX_WIDTH, d), lambda i: (0, 0))],
        out_specs=row(d),
        out_shape=jax.ShapeDtypeStruct((m, d), F32),
        compiler_params=_params(1),
        name="outproj",
    )(x2d, ya, yb, yc, gate, w_bf16)


_P_W0, _P_A0, _P_KK, _P_KA, _P_RK, _P_LNG, _P_LNB = range(7)
_P_ROWS = 8


def _rwkv_kernel(pa_ref, mu_ref, pp_ref, wz_ref, o_ref, st_ref, prev_ref):
    n_rows, tile = pa_ref.shape[0], pa_ref.shape[1]
    c_len = RWKV_CHUNK
    two_c = 2 * c_len
    n_chunks = tile // c_len
    n_pairs = A_HEADS // 2
    is_a = _lane_is_a((c_len, LANES))

    row_c = lax.broadcasted_iota(jnp.int32, (c_len, A_PROJ), 0)
    ri = lax.broadcasted_iota(jnp.int32, (c_len, c_len), 0)
    ci = lax.broadcasted_iota(jnp.int32, (c_len, c_len), 1)
    tril_c = (ri >= ci).astype(BF16)
    r2 = lax.broadcasted_iota(jnp.int32, (two_c, two_c), 0)
    c2 = lax.broadcasted_iota(jnp.int32, (two_c, two_c), 1)
    same = (r2 // c_len) == (c2 // c_len)
    m_strict = same & ((r2 % c_len) > (c2 % c_len))
    m_incl = same & ((r2 % c_len) >= (c2 % c_len))
    eye2 = (r2 == c2).astype(F32)

    def stack(x):
        return jnp.concatenate([jnp.where(is_a, x, 0.0), jnp.where(is_a, 0.0, x)], axis=0)

    @pl.when(pl.program_id(1) == 0)
    def _():
        st_ref[...] = jnp.zeros_like(st_ref)
        prev_ref[...] = jnp.zeros_like(prev_ref)

    def pair_chunk(b, j, r, k, v, lora_w, lora_a):
        pp = pp_ref[j]
        prow = lambda i: pp[i:i + 1, :]
        w0, a0, k_k, k_a, r_k = prow(_P_W0), prow(_P_A0), prow(_P_KK), prow(_P_KA), prow(_P_RK)
        ln_g, ln_b = prow(_P_LNG), prow(_P_LNB)
        w = -jax.nn.softplus(-(w0 + lora_w)) - 0.5
        lw = -jnp.exp(w)
        a = jax.nn.sigmoid(a0 + lora_a)
        kk = k * k_k
        kk = kk * lax.rsqrt(_half_sum(kk * kk, is_a) + 1e-12)
        k2 = k * (1.0 + (a - 1.0) * k_a)
        kka = kk * a

        l1 = lw.astype(BF16)
        rem = lw - l1.astype(F32)
        l2 = rem.astype(BF16)
        l3 = (rem - l2.astype(F32)).astype(BF16)
        g = _mm(tril_c, l1) + (_mm(tril_c, l2) + _mm(tril_c, l3))
        yield
        g_end = g[c_len - 1:c_len, :]
        e_pos = jnp.exp(g)
        e_neg = jnp.exp(-g)
        e_prev = jnp.exp(g - lw)
        e_tail = jnp.exp(g_end - g)

        ab2 = stack(-kk * e_prev)
        rb2 = stack(r * e_pos)
        bt2 = stack(kka * e_neg)
        kt2 = stack(k2 * e_neg)
        bp2 = stack(kka * e_tail)
        kp2 = stack(k2 * e_tail)
        v2 = stack(v)

        mm = _mm1(jnp.concatenate([ab2, rb2], axis=0), jnp.concatenate([bt2, kt2], axis=0), _NT)
        yield
        l_b = jnp.where(m_strict, mm[:two_c, :two_c], 0.0)
        l_k = jnp.where(m_strict, mm[:two_c, two_c:], 0.0)
        r_b = jnp.where(m_incl, mm[two_c:, :two_c], 0.0)
        r_k2 = jnp.where(m_incl, mm[two_c:, two_c:], 0.0)

        t_inv = eye2 + l_b
        p = _mm1(l_b, l_b)
        kv = _mm1(l_k, v2)
        yield
        steps = c_len.bit_length() - 2
        for i in range(steps):
            if i + 1 < steps:
                tp = _mm1(jnp.concatenate([t_inv, p], axis=0), p)
                yield
                t_inv = t_inv + tp[:two_c]
                p = tp[two_c:]
            else:
                tp = _mm1(t_inv, p)
                yield
                t_inv = t_inv + tp

        tw = _mm1(t_inv, jnp.concatenate([ab2, kv], axis=1))
        yield
        sv = st_ref[b, j]
        ws = _mm1(jnp.concatenate([tw[:, :LANES], rb2], axis=0), sv, _NT)
        yield
        u2 = ws[:two_c] + tw[:, LANES:]
        uv = jnp.concatenate([u2, v2], axis=0)
        y2 = ws[two_c:] + _mm1(jnp.concatenate([r_b, r_k2], axis=1), uv)
        st_ref[b, j] = sv * jnp.exp(g_end) + _mm1(uv, jnp.concatenate([bp2, kp2], axis=0), _TN)
        yield

        y = y2[:c_len] + y2[c_len:]
        mean = _half_sum(y, is_a) * (1.0 / HEAD_DIM)
        yc = y - mean
        var = _half_sum(yc * yc, is_a) * (1.0 / HEAD_DIM)
        y = yc * lax.rsqrt(var + LNX_EPS) * ln_g + ln_b
        return y + _half_sum(r * k2 * r_k, is_a) * v

    mu = mu_ref[...]
    wz = wz_ref[...]

    def body(c, prev_rows):
        t0 = pl.multiple_of(c * c_len, c_len)
        chains, last_rows = [], []
        for b in range(n_rows):
            x = pa_ref[b, pl.ds(t0, c_len), :]
            prev = jnp.where(row_c == 0, prev_rows[b], pltpu.roll(x, 1, axis=0))
            xs = x + (prev - x) * mu
            slab = lambda i, xs=xs: xs[:, i * LANES:(i + 1) * LANES]
            z = slab(3 * n_pairs)
            lora = _mm3(jnp.where(is_a, jnp.tanh(z), z), wz)
            chains += [pair_chunk(b, j, slab(j), slab(n_pairs + j), slab(2 * n_pairs + j),
                                  lora[:, 2 * j * LANES:(2 * j + 1) * LANES],
                                  lora[:, (2 * j + 1) * LANES:(2 * j + 2) * LANES])
                       for j in range(n_pairs)]
            last_rows.append(x[c_len - 1:c_len, :])
        for i, y in enumerate(_round_robin(chains)):
            b, j = divmod(i, n_pairs)
            o_ref[b, pl.ds(t0, c_len), j * LANES:(j + 1) * LANES] = y
        return tuple(last_rows)

    last = lax.fori_loop(0, n_chunks, body, tuple(prev_ref[b, 0:1, :] for b in range(n_rows)))
    for b in range(n_rows):
        prev_ref[b, 0:1, :] = last[b]


def _rwkv(pa, mu, pp, wz, tile=512, rows=4):
    bsz, seq, _ = pa.shape
    n_pairs = A_HEADS // 2
    whole = lambda shape: pl.BlockSpec(shape, lambda b, s: (0,) * len(shape))
    return pl.pallas_call(
        _rwkv_kernel,
        grid=(bsz // rows, seq // tile),
        in_specs=[pl.BlockSpec((rows, tile, A_PROJ), lambda b, s: (b, s, 0)),
                  whole((1, A_PROJ)), whole((n_pairs, _P_ROWS, LANES)), whole((LANES, n_pairs * 2 * LANES))],
        out_specs=pl.BlockSpec((rows, tile, A_WIDTH), lambda b, s: (b, s, 0)),
        out_shape=jax.ShapeDtypeStruct((bsz, seq, A_WIDTH), F32),
        scratch_shapes=[pltpu.VMEM((rows, n_pairs, LANES, LANES), F32), pltpu.VMEM((rows, 8, A_PROJ), F32)],
        compiler_params=_params(2),
        name="rwkv",
    )(pa, mu.reshape(1, A_PROJ), pp, wz)


def _rwkv_pack_params(w0, w_up, a0, a_up, k_k, k_a, r_k, ln_g, ln_b):
    n_pairs = A_HEADS // 2
    pair = lambda t, j: t[j * LANES:(j + 1) * LANES]
    pps, wzs = [], []
    zeros = jnp.zeros((LORA, LANES), F32)
    for j in range(n_pairs):
        rows = {_P_W0: w0, _P_A0: a0, _P_KK: k_k, _P_KA: k_a, _P_RK: r_k.reshape(-1), _P_LNG: ln_g, _P_LNB: ln_b}
        pps.append(jnp.stack([pair(rows[i], j) if i in rows else jnp.zeros((LANES,), F32)
                              for i in range(_P_ROWS)]))
        wzs.append(jnp.concatenate([w_up[:, j * LANES:(j + 1) * LANES], zeros], axis=0))
        wzs.append(jnp.concatenate([zeros, a_up[:, j * LANES:(j + 1) * LANES]], axis=0))
    return jnp.stack(pps), jnp.concatenate(wzs, axis=1)


def _qk_norm_into(src_ref, dst_ref, gain, scale, tile=512):
    seq = dst_ref.shape[0]

    def body(i, _):
        t0 = pl.multiple_of(i * tile, tile)
        x = src_ref[0, pl.ds(t0, tile), :]
        ms = _half_sum_mxu(x * x) * (1.0 / HEAD_DIM)
        dst_ref[pl.ds(t0, tile), :] = x * lax.rsqrt(ms + RMS_EPS) * (gain * scale)
        return 0

    lax.fori_loop(0, seq // tile, body, 0)


def _score_bound(gq, gk):
    return (HEAD_DIM * ATTN_SCALE) * jnp.max(jnp.abs(gq)) * jnp.max(jnp.abs(gk))


def _moba_kernel(q_ref, k_ref, v_ref, gq_ref, gk_ref, o_ref,
                 qn_ref, kb_ref, qs_ref, km_ref, vt_ref, bias_ref):
    seq = q_ref.shape[1]
    blk = MOBA_BLOCK
    nb = seq // blk
    ones_rows = vt_ref.shape[2] - HEAD_DIM
    is_a = _lane_is_a((blk, LANES))
    gq, gk = gq_ref[...], gk_ref[...]

    def prepare(n, _):
        t0 = pl.multiple_of(n * blk, blk)
        rows = pl.ds(t0, blk)
        q = q_ref[0, rows, :]
        qn = q * lax.rsqrt(_half_sum_mxu(q * q) * (1.0 / HEAD_DIM) + RMS_EPS) * gq
        qn_ref[rows, :] = qn
        qs = qn * ATTN_SCALE
        qs_ref[0, rows, :] = jnp.where(is_a, qs, 0.0).astype(BF16)
        qs_ref[1, rows, :] = jnp.where(is_a, 0.0, qs).astype(BF16)
        k = k_ref[0, rows, :]
        kn = k * lax.rsqrt(_half_sum_mxu(k * k) * (1.0 / HEAD_DIM) + RMS_EPS) * gk
        kb_ref[rows, :] = kn.astype(BF16)
        km = jnp.mean(kn, axis=0, keepdims=True)
        km_ref[0, pl.ds(n, 1), :] = jnp.where(is_a[:1], km, 0.0)
        km_ref[1, pl.ds(n, 1), :] = jnp.where(is_a[:1], 0.0, km)
        vt = v_ref[0, rows, :].T.astype(BF16)
        ones = jnp.ones((ones_rows, blk), BF16)
        for h in range(2):
            vt_ref[h, n, :HEAD_DIM, :] = vt[h * HEAD_DIM:(h + 1) * HEAD_DIM]
            vt_ref[h, n, HEAD_DIM:, :] = ones
        return 0

    lax.fori_loop(0, nb, prepare, 0)

    bound = _score_bound(gq, gk)
    bound_is_safe = bound <= MAX_SAFE_BOUND
    bound_row = jnp.full((1, blk), bound, F32)
    blk_row = lax.broadcasted_iota(jnp.int32, (nb, blk), 0)
    key_pos = lax.broadcasted_iota(jnp.int32, (blk, blk), 0)
    qry_pos = lax.broadcasted_iota(jnp.int32, (blk, blk), 1)
    causal = key_pos <= qry_pos

    def two_q_tiles(i, _):
        tiles = (2 * i, 2 * i + 1)
        rows = [pl.ds(pl.multiple_of(qt * blk, blk), blk) for qt in tiles]
        combos = [(t, h) for t in range(2) for h in range(2)]
        for t, h in combos:
            gate = _mm3(km_ref[h], qn_ref[rows[t], :], _NT)
            gate = jnp.where(blk_row < tiles[t], gate, -jnp.inf)
            bias = jnp.full((nb, blk), NEG, F32)
            for _ in range(MOBA_TOPK):
                top = jnp.max(gate, axis=0, keepdims=True)
                hit = (gate == top) & (top > -jnp.inf)
                first = jnp.min(jnp.where(hit, blk_row, nb), axis=0, keepdims=True)
                pick = blk_row == first
                bias = jnp.where(pick, 0.0, bias)
                gate = jnp.where(pick, -jnp.inf, gate)
            bias_ref[t, h] = bias

        qs = {(t, h): qs_ref[h, rows[t], :] for t, h in combos}

        def scores(n, t, h, own):
            s = _mm(kb_ref[pl.ds(pl.multiple_of(n * blk, blk), blk), :], qs[t, h], _NT)
            return jnp.where(causal, s, NEG) if own else s + bias_ref[t, h, pl.ds(n, 1), :]

        tail = [(tiles[0], 0, True), (tiles[0], 1, False), (tiles[1], 1, True)]

        def exact_max():
            def past_pair(j, m):
                s = {(k, t, h): scores(2 * j + k, t, h, False) for k in range(2) for t, h in combos}
                return {(t, h): jnp.maximum(m[t, h], jnp.max(jnp.maximum(s[0, t, h], s[1, t, h]),
                                                                 axis=0, keepdims=True)) for t, h in combos}

            m = {(t, h): jnp.full((1, blk), NEG, F32) for t, h in combos}
            for n, t, own in tail:
                for h in range(2):
                    m[t, h] = jnp.maximum(m[t, h], jnp.max(scores(n, t, h, own), axis=0, keepdims=True))
            return lax.fori_loop(0, i, past_pair, m)

        m = lax.cond(bound_is_safe, lambda: {c: bound_row for c in combos}, exact_max)

        def weighted_v(s, n, t, h):
            return _mm(vt_ref[h, n], jnp.exp(s - m[t, h]).astype(BF16))

        def past_pair(j, pv):
            s = {(k, t, h): scores(2 * j + k, t, h, False) for k in range(2) for t, h in combos}
            return {(t, h): pv[t, h] + weighted_v(s[0, t, h], 2 * j, t, h)
                    + weighted_v(s[1, t, h], 2 * j + 1, t, h) for t, h in combos}

        s = {(k, h): scores(n, t, h, own) for k, (n, t, own) in enumerate(tail) for h in range(2)}
        pv = {(t, h): sum(weighted_v(s[k, h], n, t, h) for k, (n, tt, _) in enumerate(tail) if tt == t)
              for t, h in combos}
        pv = lax.fori_loop(0, i, past_pair, pv)
        for t in range(2):
            o_t = jnp.concatenate([pv[t, h][:HEAD_DIM] / pv[t, h][HEAD_DIM:HEAD_DIM + 1] for h in range(2)],
                                  axis=0)
            o_ref[0, rows[t], :] = o_t.T
        return 0

    lax.fori_loop(0, nb // 2, two_q_tiles, 0)


def _moba(pb, gq, gk):
    bsz, seq, _ = pb.shape
    n_pairs = B_HEADS // 2
    nb = seq // MOBA_BLOCK
    ones_rows = 16
    slab = lambda off: pl.BlockSpec((1, seq, LANES), lambda b, j: (b, 0, off + j))
    gain = pl.BlockSpec((1, LANES), lambda b, j: (0, 0))
    return pl.pallas_call(
        _moba_kernel,
        grid=(bsz, n_pairs),
        in_specs=[slab(0), slab(n_pairs), slab(2 * n_pairs), gain, gain],
        out_specs=pl.BlockSpec((1, seq, LANES), lambda b, j: (b, 0, j)),
        out_shape=jax.ShapeDtypeStruct((bsz, seq, B_WIDTH), F32),
        scratch_shapes=[pltpu.VMEM((seq, LANES), F32), pltpu.VMEM((seq, LANES), BF16),
                        pltpu.VMEM((2, seq, LANES), BF16), pltpu.VMEM((2, nb, LANES), F32),
                        pltpu.VMEM((2, nb, HEAD_DIM + ones_rows, MOBA_BLOCK), BF16),
                        pltpu.VMEM((2, 2, nb, MOBA_BLOCK), F32)],
        compiler_params=_params(2),
        name="moba",
    )(pb, pb, pb, jnp.tile(gq, 2).reshape(1, LANES), jnp.tile(gk, 2).reshape(1, LANES))


def _dilated_kernel(q_ref, k_ref, v_ref, gq_ref, gk_ref, o_ref,
                    qn_ref, kn_ref, qs_ref, kb_ref, vt_ref, og_ref, lse_ref):
    seq = q_ref.shape[1]
    n_blocks = seq // BAND
    ones_rows = vt_ref.shape[2] - HEAD_DIM
    _qk_norm_into(q_ref, qn_ref, gq_ref[...], ATTN_SCALE)
    _qk_norm_into(k_ref, kn_ref, gk_ref[...], 1.0)
    bound = _score_bound(gq_ref[...], gk_ref[...])
    bound_is_safe = bound <= MAX_SAFE_BOUND
    bound_row = jnp.full((1, BAND), bound, F32)

    is_a = _lane_is_a((BAND, LANES))
    key_j = lax.broadcasted_iota(jnp.int32, (2 * BAND, BAND), 0)
    qry_i = lax.broadcasted_iota(jnp.int32, (2 * BAND, BAND), 1)
    band_bias = jnp.where((key_j >= qry_i) & (key_j <= qry_i + BAND), 0.0, NEG)
    band_bias = jnp.concatenate([band_bias, band_bias], axis=1)
    band_bias_less_bound = band_bias - bound
    ones = jnp.ones((ones_rows, BAND), BF16)
    vt_rows = HEAD_DIM + ones_rows

    for g, dil in enumerate(DILATIONS):
        per_residue = n_blocks // dil
        stride = None if dil == 1 else dil

        def token_rows(gb, dil=dil, per_residue=per_residue, stride=stride):
            r, c = gb // per_residue, gb % per_residue
            return pl.ds(r + c * (BAND * dil), BAND, stride=stride)

        def gather(i, _, token_rows=token_rows):
            for gb in [DIL_GROUP * i + j for j in range(DIL_GROUP)]:
                rows = token_rows(gb)
                dst = pl.ds(pl.multiple_of(gb * BAND, BAND), BAND)
                q = qn_ref[rows, :]
                qs_ref[0, dst, :] = jnp.where(is_a, q, 0.0).astype(BF16)
                qs_ref[1, dst, :] = jnp.where(is_a, 0.0, q).astype(BF16)
                kb_ref[dst, :] = kn_ref[rows, :].astype(BF16)
                vt = v_ref[0, rows, :].T.astype(BF16)
                for h in range(2):
                    vt_ref[h, gb, :HEAD_DIM, :] = vt[h * HEAD_DIM:(h + 1) * HEAD_DIM]
                    vt_ref[h, gb, HEAD_DIM:, :] = ones
            return 0

        lax.fori_loop(0, n_blocks // DIL_GROUP, gather, 0)

        def band_block(gb, use_bound, per_residue=per_residue, token_rows=token_rows, g=g):
            prev = jnp.maximum(gb - 1, 0)
            first = jnp.where(gb % per_residue == 0, NEG, 0.0)
            kb = jnp.concatenate([kb_ref[pl.ds(pl.multiple_of(prev * BAND, BAND), BAND), :],
                                  kb_ref[pl.ds(pl.multiple_of(gb * BAND, BAND), BAND), :]], axis=0)
            q_rows = pl.ds(pl.multiple_of(gb * BAND, BAND), BAND)

            def scores(bias):
                q_both = jnp.concatenate([qs_ref[0, q_rows, :], qs_ref[1, q_rows, :]], axis=0)
                s = _mm(kb, q_both, _NT) + bias
                return jnp.concatenate([s[:BAND] + first, s[BAND:]], axis=0)

            if use_bound:
                m = jnp.concatenate([bound_row, bound_row], axis=1)
                s = scores(band_bias_less_bound)
                yield
                p = jnp.exp(s)
            else:
                m = jnp.max(scores(band_bias), axis=0, keepdims=True)
                yield
                s = scores(band_bias)
                yield
                p = jnp.exp(s - m)
            vt_both = jnp.concatenate([jnp.concatenate([vt_ref[h, prev], vt_ref[h, gb]], axis=1)
                                       for h in range(2)], axis=0)
            pv = _mm(vt_both, p.astype(BF16))
            yield
            pv = [pv[h * vt_rows:(h + 1) * vt_rows, h * BAND:(h + 1) * BAND] for h in range(2)]
            l = [x[HEAD_DIM:HEAD_DIM + 1] for x in pv]
            o_t = jnp.concatenate([pv[h][:HEAD_DIM] / l[h] for h in range(2)], axis=0)
            lse_t = jnp.concatenate([jnp.broadcast_to(m[:, h * BAND:(h + 1) * BAND] + jnp.log(l[h]),
                                                      (HEAD_DIM, BAND)) for h in range(2)], axis=0)
            og_ref[g, token_rows(gb), :] = o_t.T
            lse_ref[g, token_rows(gb), :] = lse_t.T

        def group(i, _, use_bound, band_block=band_block):
            _round_robin([band_block(DIL_GROUP * i + j, use_bound) for j in range(DIL_GROUP)])
            return 0

        for use_bound in (True, False):
            @pl.when(bound_is_safe == use_bound)
            def _(use_bound=use_bound, group=group):
                lax.fori_loop(0, n_blocks // DIL_GROUP, functools.partial(group, use_bound=use_bound), 0)

    tile = 512

    def mix(i, _):
        rows = pl.ds(pl.multiple_of(i * tile, tile), tile)
        lse = [lse_ref[g, rows, :] for g in range(len(DILATIONS))]
        top = functools.reduce(jnp.maximum, lse)
        w = [jnp.exp(x - top) for x in lse]
        o_ref[0, rows, :] = sum(w[g] * og_ref[g, rows, :] for g in range(len(DILATIONS))) / sum(w)
        return 0

    lax.fori_loop(0, seq // tile, mix, 0)


def _dilated(pc, gq, gk):
    bsz, seq, _ = pc.shape
    n_pairs = C_HEADS // 2
    n_pat = len(DILATIONS)
    ones_rows = 16
    slab = lambda off: pl.BlockSpec((1, seq, LANES), lambda b, j: (b, 0, off + j))
    gain = pl.BlockSpec((1, LANES), lambda b, j: (0, 0))
    return pl.pallas_call(
        _dilated_kernel,
        grid=(bsz, n_pairs),
        in_specs=[slab(0), slab(n_pairs), slab(2 * n_pairs), gain, gain],
        out_specs=pl.BlockSpec((1, seq, LANES), lambda b, j: (b, 0, j)),
        out_shape=jax.ShapeDtypeStruct((bsz, seq, C_WIDTH), F32),
        scratch_shapes=[pltpu.VMEM((seq, LANES), F32), pltpu.VMEM((seq, LANES), F32),
                        pltpu.VMEM((2, seq, LANES), BF16), pltpu.VMEM((seq, LANES), BF16),
                        pltpu.VMEM((2, seq // BAND, HEAD_DIM + ones_rows, BAND), BF16),
                        pltpu.VMEM((n_pat, seq, LANES), F32), pltpu.VMEM((n_pat, seq, LANES), F32)],
        compiler_params=_params(2),
        name="dilated",
    )(pc, pc, pc, jnp.tile(gq, 2).reshape(1, LANES), jnp.tile(gk, 2).reshape(1, LANES))


def kernel(x, norm_g, w_in, w_out, tshift_mu, decay_w0, decay_up, iclr_a0, iclr_up,
           k_k, k_a, r_k, lnx_g, lnx_b, moba_q_g, moba_k_g, dil_q_g, dil_k_g):
    bsz, seq, d_model = x.shape
    depth = norm_g.shape[0]
    assert seq % (max(DILATIONS) * BAND) == 0 and seq % MOBA_BLOCK == 0 and seq % RWKV_CHUNK == 0
    x2d = x.reshape(bsz * seq, d_model)
    w_in_bf = w_in.astype(BF16)
    w_out_bf = w_out.astype(BF16)
    for l in range(depth):
        pa, pb, pc, gate = _inproj(x2d, norm_g[l], w_in_bf[l])
        pp, wz = _rwkv_pack_params(decay_w0[l], decay_up[l], iclr_a0[l], iclr_up[l],
                                   k_k[l], k_a[l], r_k[l], lnx_g[l], lnx_b[l])
        ya = _rwkv(pa.reshape(bsz, seq, A_PROJ), tshift_mu[l], pp, wz)
        yb = _moba(pb.reshape(bsz, seq, B_PROJ), moba_q_g[l], moba_k_g[l])
        yc = _dilated(pc.reshape(bsz, seq, C_PROJ), dil_q_g[l], dil_k_g[l])
        x2d = _outproj(x2d, ya.reshape(bsz * seq, A_WIDTH), yb.reshape(bsz * seq, B_WIDTH),
                       yc.reshape(bsz * seq, C_WIDTH), gate, w_out_bf[l])
    return x2d.reshape(bsz, seq, d_model)
```

```python
import functools

import jax
import jax.numpy as jnp
from jax import lax
from jax.experimental import pallas as pl
from jax.experimental.pallas import tpu as pltpu

F32 = jnp.float32
BF16 = jnp.bfloat16

HEAD_DIM = 64
LANES = 128
A_HEADS, B_HEADS, C_HEADS = 6, 4, 6
A_WIDTH, B_WIDTH, C_WIDTH = A_HEADS * HEAD_DIM, B_HEADS * HEAD_DIM, C_HEADS * HEAD_DIM
LORA = 64
A_PROJ = 3 * A_WIDTH + 2 * LORA
B_PROJ = 3 * B_WIDTH
C_PROJ = 3 * C_WIDTH
MIX_WIDTH = A_WIDTH + B_WIDTH + C_WIDTH
PROJ_WIDTH = A_PROJ + B_PROJ + C_PROJ + MIX_WIDTH
MOBA_BLOCK = 256
MOBA_TOPK = 3
DILATIONS = (1, 4, 16)
BAND = 128
DIL_GROUP = 8
RMS_EPS = 1e-6
LNX_EPS = HEAD_DIM * 1e-5
ATTN_SCALE = HEAD_DIM ** -0.5
RWKV_CHUNK = 64
NEG = -1e30
MAX_SAFE_BOUND = 40.0
VMEM_LIMIT = 56 * 1024 * 1024


def _params(n_axes):
    return pltpu.CompilerParams(dimension_semantics=("arbitrary",) * n_axes,
                                vmem_limit_bytes=VMEM_LIMIT)


_NN = (((1,), (0,)), ((), ()))
_NT = (((1,), (1,)), ((), ()))
_TN = (((0,), (0,)), ((), ()))


def _mm(a, b, dims=_NN):
    return lax.dot_general(a, b, dims, preferred_element_type=F32)


def _split(x):
    hi = x.astype(BF16)
    lo = (x - hi.astype(F32)).astype(BF16)
    return hi, lo


def _mm3(a, b, dims=_NN):
    ah, al = _split(a)
    bh, bl = _split(b)
    return _mm(ah, bh, dims) + (_mm(ah, bl, dims) + _mm(al, bh, dims))


def _mm1(a, b, dims=_NN):
    return _mm(a.astype(BF16), b.astype(BF16), dims)


def _half_sum(x, is_a):
    sa = jnp.sum(jnp.where(is_a, x, 0.0), axis=-1, keepdims=True)
    sb = jnp.sum(jnp.where(is_a, 0.0, x), axis=-1, keepdims=True)
    return jnp.where(is_a, sa, sb)


def _half_sum_mxu(x):
    row = lax.broadcasted_iota(jnp.int32, (LANES, LANES), 0) // HEAD_DIM
    col = lax.broadcasted_iota(jnp.int32, (LANES, LANES), 1) // HEAD_DIM
    ones_bd = (row == col).astype(BF16)
    hi, lo = _split(x)
    return _mm(hi, ones_bd) + _mm(lo, ones_bd)


def _round_robin(chains):
    results = [None] * len(chains)
    live = list(range(len(chains)))
    while live:
        for i in list(live):
            try:
                next(chains[i])
            except StopIteration as done:
                results[i] = done.value
                live.remove(i)
    return results


def _staggered(chains):
    results = [None] * len(chains)
    live, started = [], 0
    while live or started < len(chains):
        if started < len(chains):
            live.insert(0, started)
            started += 1
        for i in list(live):
            try:
                next(chains[i])
            except StopIteration as done:
                results[i] = done.value
                live.remove(i)
    return results


def _lane_is_a(shape):
    return lax.broadcasted_iota(jnp.int32, shape, len(shape) - 1) < HEAD_DIM


_PROJ_WIDTHS = (A_PROJ, B_PROJ, C_PROJ, MIX_WIDTH)
_PROJ_DTYPES = (F32, BF16, BF16, BF16)


def _proj_kernel(*refs, has_out, has_in):
    refs = list(refs)
    x = refs.pop(0)[...]
    if has_out:
        ya_ref, yb_ref, yc_ref, gt_ref, w_out_ref = (refs.pop(0) for _ in range(5))
    if has_in:
        g_ref, w_in_ref = refs.pop(0), refs.pop(0)
    if has_out:
        lo = 0
        for y_ref in (ya_ref, yb_ref, yc_ref):
            hi = lo + y_ref.shape[-1]
            g = gt_ref[:, lo:hi].astype(F32)
            y = y_ref[...] * (g * jax.nn.sigmoid(g))
            x = x + _mm(y.astype(BF16), w_out_ref[lo:hi, :])
            lo = hi
        refs.pop(0)[...] = x
    if has_in:
        ms = jnp.mean(x * x, axis=-1, keepdims=True)
        h = (x * lax.rsqrt(ms + RMS_EPS) * g_ref[...]).astype(BF16)
        lo = 0
        for ref in refs:
            hi = lo + ref.shape[-1]
            ref[...] = _mm(h, w_in_ref[:, lo:hi]).astype(ref.dtype)
            lo = hi


def _proj(x2d, out_args=None, in_args=None, tm=256):
    m, d = x2d.shape
    row = lambda w: pl.BlockSpec((tm, w), lambda i: (i, 0))
    whole = lambda shape: pl.BlockSpec(shape, lambda i: (0, 0))
    args, in_specs, out_specs, out_shape = [x2d], [row(d)], [], []
    if out_args is not None:
        args += list(out_args)
        in_specs += [row(A_WIDTH), row(B_WIDTH), row(C_WIDTH), row(MIX_WIDTH), whole((MIX_WIDTH, d))]
        out_specs.append(row(d))
        out_shape.append(jax.ShapeDtypeStruct((m, d), F32))
    if in_args is not None:
        gain, w_in = in_args
        args += [gain.reshape(1, d), w_in]
        in_specs += [whole((1, d)), whole((d, PROJ_WIDTH))]
        out_specs += [row(w) for w in _PROJ_WIDTHS]
        out_shape += [jax.ShapeDtypeStruct((m, w), dt) for w, dt in zip(_PROJ_WIDTHS, _PROJ_DTYPES)]
    return pl.pallas_call(
        functools.partial(_proj_kernel, has_out=out_args is not None, has_in=in_args is not None),
        grid=(m // tm,),
        in_specs=in_specs, out_specs=out_specs, out_shape=out_shape,
        compiler_params=_params(1),
        name="proj",
    )(*args)


_P_W0, _P_A0, _P_KK, _P_KA, _P_RK, _P_LNG, _P_LNB = range(7)
_P_ROWS = 8


def _rwkv_kernel(pa_ref, mu_ref, pp_ref, wz_ref, o_ref, st_ref, prev_ref):
    n_rows, tile = pa_ref.shape[0], pa_ref.shape[1]
    c_len = RWKV_CHUNK
    two_c = 2 * c_len
    n_chunks = tile // c_len
    n_pairs = A_HEADS // 2
    is_a = _lane_is_a((c_len, LANES))

    row_c = lax.broadcasted_iota(jnp.int32, (c_len, A_PROJ), 0)
    ri = lax.broadcasted_iota(jnp.int32, (c_len, c_len), 0)
    ci = lax.broadcasted_iota(jnp.int32, (c_len, c_len), 1)
    tril_c = (ri >= ci).astype(BF16)
    r2 = lax.broadcasted_iota(jnp.int32, (two_c, two_c), 0)
    c2 = lax.broadcasted_iota(jnp.int32, (two_c, two_c), 1)
    same = (r2 // c_len) == (c2 // c_len)
    m_strict = same & ((r2 % c_len) > (c2 % c_len))
    m_incl = same & ((r2 % c_len) >= (c2 % c_len))
    eye2 = (r2 == c2).astype(F32)

    def stack(x):
        return jnp.concatenate([jnp.where(is_a, x, 0.0), jnp.where(is_a, 0.0, x)], axis=0)

    @pl.when(pl.program_id(1) == 0)
    def _():
        st_ref[...] = jnp.zeros_like(st_ref)
        prev_ref[...] = jnp.zeros_like(prev_ref)

    def pair_chunk(b, j, r, k, v, lora_w, lora_a):
        pp = pp_ref[j]
        prow = lambda i: pp[i:i + 1, :]
        w0, a0, k_k, k_a, r_k = prow(_P_W0), prow(_P_A0), prow(_P_KK), prow(_P_KA), prow(_P_RK)
        ln_g, ln_b = prow(_P_LNG), prow(_P_LNB)
        w = -jax.nn.softplus(-(w0 + lora_w)) - 0.5
        lw = -jnp.exp(w)
        a = jax.nn.sigmoid(a0 + lora_a)
        kk = k * k_k
        kk = kk * lax.rsqrt(_half_sum(kk * kk, is_a) + 1e-12)
        k2 = k * (1.0 + (a - 1.0) * k_a)
        kka = kk * a

        l1 = lw.astype(BF16)
        rem = lw - l1.astype(F32)
        l2 = rem.astype(BF16)
        l3 = (rem - l2.astype(F32)).astype(BF16)
        g = _mm(tril_c, l1) + (_mm(tril_c, l2) + _mm(tril_c, l3))
        yield
        g_end = g[c_len - 1:c_len, :]
        e_pos = jnp.exp(g)
        e_neg = jnp.exp(-g)
        e_prev = jnp.exp(g - lw)
        e_tail = jnp.exp(g_end - g)

        ab2 = stack(-kk * e_prev)
        rb2 = stack(r * e_pos)
        bt2 = stack(kka * e_neg)
        kt2 = stack(k2 * e_neg)
        bp2 = stack(kka * e_tail)
        kp2 = stack(k2 * e_tail)
        v2 = stack(v)

        mm = _mm1(jnp.concatenate([ab2, rb2], axis=0), jnp.concatenate([bt2, kt2], axis=0), _NT)
        yield
        l_b = jnp.where(m_strict, mm[:two_c, :two_c], 0.0)
        l_k = jnp.where(m_strict, mm[:two_c, two_c:], 0.0)
        r_b = jnp.where(m_incl, mm[two_c:, :two_c], 0.0)
        r_k2 = jnp.where(m_incl, mm[two_c:, two_c:], 0.0)

        t_inv = eye2 + l_b
        p = _mm1(l_b, l_b)
        kv = _mm1(l_k, v2)
        yield
        steps = c_len.bit_length() - 2
        for i in range(steps):
            if i + 1 < steps:
                tp = _mm1(jnp.concatenate([t_inv, p], axis=0), p)
                yield
                t_inv = t_inv + tp[:two_c]
                p = tp[two_c:]
            else:
                tp = _mm1(t_inv, p)
                yield
                t_inv = t_inv + tp

        tw = _mm1(t_inv, jnp.concatenate([ab2, kv], axis=1))
        yield
        sv = st_ref[b, j]
        ws = _mm1(jnp.concatenate([tw[:, :LANES], rb2], axis=0), sv, _NT)
        yield
        u2 = ws[:two_c] + tw[:, LANES:]
        uv = jnp.concatenate([u2, v2], axis=0)
        y2 = ws[two_c:] + _mm1(jnp.concatenate([r_b, r_k2], axis=1), uv)
        st_ref[b, j] = sv * jnp.exp(g_end) + _mm1(uv, jnp.concatenate([bp2, kp2], axis=0), _TN)
        yield

        y = y2[:c_len] + y2[c_len:]
        mean = _half_sum(y, is_a) * (1.0 / HEAD_DIM)
        yc = y - mean
        var = _half_sum(yc * yc, is_a) * (1.0 / HEAD_DIM)
        y = yc * lax.rsqrt(var + LNX_EPS) * ln_g + ln_b
        return y + _half_sum(r * k2 * r_k, is_a) * v

    mu = mu_ref[...]
    wz = wz_ref[...]

    def body(c, prev_rows):
        t0 = pl.multiple_of(c * c_len, c_len)
        chains, last_rows = [], []
        for b in range(n_rows):
            x = pa_ref[b, pl.ds(t0, c_len), :]
            prev = jnp.where(row_c == 0, prev_rows[b], pltpu.roll(x, 1, axis=0))
            xs = x + (prev - x) * mu
            slab = lambda i, xs=xs: xs[:, i * LANES:(i + 1) * LANES]
            z = slab(3 * n_pairs)
            lora = _mm3(jnp.where(is_a, jnp.tanh(z), z), wz)
            chains += [pair_chunk(b, j, slab(j), slab(n_pairs + j), slab(2 * n_pairs + j),
                                  lora[:, 2 * j * LANES:(2 * j + 1) * LANES],
                                  lora[:, (2 * j + 1) * LANES:(2 * j + 2) * LANES])
                       for j in range(n_pairs)]
            last_rows.append(x[c_len - 1:c_len, :])
        for i, y in enumerate(_round_robin(chains)):
            b, j = divmod(i, n_pairs)
            o_ref[b, pl.ds(t0, c_len), j * LANES:(j + 1) * LANES] = y
        return tuple(last_rows)

    last = lax.fori_loop(0, n_chunks, body, tuple(prev_ref[b, 0:1, :] for b in range(n_rows)))
    for b in range(n_rows):
        prev_ref[b, 0:1, :] = last[b]


def _rwkv(pa, mu, pp, wz, tile=512):
    bsz, seq, _ = pa.shape
    rows = next(r for r in (4, 2, 1) if bsz % r == 0)
    n_pairs = A_HEADS // 2
    whole = lambda shape: pl.BlockSpec(shape, lambda b, s: (0,) * len(shape))
    return pl.pallas_call(
        _rwkv_kernel,
        grid=(bsz // rows, seq // tile),
        in_specs=[pl.BlockSpec((rows, tile, A_PROJ), lambda b, s: (b, s, 0)),
                  whole((1, A_PROJ)), whole((n_pairs, _P_ROWS, LANES)), whole((LANES, n_pairs * 2 * LANES))],
        out_specs=pl.BlockSpec((rows, tile, A_WIDTH), lambda b, s: (b, s, 0)),
        out_shape=jax.ShapeDtypeStruct((bsz, seq, A_WIDTH), F32),
        scratch_shapes=[pltpu.VMEM((rows, n_pairs, LANES, LANES), F32), pltpu.VMEM((rows, 8, A_PROJ), F32)],
        compiler_params=_params(2),
        name="rwkv",
    )(pa, mu.reshape(1, A_PROJ), pp, wz)


def _rwkv_pack_params(w0, w_up, a0, a_up, k_k, k_a, r_k, ln_g, ln_b):
    n_pairs = A_HEADS // 2
    pair = lambda t, j: t[j * LANES:(j + 1) * LANES]
    pps, wzs = [], []
    zeros = jnp.zeros((LORA, LANES), F32)
    for j in range(n_pairs):
        rows = {_P_W0: w0, _P_A0: a0, _P_KK: k_k, _P_KA: k_a, _P_RK: r_k.reshape(-1), _P_LNG: ln_g, _P_LNB: ln_b}
        pps.append(jnp.stack([pair(rows[i], j) if i in rows else jnp.zeros((LANES,), F32)
                              for i in range(_P_ROWS)]))
        wzs.append(jnp.concatenate([w_up[:, j * LANES:(j + 1) * LANES], zeros], axis=0))
        wzs.append(jnp.concatenate([zeros, a_up[:, j * LANES:(j + 1) * LANES]], axis=0))
    return jnp.stack(pps), jnp.concatenate(wzs, axis=1)


def _qk_norm_into(src_ref, dst_ref, gain, scale, tile=512):
    seq = dst_ref.shape[0]

    def body(i, _):
        t0 = pl.multiple_of(i * tile, tile)
        x = src_ref[0, pl.ds(t0, tile), :].astype(F32)
        ms = _half_sum_mxu(x * x) * (1.0 / HEAD_DIM)
        dst_ref[pl.ds(t0, tile), :] = x * lax.rsqrt(ms + RMS_EPS) * (gain * scale)
        return 0

    lax.fori_loop(0, seq // tile, body, 0)


def _score_bound(gq, gk):
    return (HEAD_DIM * ATTN_SCALE) * jnp.max(jnp.abs(gq)) * jnp.max(jnp.abs(gk))


def _moba_kernel(q_ref, k_ref, v_ref, gq_ref, gk_ref, o_ref,
                 qn_ref, kb_ref, qs_ref, km_ref, vt_ref, bias_ref):
    seq = q_ref.shape[1]
    blk = MOBA_BLOCK
    nb = seq // blk
    ones_rows = vt_ref.shape[2] - HEAD_DIM
    is_a = _lane_is_a((blk, LANES))
    gq, gk = gq_ref[...], gk_ref[...]

    def prepare(n, _):
        t0 = pl.multiple_of(n * blk, blk)
        rows = pl.ds(t0, blk)
        q = q_ref[0, rows, :].astype(F32)
        qn = q * lax.rsqrt(_half_sum_mxu(q * q) * (1.0 / HEAD_DIM) + RMS_EPS) * gq
        qn_ref[rows, :] = qn
        qs = qn * ATTN_SCALE
        qs_ref[0, rows, :] = jnp.where(is_a, qs, 0.0).astype(BF16)
        qs_ref[1, rows, :] = jnp.where(is_a, 0.0, qs).astype(BF16)
        k = k_ref[0, rows, :].astype(F32)
        kn = k * lax.rsqrt(_half_sum_mxu(k * k) * (1.0 / HEAD_DIM) + RMS_EPS) * gk
        kb_ref[rows, :] = kn.astype(BF16)
        km = jnp.mean(kn, axis=0, keepdims=True)
        km_ref[0, pl.ds(n, 1), :] = jnp.where(is_a[:1], km, 0.0)
        km_ref[1, pl.ds(n, 1), :] = jnp.where(is_a[:1], 0.0, km)
        vt = v_ref[0, rows, :].astype(F32).T.astype(BF16)
        ones = jnp.ones((ones_rows, blk), BF16)
        for h in range(2):
            vt_ref[h, n, :HEAD_DIM, :] = vt[h * HEAD_DIM:(h + 1) * HEAD_DIM]
            vt_ref[h, n, HEAD_DIM:, :] = ones
        return 0

    lax.fori_loop(0, nb, prepare, 0)

    bound = _score_bound(gq, gk)
    bound_is_safe = bound <= MAX_SAFE_BOUND
    bound_row = jnp.full((1, blk), bound, F32)
    blk_row = lax.broadcasted_iota(jnp.int32, (nb, blk), 0)
    key_pos = lax.broadcasted_iota(jnp.int32, (blk, blk), 0)
    qry_pos = lax.broadcasted_iota(jnp.int32, (blk, blk), 1)
    causal = key_pos <= qry_pos

    def two_q_tiles(i, _):
        tiles = (2 * i, 2 * i + 1)
        rows = [pl.ds(pl.multiple_of(qt * blk, blk), blk) for qt in tiles]
        combos = [(t, h) for t in range(2) for h in range(2)]
        for t, h in combos:
            gate = _mm3(km_ref[h], qn_ref[rows[t], :], _NT)
            gate = jnp.where(blk_row < tiles[t], gate, -jnp.inf)
            bias = jnp.full((nb, blk), NEG, F32)
            for _ in range(MOBA_TOPK):
                top = jnp.max(gate, axis=0, keepdims=True)
                hit = (gate == top) & (top > -jnp.inf)
                first = jnp.min(jnp.where(hit, blk_row, nb), axis=0, keepdims=True)
                pick = blk_row == first
                bias = jnp.where(pick, 0.0, bias)
                gate = jnp.where(pick, -jnp.inf, gate)
            bias_ref[t, h] = bias

        qs = {(t, h): qs_ref[h, rows[t], :] for t, h in combos}

        def scores(n, t, h, own):
            s = _mm(kb_ref[pl.ds(pl.multiple_of(n * blk, blk), blk), :], qs[t, h], _NT)
            return jnp.where(causal, s, NEG) if own else s + bias_ref[t, h, pl.ds(n, 1), :]

        tail = [(tiles[0], 0, True), (tiles[0], 1, False), (tiles[1], 1, True)]

        def exact_max():
            def past_pair(j, m):
                s = {(k, t, h): scores(2 * j + k, t, h, False) for k in range(2) for t, h in combos}
                return {(t, h): jnp.maximum(m[t, h], jnp.max(jnp.maximum(s[0, t, h], s[1, t, h]),
                                                                 axis=0, keepdims=True)) for t, h in combos}

            m = {(t, h): jnp.full((1, blk), NEG, F32) for t, h in combos}
            for n, t, own in tail:
                for h in range(2):
                    m[t, h] = jnp.maximum(m[t, h], jnp.max(scores(n, t, h, own), axis=0, keepdims=True))
            return lax.fori_loop(0, i, past_pair, m)

        m = lax.cond(bound_is_safe, lambda: {c: bound_row for c in combos}, exact_max)

        def weighted_v(s, n, t, h):
            return _mm(vt_ref[h, n], jnp.exp(s - m[t, h]).astype(BF16))

        def past_pair(j, pv):
            s = {(k, t, h): scores(2 * j + k, t, h, False) for k in range(2) for t, h in combos}
            return {(t, h): pv[t, h] + weighted_v(s[0, t, h], 2 * j, t, h)
                    + weighted_v(s[1, t, h], 2 * j + 1, t, h) for t, h in combos}

        s = {(k, h): scores(n, t, h, own) for k, (n, t, own) in enumerate(tail) for h in range(2)}
        pv = {(t, h): sum(weighted_v(s[k, h], n, t, h) for k, (n, tt, _) in enumerate(tail) if tt == t)
              for t, h in combos}
        pv = lax.fori_loop(0, i, past_pair, pv)
        for t in range(2):
            o_t = jnp.concatenate([pv[t, h][:HEAD_DIM] / pv[t, h][HEAD_DIM:HEAD_DIM + 1] for h in range(2)],
                                  axis=0)
            o_ref[0, rows[t], :] = o_t.T
        return 0

    lax.fori_loop(0, nb // 2, two_q_tiles, 0)


def _moba(pb, gq, gk):
    bsz, seq, _ = pb.shape
    n_pairs = B_HEADS // 2
    nb = seq // MOBA_BLOCK
    ones_rows = 16
    slab = lambda off: pl.BlockSpec((1, seq, LANES), lambda b, j: (b, 0, off + j))
    gain = pl.BlockSpec((1, LANES), lambda b, j: (0, 0))
    return pl.pallas_call(
        _moba_kernel,
        grid=(bsz, n_pairs),
        in_specs=[slab(0), slab(n_pairs), slab(2 * n_pairs), gain, gain],
        out_specs=pl.BlockSpec((1, seq, LANES), lambda b, j: (b, 0, j)),
        out_shape=jax.ShapeDtypeStruct((bsz, seq, B_WIDTH), F32),
        scratch_shapes=[pltpu.VMEM((seq, LANES), F32), pltpu.VMEM((seq, LANES), BF16),
                        pltpu.VMEM((2, seq, LANES), BF16), pltpu.VMEM((2, nb, LANES), F32),
                        pltpu.VMEM((2, nb, HEAD_DIM + ones_rows, MOBA_BLOCK), BF16),
                        pltpu.VMEM((2, 2, nb, MOBA_BLOCK), F32)],
        compiler_params=_params(2),
        name="moba",
    )(pb, pb, pb, jnp.tile(gq, 2).reshape(1, LANES), jnp.tile(gk, 2).reshape(1, LANES))


def _dilated_kernel(q_ref, k_ref, v_ref, gq_ref, gk_ref, o_ref,
                    qn_ref, kn_ref, vv_ref, qs_ref, kb_ref, vt_ref, og_ref, lse_ref):
    seq = q_ref.shape[1]
    n_blocks = seq // BAND
    ones_rows = vt_ref.shape[2] - HEAD_DIM
    _qk_norm_into(q_ref, qn_ref, gq_ref[...], ATTN_SCALE)
    _qk_norm_into(k_ref, kn_ref, gk_ref[...], 1.0)
    vv_ref[...] = v_ref[0].astype(F32)
    bound = _score_bound(gq_ref[...], gk_ref[...])
    bound_is_safe = bound <= MAX_SAFE_BOUND
    bound_row = jnp.full((1, BAND), bound, F32)

    is_a = _lane_is_a((BAND, LANES))
    key_j = lax.broadcasted_iota(jnp.int32, (2 * BAND, BAND), 0)
    qry_i = lax.broadcasted_iota(jnp.int32, (2 * BAND, BAND), 1)
    band_bias = jnp.where((key_j >= qry_i) & (key_j <= qry_i + BAND), 0.0, NEG)
    band_bias = jnp.concatenate([band_bias, band_bias], axis=1)
    band_bias_less_bound = band_bias - bound
    ones = jnp.ones((ones_rows, BAND), BF16)
    vt_rows = HEAD_DIM + ones_rows

    for g, dil in enumerate(DILATIONS):
        per_residue = n_blocks // dil
        stride = None if dil == 1 else dil

        def token_rows(gb, dil=dil, per_residue=per_residue, stride=stride):
            r, c = gb // per_residue, gb % per_residue
            return pl.ds(r + c * (BAND * dil), BAND, stride=stride)

        def gather(i, _, token_rows=token_rows):
            for gb in [DIL_GROUP * i + j for j in range(DIL_GROUP)]:
                rows = token_rows(gb)
                dst = pl.ds(pl.multiple_of(gb * BAND, BAND), BAND)
                q = qn_ref[rows, :]
                qs_ref[0, dst, :] = jnp.where(is_a, q, 0.0).astype(BF16)
                qs_ref[1, dst, :] = jnp.where(is_a, 0.0, q).astype(BF16)
                kb_ref[dst, :] = kn_ref[rows, :].astype(BF16)
                vt = vv_ref[rows, :].T.astype(BF16)
                for h in range(2):
                    vt_ref[h, gb, :HEAD_DIM, :] = vt[h * HEAD_DIM:(h + 1) * HEAD_DIM]
                    vt_ref[h, gb, HEAD_DIM:, :] = ones
            return 0

        lax.fori_loop(0, n_blocks // DIL_GROUP, gather, 0)

        def band_block(gb, use_bound, per_residue=per_residue, token_rows=token_rows, g=g):
            prev = jnp.maximum(gb - 1, 0)
            first = jnp.where(gb % per_residue == 0, NEG, 0.0)
            kb = jnp.concatenate([kb_ref[pl.ds(pl.multiple_of(prev * BAND, BAND), BAND), :],
                                  kb_ref[pl.ds(pl.multiple_of(gb * BAND, BAND), BAND), :]], axis=0)
            q_rows = pl.ds(pl.multiple_of(gb * BAND, BAND), BAND)

            def scores(bias):
                q_both = jnp.concatenate([qs_ref[0, q_rows, :], qs_ref[1, q_rows, :]], axis=0)
                s = _mm(kb, q_both, _NT) + bias
                return jnp.concatenate([s[:BAND] + first, s[BAND:]], axis=0)

            if use_bound:
                m = jnp.concatenate([bound_row, bound_row], axis=1)
                s = scores(band_bias_less_bound)
                yield
                p = jnp.exp(s)
            else:
                m = jnp.max(scores(band_bias), axis=0, keepdims=True)
                yield
                s = scores(band_bias)
                yield
                p = jnp.exp(s - m)
            vt_both = jnp.concatenate([jnp.concatenate([vt_ref[h, prev], vt_ref[h, gb]], axis=1)
                                       for h in range(2)], axis=0)
            pv = _mm(vt_both, p.astype(BF16))
            yield
            pv = [pv[h * vt_rows:(h + 1) * vt_rows, h * BAND:(h + 1) * BAND] for h in range(2)]
            l = [x[HEAD_DIM:HEAD_DIM + 1] for x in pv]
            o_t = jnp.concatenate([pv[h][:HEAD_DIM] / l[h] for h in range(2)], axis=0)
            lse_t = jnp.concatenate([jnp.broadcast_to(m[:, h * BAND:(h + 1) * BAND] + jnp.log(l[h]),
                                                      (HEAD_DIM, BAND)) for h in range(2)], axis=0)
            og_ref[g, token_rows(gb), :] = o_t.T
            lse_ref[g, token_rows(gb), :] = lse_t.T

        def group(i, _, use_bound, band_block=band_block):
            _round_robin([band_block(DIL_GROUP * i + j, use_bound) for j in range(DIL_GROUP)])
            return 0

        for use_bound in (True, False):
            @pl.when(bound_is_safe == use_bound)
            def _(use_bound=use_bound, group=group):
                lax.fori_loop(0, n_blocks // DIL_GROUP, functools.partial(group, use_bound=use_bound), 0)

    tile = 512

    def mix(i, _):
        rows = pl.ds(pl.multiple_of(i * tile, tile), tile)
        lse = [lse_ref[g, rows, :] for g in range(len(DILATIONS))]
        top = functools.reduce(jnp.maximum, lse)
        w = [jnp.exp(x - top) for x in lse]
        o_ref[0, rows, :] = sum(w[g] * og_ref[g, rows, :] for g in range(len(DILATIONS))) / sum(w)
        return 0

    lax.fori_loop(0, seq // tile, mix, 0)


def _dilated(pc, gq, gk):
    bsz, seq, _ = pc.shape
    n_pairs = C_HEADS // 2
    n_pat = len(DILATIONS)
    ones_rows = 16
    slab = lambda off: pl.BlockSpec((1, seq, LANES), lambda b, j: (b, 0, off + j))
    gain = pl.BlockSpec((1, LANES), lambda b, j: (0, 0))
    return pl.pallas_call(
        _dilated_kernel,
        grid=(bsz, n_pairs),
        in_specs=[slab(0), slab(n_pairs), slab(2 * n_pairs), gain, gain],
        out_specs=pl.BlockSpec((1, seq, LANES), lambda b, j: (b, 0, j)),
        out_shape=jax.ShapeDtypeStruct((bsz, seq, C_WIDTH), F32),
        scratch_shapes=[pltpu.VMEM((seq, LANES), F32), pltpu.VMEM((seq, LANES), F32),
                        pltpu.VMEM((seq, LANES), F32),
                        pltpu.VMEM((2, seq, LANES), BF16), pltpu.VMEM((seq, LANES), BF16),
                        pltpu.VMEM((2, seq // BAND, HEAD_DIM + ones_rows, BAND), BF16),
                        pltpu.VMEM((n_pat, seq, LANES), F32), pltpu.VMEM((n_pat, seq, LANES), F32)],
        compiler_params=_params(2),
        name="dilated",
    )(pc, pc, pc, jnp.tile(gq, 2).reshape(1, LANES), jnp.tile(gk, 2).reshape(1, LANES))


def kernel(x, norm_g, w_in, w_out, tshift_mu, decay_w0, decay_up, iclr_a0, iclr_up,
           k_k, k_a, r_k, lnx_g, lnx_b, moba_q_g, moba_k_g, dil_q_g, dil_k_g):
    bsz, seq, d_model = x.shape
    depth = norm_g.shape[0]
    assert seq % (max(DILATIONS) * BAND) == 0 and seq % MOBA_BLOCK == 0 and seq % RWKV_CHUNK == 0
    x2d = x.reshape(bsz * seq, d_model)
    w_in_bf = w_in.astype(BF16)
    w_out_bf = w_out.astype(BF16)
    pa, pb, pc, gate = _proj(x2d, in_args=(norm_g[0], w_in_bf[0]))
    for l in range(depth):
        pp, wz = _rwkv_pack_params(decay_w0[l], decay_up[l], iclr_a0[l], iclr_up[l],
                                   k_k[l], k_a[l], r_k[l], lnx_g[l], lnx_b[l])
        ya = _rwkv(pa.reshape(bsz, seq, A_PROJ), tshift_mu[l], pp, wz)
        yb = _moba(pb.reshape(bsz, seq, B_PROJ), moba_q_g[l], moba_k_g[l])
        yc = _dilated(pc.reshape(bsz, seq, C_PROJ), dil_q_g[l], dil_k_g[l])
        out_args = (ya.reshape(bsz * seq, A_WIDTH), yb.reshape(bsz * seq, B_WIDTH),
                    yc.reshape(bsz * seq, C_WIDTH), gate, w_out_bf[l])
        if l + 1 < depth:
            x2d, pa, pb, pc, gate = _proj(x2d, out_args, (norm_g[l + 1], w_in_bf[l + 1]))
        else:
            (x2d,) = _proj(x2d, out_args)
    return x2d.reshape(bsz, seq, d_model)
```

```python
import functools

import jax
import jax.numpy as jnp
from jax import lax
from jax.experimental import pallas as pl
from jax.experimental.pallas import tpu as pltpu

F32 = jnp.float32
BF16 = jnp.bfloat16

HEAD_DIM = 64
LANES = 128
A_HEADS, B_HEADS, C_HEADS = 6, 4, 6
A_WIDTH, B_WIDTH, C_WIDTH = A_HEADS * HEAD_DIM, B_HEADS * HEAD_DIM, C_HEADS * HEAD_DIM
LORA = 64
A_PROJ = 3 * A_WIDTH + 2 * LORA
B_PROJ = 3 * B_WIDTH
C_PROJ = 3 * C_WIDTH
MIX_WIDTH = A_WIDTH + B_WIDTH + C_WIDTH
PROJ_WIDTH = A_PROJ + B_PROJ + C_PROJ + MIX_WIDTH
MOBA_BLOCK = 256
MOBA_TOPK = 3
DILATIONS = (1, 4, 16)
BAND = 128
DIL_GROUP = 8
RMS_EPS = 1e-6
LNX_EPS = HEAD_DIM * 1e-5
ATTN_SCALE = HEAD_DIM ** -0.5
RWKV_CHUNK = 64
NEG = -1e30
MAX_SAFE_BOUND = 40.0
VMEM_LIMIT = 56 * 1024 * 1024


def _params(n_axes):
    return pltpu.CompilerParams(dimension_semantics=("arbitrary",) * n_axes,
                                vmem_limit_bytes=VMEM_LIMIT)


_NN = (((1,), (0,)), ((), ()))
_NT = (((1,), (1,)), ((), ()))
_TN = (((0,), (0,)), ((), ()))


def _mm(a, b, dims=_NN):
    return lax.dot_general(a, b, dims, preferred_element_type=F32)


def _split(x):
    hi = x.astype(BF16)
    lo = (x - hi.astype(F32)).astype(BF16)
    return hi, lo


def _mm3(a, b, dims=_NN):
    ah, al = _split(a)
    bh, bl = _split(b)
    return _mm(ah, bh, dims) + (_mm(ah, bl, dims) + _mm(al, bh, dims))


def _mm1(a, b, dims=_NN):
    return _mm(a.astype(BF16), b.astype(BF16), dims)


def _half_sum(x, is_a):
    sa = jnp.sum(jnp.where(is_a, x, 0.0), axis=-1, keepdims=True)
    sb = jnp.sum(jnp.where(is_a, 0.0, x), axis=-1, keepdims=True)
    return jnp.where(is_a, sa, sb)


def _half_sum_mxu(x):
    row = lax.broadcasted_iota(jnp.int32, (LANES, LANES), 0) // HEAD_DIM
    col = lax.broadcasted_iota(jnp.int32, (LANES, LANES), 1) // HEAD_DIM
    ones_bd = (row == col).astype(BF16)
    hi, lo = _split(x)
    return _mm(hi, ones_bd) + _mm(lo, ones_bd)


def _round_robin(chains):
    results = [None] * len(chains)
    live = list(range(len(chains)))
    while live:
        for i in list(live):
            try:
                next(chains[i])
            except StopIteration as done:
                results[i] = done.value
                live.remove(i)
    return results


def _staggered(chains):
    results = [None] * len(chains)
    live, started = [], 0
    while live or started < len(chains):
        if started < len(chains):
            live.insert(0, started)
            started += 1
        for i in list(live):
            try:
                next(chains[i])
            except StopIteration as done:
                results[i] = done.value
                live.remove(i)
    return results


def _lane_is_a(shape):
    return lax.broadcasted_iota(jnp.int32, shape, len(shape) - 1) < HEAD_DIM


_PROJ_WIDTHS = (A_PROJ, B_PROJ, C_PROJ, MIX_WIDTH)
_PROJ_DTYPES = (F32, BF16, BF16, BF16)


def _proj_kernel(*refs, has_out, has_in):
    refs = list(refs)
    x = refs.pop(0)[...]
    if has_out:
        ya_ref, yb_ref, yc_ref, gt_ref, w_out_ref = (refs.pop(0) for _ in range(5))
    if has_in:
        g_ref, w_in_ref = refs.pop(0), refs.pop(0)
    if has_out:
        lo = 0
        for y_ref in (ya_ref, yb_ref, yc_ref):
            hi = lo + y_ref.shape[-1]
            g = gt_ref[:, lo:hi].astype(F32)
            y = y_ref[...] * (g * jax.nn.sigmoid(g))
            x = x + _mm(y.astype(BF16), w_out_ref[lo:hi, :])
            lo = hi
        refs.pop(0)[...] = x
    if has_in:
        ms = jnp.mean(x * x, axis=-1, keepdims=True)
        h = (x * lax.rsqrt(ms + RMS_EPS) * g_ref[...]).astype(BF16)
        lo = 0
        for ref in refs:
            hi = lo + ref.shape[-1]
            ref[...] = _mm(h, w_in_ref[:, lo:hi]).astype(ref.dtype)
            lo = hi


def _proj(x2d, out_args=None, in_args=None, tm=256):
    m, d = x2d.shape
    row = lambda w: pl.BlockSpec((tm, w), lambda i: (i, 0))
    whole = lambda shape: pl.BlockSpec(shape, lambda i: (0, 0))
    args, in_specs, out_specs, out_shape = [x2d], [row(d)], [], []
    if out_args is not None:
        args += list(out_args)
        in_specs += [row(A_WIDTH), row(B_WIDTH), row(C_WIDTH), row(MIX_WIDTH), whole((MIX_WIDTH, d))]
        out_specs.append(row(d))
        out_shape.append(jax.ShapeDtypeStruct((m, d), F32))
    if in_args is not None:
        gain, w_in = in_args
        args += [gain.reshape(1, d), w_in]
        in_specs += [whole((1, d)), whole((d, PROJ_WIDTH))]
        out_specs += [row(w) for w in _PROJ_WIDTHS]
        out_shape += [jax.ShapeDtypeStruct((m, w), dt) for w, dt in zip(_PROJ_WIDTHS, _PROJ_DTYPES)]
    return pl.pallas_call(
        functools.partial(_proj_kernel, has_out=out_args is not None, has_in=in_args is not None),
        grid=(m // tm,),
        in_specs=in_specs, out_specs=out_specs, out_shape=out_shape,
        compiler_params=_params(1),
        name="proj",
    )(*args)


_P_W0, _P_A0, _P_KK, _P_KA, _P_RK, _P_LNG, _P_LNB = range(7)
_P_ROWS = 8


def _rwkv_kernel(pa_ref, mu_ref, pp_ref, wz_ref, o_ref, st_ref, prev_ref):
    n_rows, tile = pa_ref.shape[0], pa_ref.shape[1]
    c_len = RWKV_CHUNK
    two_c = 2 * c_len
    n_chunks = tile // c_len
    n_pairs = A_HEADS // 2
    is_a = _lane_is_a((c_len, LANES))

    row_c = lax.broadcasted_iota(jnp.int32, (c_len, A_PROJ), 0)
    ri = lax.broadcasted_iota(jnp.int32, (c_len, c_len), 0)
    ci = lax.broadcasted_iota(jnp.int32, (c_len, c_len), 1)
    tril_c = (ri >= ci).astype(BF16)
    r2 = lax.broadcasted_iota(jnp.int32, (c_len, two_c), 0)
    c2 = lax.broadcasted_iota(jnp.int32, (c_len, two_c), 1) % c_len
    m_strict = r2 > c2
    m_incl = r2 >= c2
    eye2 = (r2 == c2).astype(F32)

    def stack(x):
        return jnp.concatenate([jnp.where(is_a, x, 0.0), jnp.where(is_a, 0.0, x)], axis=0)

    @pl.when(pl.program_id(1) == 0)
    def _():
        st_ref[...] = jnp.zeros_like(st_ref)
        prev_ref[...] = jnp.zeros_like(prev_ref)

    def pair_chunk(b, j, r, k, v, lora_w, lora_a):
        pp = pp_ref[j]
        prow = lambda i: pp[i:i + 1, :]
        w0, a0, k_k, k_a, r_k = prow(_P_W0), prow(_P_A0), prow(_P_KK), prow(_P_KA), prow(_P_RK)
        ln_g, ln_b = prow(_P_LNG), prow(_P_LNB)
        w = -jax.nn.softplus(-(w0 + lora_w)) - 0.5
        lw = -jnp.exp(w)
        a = jax.nn.sigmoid(a0 + lora_a)
        kk = k * k_k
        kk = kk * lax.rsqrt(_half_sum(kk * kk, is_a) + 1e-12)
        k2 = k * (1.0 + (a - 1.0) * k_a)
        kka = kk * a

        l1 = lw.astype(BF16)
        rem = lw - l1.astype(F32)
        l2 = rem.astype(BF16)
        l3 = (rem - l2.astype(F32)).astype(BF16)
        g = _mm(tril_c, l1) + (_mm(tril_c, l2) + _mm(tril_c, l3))
        yield
        g_end = g[c_len - 1:c_len, :]
        e_pos = jnp.exp(g)
        e_neg = jnp.exp(-g)
        e_prev = jnp.exp(g - lw)
        e_tail = jnp.exp(g_end - g)

        ab = -kk * e_prev
        rb = r * e_pos
        ab2 = stack(ab)
        bt2 = stack(kka * e_neg)
        kt2 = stack(k2 * e_neg)
        bp2 = stack(kka * e_tail)
        kp2 = stack(k2 * e_tail)
        v2 = stack(v)

        mm = _mm1(jnp.concatenate([ab, rb], axis=0), jnp.concatenate([bt2, kt2], axis=0), _NT)
        yield
        l_b = jnp.where(m_strict, mm[:c_len, :two_c], 0.0)
        l_k = jnp.where(m_strict, mm[:c_len, two_c:], 0.0)
        r_b = jnp.where(m_incl, mm[c_len:, :two_c], 0.0)
        r_k2 = jnp.where(m_incl, mm[c_len:, two_c:], 0.0)

        t_inv = eye2 + l_b
        p = _mm1(l_b, stack(l_b))
        kv = _mm1(l_k, v2)
        yield
        steps = c_len.bit_length() - 2
        for i in range(steps):
            if i + 1 < steps:
                tp = _mm1(jnp.concatenate([t_inv, p], axis=0), stack(p))
                yield
                t_inv = t_inv + tp[:c_len]
                p = tp[c_len:]
            else:
                tp = _mm1(t_inv, stack(p))
                yield
                t_inv = t_inv + tp

        tw = _mm1(t_inv, jnp.concatenate([ab2, stack(kv)], axis=1))
        yield
        sv = st_ref[b, j]
        ws = _mm1(jnp.concatenate([tw[:, :LANES], rb], axis=0), sv, _NT)
        yield
        uv = jnp.concatenate([stack(ws[:c_len] + tw[:, LANES:]), v2], axis=0)
        y = ws[c_len:] + _mm1(jnp.concatenate([r_b, r_k2], axis=1), uv)
        st_ref[b, j] = sv * jnp.exp(g_end) + _mm1(uv, jnp.concatenate([bp2, kp2], axis=0), _TN)
        yield

        mean = _half_sum(y, is_a) * (1.0 / HEAD_DIM)
        yc = y - mean
        var = _half_sum(yc * yc, is_a) * (1.0 / HEAD_DIM)
        y = yc * lax.rsqrt(var + LNX_EPS) * ln_g + ln_b
        return y + _half_sum(r * k2 * r_k, is_a) * v

    mu = mu_ref[...]
    wz = wz_ref[...]

    def body(c, prev_rows):
        t0 = pl.multiple_of(c * c_len, c_len)
        chains, last_rows = [], []
        for b in range(n_rows):
            x = pa_ref[b, pl.ds(t0, c_len), :]
            prev = jnp.where(row_c == 0, prev_rows[b], pltpu.roll(x, 1, axis=0))
            xs = x + (prev - x) * mu
            slab = lambda i, xs=xs: xs[:, i * LANES:(i + 1) * LANES]
            z = slab(3 * n_pairs)
            lora = _mm3(jnp.where(is_a, jnp.tanh(z), z), wz)
            chains += [pair_chunk(b, j, slab(j), slab(n_pairs + j), slab(2 * n_pairs + j),
                                  lora[:, 2 * j * LANES:(2 * j + 1) * LANES],
                                  lora[:, (2 * j + 1) * LANES:(2 * j + 2) * LANES])
                       for j in range(n_pairs)]
            last_rows.append(x[c_len - 1:c_len, :])
        for i, y in enumerate(_round_robin(chains)):
            b, j = divmod(i, n_pairs)
            o_ref[b, pl.ds(t0, c_len), j * LANES:(j + 1) * LANES] = y
        return tuple(last_rows)

    last = lax.fori_loop(0, n_chunks, body, tuple(prev_ref[b, 0:1, :] for b in range(n_rows)))
    for b in range(n_rows):
        prev_ref[b, 0:1, :] = last[b]


def _rwkv(pa, mu, pp, wz, tile=512):
    bsz, seq, _ = pa.shape
    rows = next(r for r in (4, 2, 1) if bsz % r == 0)
    n_pairs = A_HEADS // 2
    whole = lambda shape: pl.BlockSpec(shape, lambda b, s: (0,) * len(shape))
    return pl.pallas_call(
        _rwkv_kernel,
        grid=(bsz // rows, seq // tile),
        in_specs=[pl.BlockSpec((rows, tile, A_PROJ), lambda b, s: (b, s, 0)),
                  whole((1, A_PROJ)), whole((n_pairs, _P_ROWS, LANES)), whole((LANES, n_pairs * 2 * LANES))],
        out_specs=pl.BlockSpec((rows, tile, A_WIDTH), lambda b, s: (b, s, 0)),
        out_shape=jax.ShapeDtypeStruct((bsz, seq, A_WIDTH), F32),
        scratch_shapes=[pltpu.VMEM((rows, n_pairs, LANES, LANES), F32), pltpu.VMEM((rows, 8, A_PROJ), F32)],
        compiler_params=_params(2),
        name="rwkv",
    )(pa, mu.reshape(1, A_PROJ), pp, wz)


def _rwkv_pack_params(w0, w_up, a0, a_up, k_k, k_a, r_k, ln_g, ln_b):
    n_pairs = A_HEADS // 2
    pair = lambda t, j: t[j * LANES:(j + 1) * LANES]
    pps, wzs = [], []
    zeros = jnp.zeros((LORA, LANES), F32)
    for j in range(n_pairs):
        rows = {_P_W0: w0, _P_A0: a0, _P_KK: k_k, _P_KA: k_a, _P_RK: r_k.reshape(-1), _P_LNG: ln_g, _P_LNB: ln_b}
        pps.append(jnp.stack([pair(rows[i], j) if i in rows else jnp.zeros((LANES,), F32)
                              for i in range(_P_ROWS)]))
        wzs.append(jnp.concatenate([w_up[:, j * LANES:(j + 1) * LANES], zeros], axis=0))
        wzs.append(jnp.concatenate([zeros, a_up[:, j * LANES:(j + 1) * LANES]], axis=0))
    return jnp.stack(pps), jnp.concatenate(wzs, axis=1)


def _qk_norm_into(src_ref, dst_ref, gain, scale, tile=512):
    seq = dst_ref.shape[0]

    def body(i, _):
        t0 = pl.multiple_of(i * tile, tile)
        x = src_ref[0, pl.ds(t0, tile), :].astype(F32)
        ms = _half_sum_mxu(x * x) * (1.0 / HEAD_DIM)
        dst_ref[pl.ds(t0, tile), :] = x * lax.rsqrt(ms + RMS_EPS) * (gain * scale)
        return 0

    lax.fori_loop(0, seq // tile, body, 0)


def _score_bound(gq, gk):
    return (HEAD_DIM * ATTN_SCALE) * jnp.max(jnp.abs(gq)) * jnp.max(jnp.abs(gk))


def _moba_kernel(q_ref, k_ref, v_ref, gq_ref, gk_ref, o_ref,
                 qn_ref, kb_ref, qs_ref, km_ref, vt_ref, bias_ref):
    seq = q_ref.shape[1]
    blk = MOBA_BLOCK
    nb = seq // blk
    ones_rows = vt_ref.shape[2] - HEAD_DIM
    is_a = _lane_is_a((blk, LANES))
    gq, gk = gq_ref[...], gk_ref[...]

    def prepare(n, _):
        t0 = pl.multiple_of(n * blk, blk)
        rows = pl.ds(t0, blk)
        q = q_ref[0, rows, :].astype(F32)
        qn = q * lax.rsqrt(_half_sum_mxu(q * q) * (1.0 / HEAD_DIM) + RMS_EPS) * gq
        qn_ref[rows, :] = qn
        qs = qn * ATTN_SCALE
        qs_ref[0, rows, :] = jnp.where(is_a, qs, 0.0).astype(BF16)
        qs_ref[1, rows, :] = jnp.where(is_a, 0.0, qs).astype(BF16)
        k = k_ref[0, rows, :].astype(F32)
        kn = k * lax.rsqrt(_half_sum_mxu(k * k) * (1.0 / HEAD_DIM) + RMS_EPS) * gk
        kb_ref[rows, :] = kn.astype(BF16)
        km = jnp.mean(kn, axis=0, keepdims=True)
        km_ref[0, pl.ds(n, 1), :] = jnp.where(is_a[:1], km, 0.0)
        km_ref[1, pl.ds(n, 1), :] = jnp.where(is_a[:1], 0.0, km)
        vt = v_ref[0, rows, :].astype(F32).T.astype(BF16)
        ones = jnp.ones((ones_rows, blk), BF16)
        for h in range(2):
            vt_ref[h, n, :HEAD_DIM, :] = vt[h * HEAD_DIM:(h + 1) * HEAD_DIM]
            vt_ref[h, n, HEAD_DIM:, :] = ones
        return 0

    lax.fori_loop(0, nb, prepare, 0)

    bound = _score_bound(gq, gk)
    bound_is_safe = bound <= MAX_SAFE_BOUND
    bound_row = jnp.full((1, blk), bound, F32)
    blk_row = lax.broadcasted_iota(jnp.int32, (nb, blk), 0)
    key_pos = lax.broadcasted_iota(jnp.int32, (blk, blk), 0)
    qry_pos = lax.broadcasted_iota(jnp.int32, (blk, blk), 1)
    causal = key_pos <= qry_pos

    def two_q_tiles(i, _):
        tiles = (2 * i, 2 * i + 1)
        rows = [pl.ds(pl.multiple_of(qt * blk, blk), blk) for qt in tiles]
        combos = [(t, h) for t in range(2) for h in range(2)]
        for t, h in combos:
            gate = _mm3(km_ref[h], qn_ref[rows[t], :], _NT)
            gate = jnp.where(blk_row < tiles[t], gate, -jnp.inf)
            bias = jnp.full((nb, blk), NEG, F32)
            for _ in range(MOBA_TOPK):
                top = jnp.max(gate, axis=0, keepdims=True)
                hit = (gate == top) & (top > -jnp.inf)
                first = jnp.min(jnp.where(hit, blk_row, nb), axis=0, keepdims=True)
                pick = blk_row == first
                bias = jnp.where(pick, 0.0, bias)
                gate = jnp.where(pick, -jnp.inf, gate)
            bias_ref[t, h] = bias

        qs = {(t, h): qs_ref[h, rows[t], :] for t, h in combos}

        def scores(n, t, h, own):
            s = _mm(kb_ref[pl.ds(pl.multiple_of(n * blk, blk), blk), :], qs[t, h], _NT)
            return jnp.where(causal, s, NEG) if own else s + bias_ref[t, h, pl.ds(n, 1), :]

        tail = [(tiles[0], 0, True), (tiles[0], 1, False), (tiles[1], 1, True)]

        def exact_max():
            def past_pair(j, m):
                s = {(k, t, h): scores(2 * j + k, t, h, False) for k in range(2) for t, h in combos}
                return {(t, h): jnp.maximum(m[t, h], jnp.max(jnp.maximum(s[0, t, h], s[1, t, h]),
                                                                 axis=0, keepdims=True)) for t, h in combos}

            m = {(t, h): jnp.full((1, blk), NEG, F32) for t, h in combos}
            for n, t, own in tail:
                for h in range(2):
                    m[t, h] = jnp.maximum(m[t, h], jnp.max(scores(n, t, h, own), axis=0, keepdims=True))
            return lax.fori_loop(0, i, past_pair, m)

        m = lax.cond(bound_is_safe, lambda: {c: bound_row for c in combos}, exact_max)

        def weighted_v(s, n, t, h):
            return _mm(vt_ref[h, n], jnp.exp(s - m[t, h]).astype(BF16))

        def past_pair(j, pv):
            s = {(k, t, h): scores(2 * j + k, t, h, False) for k in range(2) for t, h in combos}
            return {(t, h): pv[t, h] + weighted_v(s[0, t, h], 2 * j, t, h)
                    + weighted_v(s[1, t, h], 2 * j + 1, t, h) for t, h in combos}

        s = {(k, h): scores(n, t, h, own) for k, (n, t, own) in enumerate(tail) for h in range(2)}
        pv = {(t, h): sum(weighted_v(s[k, h], n, t, h) for k, (n, tt, _) in enumerate(tail) if tt == t)
              for t, h in combos}
        pv = lax.fori_loop(0, i, past_pair, pv)
        for t in range(2):
            o_t = jnp.concatenate([pv[t, h][:HEAD_DIM] / pv[t, h][HEAD_DIM:HEAD_DIM + 1] for h in range(2)],
                                  axis=0)
            o_ref[0, rows[t], :] = o_t.T
        return 0

    lax.fori_loop(0, nb // 2, two_q_tiles, 0)


def _moba(pb, gq, gk):
    bsz, seq, _ = pb.shape
    n_pairs = B_HEADS // 2
    nb = seq // MOBA_BLOCK
    ones_rows = 16
    slab = lambda off: pl.BlockSpec((1, seq, LANES), lambda b, j: (b, 0, off + j))
    gain = pl.BlockSpec((1, LANES), lambda b, j: (0, 0))
    return pl.pallas_call(
        _moba_kernel,
        grid=(bsz, n_pairs),
        in_specs=[slab(0), slab(n_pairs), slab(2 * n_pairs), gain, gain],
        out_specs=pl.BlockSpec((1, seq, LANES), lambda b, j: (b, 0, j)),
        out_shape=jax.ShapeDtypeStruct((bsz, seq, B_WIDTH), F32),
        scratch_shapes=[pltpu.VMEM((seq, LANES), F32), pltpu.VMEM((seq, LANES), BF16),
                        pltpu.VMEM((2, seq, LANES), BF16), pltpu.VMEM((2, nb, LANES), F32),
                        pltpu.VMEM((2, nb, HEAD_DIM + ones_rows, MOBA_BLOCK), BF16),
                        pltpu.VMEM((2, 2, nb, MOBA_BLOCK), F32)],
        compiler_params=_params(2),
        name="moba",
    )(pb, pb, pb, jnp.tile(gq, 2).reshape(1, LANES), jnp.tile(gk, 2).reshape(1, LANES))


def _dilated_kernel(q_ref, k_ref, v_ref, gq_ref, gk_ref, o_ref,
                    qn_ref, kn_ref, vv_ref, qs_ref, kb_ref, vt_ref, og_ref, lse_ref):
    seq = q_ref.shape[1]
    n_blocks = seq // BAND
    ones_rows = vt_ref.shape[2] - HEAD_DIM
    _qk_norm_into(q_ref, qn_ref, gq_ref[...], ATTN_SCALE)
    _qk_norm_into(k_ref, kn_ref, gk_ref[...], 1.0)
    vv_ref[...] = v_ref[0].astype(F32)
    bound = _score_bound(gq_ref[...], gk_ref[...])
    bound_is_safe = bound <= MAX_SAFE_BOUND
    bound_row = jnp.full((1, BAND), bound, F32)

    is_a = _lane_is_a((BAND, LANES))
    key_j = lax.broadcasted_iota(jnp.int32, (2 * BAND, BAND), 0)
    qry_i = lax.broadcasted_iota(jnp.int32, (2 * BAND, BAND), 1)
    band_bias = jnp.where((key_j >= qry_i) & (key_j <= qry_i + BAND), 0.0, NEG)
    band_bias = jnp.concatenate([band_bias, band_bias], axis=1)
    band_bias_less_bound = band_bias - bound
    ones = jnp.ones((ones_rows, BAND), BF16)
    vt_rows = HEAD_DIM + ones_rows

    for g, dil in enumerate(DILATIONS):
        per_residue = n_blocks // dil
        stride = None if dil == 1 else dil

        def token_rows(gb, dil=dil, per_residue=per_residue, stride=stride):
            r, c = gb // per_residue, gb % per_residue
            return pl.ds(r + c * (BAND * dil), BAND, stride=stride)

        def gather(i, _, token_rows=token_rows):
            for gb in [DIL_GROUP * i + j for j in range(DIL_GROUP)]:
                rows = token_rows(gb)
                dst = pl.ds(pl.multiple_of(gb * BAND, BAND), BAND)
                q = qn_ref[rows, :]
                qs_ref[0, dst, :] = jnp.where(is_a, q, 0.0).astype(BF16)
                qs_ref[1, dst, :] = jnp.where(is_a, 0.0, q).astype(BF16)
                kb_ref[dst, :] = kn_ref[rows, :].astype(BF16)
                vt = vv_ref[rows, :].T.astype(BF16)
                for h in range(2):
                    vt_ref[h, gb, :HEAD_DIM, :] = vt[h * HEAD_DIM:(h + 1) * HEAD_DIM]
                    vt_ref[h, gb, HEAD_DIM:, :] = ones
            return 0

        lax.fori_loop(0, n_blocks // DIL_GROUP, gather, 0)

        def band_block(gb, use_bound, per_residue=per_residue, token_rows=token_rows, g=g):
            prev = jnp.maximum(gb - 1, 0)
            first = jnp.where(gb % per_residue == 0, NEG, 0.0)
            kb = jnp.concatenate([kb_ref[pl.ds(pl.multiple_of(prev * BAND, BAND), BAND), :],
                                  kb_ref[pl.ds(pl.multiple_of(gb * BAND, BAND), BAND), :]], axis=0)
            q_rows = pl.ds(pl.multiple_of(gb * BAND, BAND), BAND)

            def scores(bias):
                q_both = jnp.concatenate([qs_ref[0, q_rows, :], qs_ref[1, q_rows, :]], axis=0)
                s = _mm(kb, q_both, _NT) + bias
                return jnp.concatenate([s[:BAND] + first, s[BAND:]], axis=0)

            if use_bound:
                m = jnp.concatenate([bound_row, bound_row], axis=1)
                s = scores(band_bias_less_bound)
                yield
                p = jnp.exp(s)
            else:
                m = jnp.max(scores(band_bias), axis=0, keepdims=True)
                yield
                s = scores(band_bias)
                yield
                p = jnp.exp(s - m)
            vt_both = jnp.concatenate([jnp.concatenate([vt_ref[h, prev], vt_ref[h, gb]], axis=1)
                                       for h in range(2)], axis=0)
            pv = _mm(vt_both, p.astype(BF16))
            yield
            pv = [pv[h * vt_rows:(h + 1) * vt_rows, h * BAND:(h + 1) * BAND] for h in range(2)]
            l = [x[HEAD_DIM:HEAD_DIM + 1] for x in pv]
            o_t = jnp.concatenate([pv[h][:HEAD_DIM] / l[h] for h in range(2)], axis=0)
            lse_t = jnp.concatenate([jnp.broadcast_to(m[:, h * BAND:(h + 1) * BAND] + jnp.log(l[h]),
                                                      (HEAD_DIM, BAND)) for h in range(2)], axis=0)
            og_ref[g, token_rows(gb), :] = o_t.T
            lse_ref[g, token_rows(gb), :] = lse_t.T

        def group(i, _, use_bound, band_block=band_block):
            _round_robin([band_block(DIL_GROUP * i + j, use_bound) for j in range(DIL_GROUP)])
            return 0

        for use_bound in (True, False):
            @pl.when(bound_is_safe == use_bound)
            def _(use_bound=use_bound, group=group):
                lax.fori_loop(0, n_blocks // DIL_GROUP, functools.partial(group, use_bound=use_bound), 0)

    tile = 512

    def mix(i, _):
        rows = pl.ds(pl.multiple_of(i * tile, tile), tile)
        lse = [lse_ref[g, rows, :] for g in range(len(DILATIONS))]
        top = functools.reduce(jnp.maximum, lse)
        w = [jnp.exp(x - top) for x in lse]
        o_ref[0, rows, :] = sum(w[g] * og_ref[g, rows, :] for g in range(len(DILATIONS))) / sum(w)
        return 0

    lax.fori_loop(0, seq // tile, mix, 0)


def _dilated(pc, gq, gk):
    bsz, seq, _ = pc.shape
    n_pairs = C_HEADS // 2
    n_pat = len(DILATIONS)
    ones_rows = 16
    slab = lambda off: pl.BlockSpec((1, seq, LANES), lambda b, j: (b, 0, off + j))
    gain = pl.BlockSpec((1, LANES), lambda b, j: (0, 0))
    return pl.pallas_call(
        _dilated_kernel,
        grid=(bsz, n_pairs),
        in_specs=[slab(0), slab(n_pairs), slab(2 * n_pairs), gain, gain],
        out_specs=pl.BlockSpec((1, seq, LANES), lambda b, j: (b, 0, j)),
        out_shape=jax.ShapeDtypeStruct((bsz, seq, C_WIDTH), F32),
        scratch_shapes=[pltpu.VMEM((seq, LANES), F32), pltpu.VMEM((seq, LANES), F32),
                        pltpu.VMEM((seq, LANES), F32),
                        pltpu.VMEM((2, seq, LANES), BF16), pltpu.VMEM((seq, LANES), BF16),
                        pltpu.VMEM((2, seq // BAND, HEAD_DIM + ones_rows, BAND), BF16),
                        pltpu.VMEM((n_pat, seq, LANES), F32), pltpu.VMEM((n_pat, seq, LANES), F32)],
        compiler_params=_params(2),
        name="dilated",
    )(pc, pc, pc, jnp.tile(gq, 2).reshape(1, LANES), jnp.tile(gk, 2).reshape(1, LANES))


def kernel(x, norm_g, w_in, w_out, tshift_mu, decay_w0, decay_up, iclr_a0, iclr_up,
           k_k, k_a, r_k, lnx_g, lnx_b, moba_q_g, moba_k_g, dil_q_g, dil_k_g):
    bsz, seq, d_model = x.shape
    depth = norm_g.shape[0]
    assert seq % (max(DILATIONS) * BAND) == 0 and seq % MOBA_BLOCK == 0 and seq % RWKV_CHUNK == 0
    x2d = x.reshape(bsz * seq, d_model)
    w_in_bf = w_in.astype(BF16)
    w_out_bf = w_out.astype(BF16)
    pa, pb, pc, gate = _proj(x2d, in_args=(norm_g[0], w_in_bf[0]))
    for l in range(depth):
        pp, wz = _rwkv_pack_params(decay_w0[l], decay_up[l], iclr_a0[l], iclr_up[l],
                                   k_k[l], k_a[l], r_k[l], lnx_g[l], lnx_b[l])
        ya = _rwkv(pa.reshape(bsz, seq, A_PROJ), tshift_mu[l], pp, wz)
        yb = _moba(pb.reshape(bsz, seq, B_PROJ), moba_q_g[l], moba_k_g[l])
        yc = _dilated(pc.reshape(bsz, seq, C_PROJ), dil_q_g[l], dil_k_g[l])
        out_args = (ya.reshape(bsz * seq, A_WIDTH), yb.reshape(bsz * seq, B_WIDTH),
                    yc.reshape(bsz * seq, C_WIDTH), gate, w_out_bf[l])
        if l + 1 < depth:
            x2d, pa, pb, pc, gate = _proj(x2d, out_args, (norm_g[l + 1], w_in_bf[l + 1]))
        else:
            (x2d,) = _proj(x2d, out_args)
    return x2d.reshape(bsz, seq, d_model)
```

```python
import functools

import jax
import jax.numpy as jnp
from jax import lax
from jax.experimental import pallas as pl
from jax.experimental.pallas import tpu as pltpu

F32 = jnp.float32
BF16 = jnp.bfloat16

HEAD_DIM = 64
LANES = 128
A_HEADS, B_HEADS, C_HEADS = 6, 4, 6
A_WIDTH, B_WIDTH, C_WIDTH = A_HEADS * HEAD_DIM, B_HEADS * HEAD_DIM, C_HEADS * HEAD_DIM
LORA = 64
A_PROJ = 3 * A_WIDTH + 2 * LORA
B_PROJ = 3 * B_WIDTH
C_PROJ = 3 * C_WIDTH
MIX_WIDTH = A_WIDTH + B_WIDTH + C_WIDTH
PROJ_WIDTH = A_PROJ + B_PROJ + C_PROJ + MIX_WIDTH
MOBA_BLOCK = 256
MOBA_TOPK = 3
DILATIONS = (1, 4, 16)
BAND = 128
DIL_GROUP = 16
RMS_EPS = 1e-6
LNX_EPS = HEAD_DIM * 1e-5
ATTN_SCALE = HEAD_DIM ** -0.5
RWKV_CHUNK = 64
NEG = -1e30
MAX_SAFE_BOUND = 40.0
VMEM_LIMIT = 56 * 1024 * 1024


def _params(n_axes):
    return pltpu.CompilerParams(dimension_semantics=("arbitrary",) * n_axes,
                                vmem_limit_bytes=VMEM_LIMIT)


_NN = (((1,), (0,)), ((), ()))
_NT = (((1,), (1,)), ((), ()))
_TN = (((0,), (0,)), ((), ()))


def _mm(a, b, dims=_NN):
    return lax.dot_general(a, b, dims, preferred_element_type=F32)


def _split(x):
    hi = x.astype(BF16)
    lo = (x - hi.astype(F32)).astype(BF16)
    return hi, lo


def _mm3(a, b, dims=_NN):
    ah, al = _split(a)
    bh, bl = _split(b)
    return _mm(ah, bh, dims) + (_mm(ah, bl, dims) + _mm(al, bh, dims))


def _mm1(a, b, dims=_NN):
    return _mm(a.astype(BF16), b.astype(BF16), dims)


def _half_sum(x, is_a):
    sa = jnp.sum(jnp.where(is_a, x, 0.0), axis=-1, keepdims=True)
    sb = jnp.sum(jnp.where(is_a, 0.0, x), axis=-1, keepdims=True)
    return jnp.where(is_a, sa, sb)


def _half_sum_mxu(x):
    row = lax.broadcasted_iota(jnp.int32, (LANES, LANES), 0) // HEAD_DIM
    col = lax.broadcasted_iota(jnp.int32, (LANES, LANES), 1) // HEAD_DIM
    ones_bd = (row == col).astype(BF16)
    hi, lo = _split(x)
    return _mm(hi, ones_bd) + _mm(lo, ones_bd)


def _round_robin(chains):
    results = [None] * len(chains)
    live = list(range(len(chains)))
    while live:
        for i in list(live):
            try:
                next(chains[i])
            except StopIteration as done:
                results[i] = done.value
                live.remove(i)
    return results


def _staggered(chains):
    results = [None] * len(chains)
    live, started = [], 0
    while live or started < len(chains):
        if started < len(chains):
            live.insert(0, started)
            started += 1
        for i in list(live):
            try:
                next(chains[i])
            except StopIteration as done:
                results[i] = done.value
                live.remove(i)
    return results


def _lane_is_a(shape):
    return lax.broadcasted_iota(jnp.int32, shape, len(shape) - 1) < HEAD_DIM


_PROJ_WIDTHS = (A_PROJ, B_PROJ, C_PROJ, MIX_WIDTH)
_PROJ_DTYPES = (F32, BF16, BF16, BF16)


def _proj_kernel(*refs, has_out, has_in):
    refs = list(refs)
    x = refs.pop(0)[...]
    if has_out:
        ya_ref, yb_ref, yc_ref, gt_ref, w_out_ref = (refs.pop(0) for _ in range(5))
    if has_in:
        g_ref, w_in_ref = refs.pop(0), refs.pop(0)
    if has_out:
        lo = 0
        for y_ref in (ya_ref, yb_ref, yc_ref):
            hi = lo + y_ref.shape[-1]
            g = gt_ref[:, lo:hi].astype(F32)
            y = y_ref[...] * (g * jax.nn.sigmoid(g))
            x = x + _mm(y.astype(BF16), w_out_ref[lo:hi, :])
            lo = hi
        refs.pop(0)[...] = x
    if has_in:
        ms = jnp.mean(x * x, axis=-1, keepdims=True)
        h = (x * lax.rsqrt(ms + RMS_EPS) * g_ref[...]).astype(BF16)
        lo = 0
        for ref in refs:
            hi = lo + ref.shape[-1]
            ref[...] = _mm(h, w_in_ref[:, lo:hi]).astype(ref.dtype)
            lo = hi


def _proj(x2d, out_args=None, in_args=None, tm=512):
    m, d = x2d.shape
    row = lambda w: pl.BlockSpec((tm, w), lambda i: (i, 0))
    whole = lambda shape: pl.BlockSpec(shape, lambda i: (0, 0))
    args, in_specs, out_specs, out_shape = [x2d], [row(d)], [], []
    if out_args is not None:
        args += list(out_args)
        in_specs += [row(A_WIDTH), row(B_WIDTH), row(C_WIDTH), row(MIX_WIDTH), whole((MIX_WIDTH, d))]
        out_specs.append(row(d))
        out_shape.append(jax.ShapeDtypeStruct((m, d), F32))
    if in_args is not None:
        gain, w_in = in_args
        args += [gain.reshape(1, d), w_in]
        in_specs += [whole((1, d)), whole((d, PROJ_WIDTH))]
        out_specs += [row(w) for w in _PROJ_WIDTHS]
        out_shape += [jax.ShapeDtypeStruct((m, w), dt) for w, dt in zip(_PROJ_WIDTHS, _PROJ_DTYPES)]
    return pl.pallas_call(
        functools.partial(_proj_kernel, has_out=out_args is not None, has_in=in_args is not None),
        grid=(m // tm,),
        in_specs=in_specs, out_specs=out_specs, out_shape=out_shape,
        compiler_params=_params(1),
        name="proj",
    )(*args)


_P_W0, _P_A0, _P_KK, _P_KA, _P_RK, _P_LNG, _P_LNB = range(7)
_P_ROWS = 8


def _rwkv_kernel(pa_ref, mu_ref, pp_ref, wz_ref, o_ref, st_ref, prev_ref):
    n_rows, tile = pa_ref.shape[0], pa_ref.shape[1]
    c_len = RWKV_CHUNK
    two_c = 2 * c_len
    n_chunks = tile // c_len
    n_pairs = A_HEADS // 2
    is_a = _lane_is_a((c_len, LANES))

    row_c = lax.broadcasted_iota(jnp.int32, (c_len, A_PROJ), 0)
    ri = lax.broadcasted_iota(jnp.int32, (c_len, c_len), 0)
    ci = lax.broadcasted_iota(jnp.int32, (c_len, c_len), 1)
    tril_c = (ri >= ci).astype(BF16)
    r2 = lax.broadcasted_iota(jnp.int32, (c_len, two_c), 0)
    c2 = lax.broadcasted_iota(jnp.int32, (c_len, two_c), 1) % c_len
    m_strict = r2 > c2
    m_incl = r2 >= c2
    eye2 = (r2 == c2).astype(F32)

    def stack(x):
        return jnp.concatenate([jnp.where(is_a, x, 0.0), jnp.where(is_a, 0.0, x)], axis=0)

    @pl.when(pl.program_id(1) == 0)
    def _():
        st_ref[...] = jnp.zeros_like(st_ref)
        prev_ref[...] = jnp.zeros_like(prev_ref)

    def pair_chunk(b, j, r, k, v, lora_w, lora_a):
        pp = pp_ref[j]
        prow = lambda i: pp[i:i + 1, :]
        w0, a0, k_k, k_a, r_k = prow(_P_W0), prow(_P_A0), prow(_P_KK), prow(_P_KA), prow(_P_RK)
        ln_g, ln_b = prow(_P_LNG), prow(_P_LNB)
        w = -jax.nn.softplus(-(w0 + lora_w)) - 0.5
        lw = -jnp.exp(w)
        a = jax.nn.sigmoid(a0 + lora_a)
        kk = k * k_k
        kk = kk * lax.rsqrt(_half_sum(kk * kk, is_a) + 1e-12)
        k2 = k * (1.0 + (a - 1.0) * k_a)
        kka = kk * a

        l1 = lw.astype(BF16)
        rem = lw - l1.astype(F32)
        l2 = rem.astype(BF16)
        l3 = (rem - l2.astype(F32)).astype(BF16)
        g = _mm(tril_c, l1) + (_mm(tril_c, l2) + _mm(tril_c, l3))
        yield
        g_end = g[c_len - 1:c_len, :]
        e_pos = jnp.exp(g)
        e_neg = jnp.exp(-g)
        e_prev = jnp.exp(g - lw)
        e_tail = jnp.exp(g_end - g)

        ab = -kk * e_prev
        rb = r * e_pos
        ab2 = stack(ab)
        bt2 = stack(kka * e_neg)
        kt2 = stack(k2 * e_neg)
        bp2 = stack(kka * e_tail)
        kp2 = stack(k2 * e_tail)
        v2 = stack(v)

        mm = _mm1(jnp.concatenate([ab, rb], axis=0), jnp.concatenate([bt2, kt2], axis=0), _NT)
        yield
        l_b = jnp.where(m_strict, mm[:c_len, :two_c], 0.0)
        l_k = jnp.where(m_strict, mm[:c_len, two_c:], 0.0)
        r_b = jnp.where(m_incl, mm[c_len:, :two_c], 0.0)
        r_k2 = jnp.where(m_incl, mm[c_len:, two_c:], 0.0)

        t_inv = eye2 + l_b
        p = _mm1(l_b, stack(l_b))
        kv = _mm1(l_k, v2)
        yield
        steps = c_len.bit_length() - 2
        for i in range(steps):
            if i + 1 < steps:
                tp = _mm1(jnp.concatenate([t_inv, p], axis=0), stack(p))
                yield
                t_inv = t_inv + tp[:c_len]
                p = tp[c_len:]
            else:
                tp = _mm1(t_inv, stack(p))
                yield
                t_inv = t_inv + tp

        tw = _mm1(t_inv, jnp.concatenate([ab2, stack(kv)], axis=1))
        yield
        sv = st_ref[b, j]
        ws = _mm1(jnp.concatenate([tw[:, :LANES], rb], axis=0), sv, _NT)
        yield
        uv = jnp.concatenate([stack(ws[:c_len] + tw[:, LANES:]), v2], axis=0)
        y = ws[c_len:] + _mm1(jnp.concatenate([r_b, r_k2], axis=1), uv)
        st_ref[b, j] = sv * jnp.exp(g_end) + _mm1(uv, jnp.concatenate([bp2, kp2], axis=0), _TN)
        yield

        mean = _half_sum(y, is_a) * (1.0 / HEAD_DIM)
        yc = y - mean
        var = _half_sum(yc * yc, is_a) * (1.0 / HEAD_DIM)
        y = yc * lax.rsqrt(var + LNX_EPS) * ln_g + ln_b
        return y + _half_sum(r * k2 * r_k, is_a) * v

    mu = mu_ref[...]
    wz = wz_ref[...]

    def body(c, prev_rows):
        t0 = pl.multiple_of(c * c_len, c_len)
        chains, last_rows = [], []
        for b in range(n_rows):
            x = pa_ref[b, pl.ds(t0, c_len), :]
            prev = jnp.where(row_c == 0, prev_rows[b], pltpu.roll(x, 1, axis=0))
            xs = x + (prev - x) * mu
            slab = lambda i, xs=xs: xs[:, i * LANES:(i + 1) * LANES]
            z = slab(3 * n_pairs)
            lora = _mm3(jnp.where(is_a, jnp.tanh(z), z), wz)
            chains += [pair_chunk(b, j, slab(j), slab(n_pairs + j), slab(2 * n_pairs + j),
                                  lora[:, 2 * j * LANES:(2 * j + 1) * LANES],
                                  lora[:, (2 * j + 1) * LANES:(2 * j + 2) * LANES])
                       for j in range(n_pairs)]
            last_rows.append(x[c_len - 1:c_len, :])
        for i, y in enumerate(_round_robin(chains)):
            b, j = divmod(i, n_pairs)
            o_ref[b, pl.ds(t0, c_len), j * LANES:(j + 1) * LANES] = y
        return tuple(last_rows)

    last = lax.fori_loop(0, n_chunks, body, tuple(prev_ref[b, 0:1, :] for b in range(n_rows)))
    for b in range(n_rows):
        prev_ref[b, 0:1, :] = last[b]


def _rwkv(pa, mu, pp, wz, tile=512):
    bsz, seq, _ = pa.shape
    rows = next(r for r in (4, 2, 1) if bsz % r == 0)
    n_pairs = A_HEADS // 2
    whole = lambda shape: pl.BlockSpec(shape, lambda b, s: (0,) * len(shape))
    return pl.pallas_call(
        _rwkv_kernel,
        grid=(bsz // rows, seq // tile),
        in_specs=[pl.BlockSpec((rows, tile, A_PROJ), lambda b, s: (b, s, 0)),
                  whole((1, A_PROJ)), whole((n_pairs, _P_ROWS, LANES)), whole((LANES, n_pairs * 2 * LANES))],
        out_specs=pl.BlockSpec((rows, tile, A_WIDTH), lambda b, s: (b, s, 0)),
        out_shape=jax.ShapeDtypeStruct((bsz, seq, A_WIDTH), F32),
        scratch_shapes=[pltpu.VMEM((rows, n_pairs, LANES, LANES), F32), pltpu.VMEM((rows, 8, A_PROJ), F32)],
        compiler_params=_params(2),
        name="rwkv",
    )(pa, mu.reshape(1, A_PROJ), pp, wz)


def _rwkv_pack_params(w0, w_up, a0, a_up, k_k, k_a, r_k, ln_g, ln_b):
    n_pairs = A_HEADS // 2
    pair = lambda t, j: t[j * LANES:(j + 1) * LANES]
    pps, wzs = [], []
    zeros = jnp.zeros((LORA, LANES), F32)
    for j in range(n_pairs):
        rows = {_P_W0: w0, _P_A0: a0, _P_KK: k_k, _P_KA: k_a, _P_RK: r_k.reshape(-1), _P_LNG: ln_g, _P_LNB: ln_b}
        pps.append(jnp.stack([pair(rows[i], j) if i in rows else jnp.zeros((LANES,), F32)
                              for i in range(_P_ROWS)]))
        wzs.append(jnp.concatenate([w_up[:, j * LANES:(j + 1) * LANES], zeros], axis=0))
        wzs.append(jnp.concatenate([zeros, a_up[:, j * LANES:(j + 1) * LANES]], axis=0))
    return jnp.stack(pps), jnp.concatenate(wzs, axis=1)


def _qk_norm_into(src_ref, dst_ref, gain, scale, tile=512):
    seq = dst_ref.shape[0]

    def body(i, _):
        t0 = pl.multiple_of(i * tile, tile)
        x = src_ref[0, pl.ds(t0, tile), :].astype(F32)
        ms = _half_sum_mxu(x * x) * (1.0 / HEAD_DIM)
        dst_ref[pl.ds(t0, tile), :] = x * lax.rsqrt(ms + RMS_EPS) * (gain * scale)
        return 0

    lax.fori_loop(0, seq // tile, body, 0)


def _score_bound(gq, gk):
    return (HEAD_DIM * ATTN_SCALE) * jnp.max(jnp.abs(gq)) * jnp.max(jnp.abs(gk))


def _moba_kernel(q_ref, k_ref, v_ref, gq_ref, gk_ref, o_ref,
                 qn_ref, kb_ref, qs_ref, km_ref, vt_ref, bias_ref):
    seq = q_ref.shape[1]
    blk = MOBA_BLOCK
    nb = seq // blk
    ones_rows = vt_ref.shape[2] - HEAD_DIM
    is_a = _lane_is_a((blk, LANES))
    gq, gk = gq_ref[...], gk_ref[...]

    def prepare(n, _):
        t0 = pl.multiple_of(n * blk, blk)
        rows = pl.ds(t0, blk)
        q = q_ref[0, rows, :].astype(F32)
        qn = q * lax.rsqrt(_half_sum_mxu(q * q) * (1.0 / HEAD_DIM) + RMS_EPS) * gq
        qn_ref[rows, :] = qn
        qs = qn * ATTN_SCALE
        qs_ref[0, rows, :] = jnp.where(is_a, qs, 0.0).astype(BF16)
        qs_ref[1, rows, :] = jnp.where(is_a, 0.0, qs).astype(BF16)
        k = k_ref[0, rows, :].astype(F32)
        kn = k * lax.rsqrt(_half_sum_mxu(k * k) * (1.0 / HEAD_DIM) + RMS_EPS) * gk
        kb_ref[rows, :] = kn.astype(BF16)
        km = jnp.mean(kn, axis=0, keepdims=True)
        km_ref[0, pl.ds(n, 1), :] = jnp.where(is_a[:1], km, 0.0)
        km_ref[1, pl.ds(n, 1), :] = jnp.where(is_a[:1], 0.0, km)
        vt = v_ref[0, rows, :].astype(F32).T.astype(BF16)
        ones = jnp.ones((ones_rows, blk), BF16)
        for h in range(2):
            vt_ref[h, n, :HEAD_DIM, :] = vt[h * HEAD_DIM:(h + 1) * HEAD_DIM]
            vt_ref[h, n, HEAD_DIM:, :] = ones
        return 0

    lax.fori_loop(0, nb, prepare, 0)

    bound = _score_bound(gq, gk)
    bound_is_safe = bound <= MAX_SAFE_BOUND
    bound_row = jnp.full((1, blk), bound, F32)
    blk_row = lax.broadcasted_iota(jnp.int32, (nb, blk), 0)
    key_pos = lax.broadcasted_iota(jnp.int32, (blk, blk), 0)
    qry_pos = lax.broadcasted_iota(jnp.int32, (blk, blk), 1)
    causal = key_pos <= qry_pos

    def two_q_tiles(i, _):
        tiles = (2 * i, 2 * i + 1)
        rows = [pl.ds(pl.multiple_of(qt * blk, blk), blk) for qt in tiles]
        combos = [(t, h) for t in range(2) for h in range(2)]
        for t, h in combos:
            gate = _mm3(km_ref[h], qn_ref[rows[t], :], _NT)
            gate = jnp.where(blk_row < tiles[t], gate, -jnp.inf)
            bias = jnp.full((nb, blk), NEG, F32)
            for _ in range(MOBA_TOPK):
                top = jnp.max(gate, axis=0, keepdims=True)
                hit = (gate == top) & (top > -jnp.inf)
                first = jnp.min(jnp.where(hit, blk_row, nb), axis=0, keepdims=True)
                pick = blk_row == first
                bias = jnp.where(pick, 0.0, bias)
                gate = jnp.where(pick, -jnp.inf, gate)
            bias_ref[t, h] = bias

        qs = {(t, h): qs_ref[h, rows[t], :] for t, h in combos}

        def scores(n, t, h, own, minus=0.0):
            s = _mm(kb_ref[pl.ds(pl.multiple_of(n * blk, blk), blk), :], qs[t, h], _NT)
            if own:
                return jnp.where(causal, s - minus, NEG)
            return s + (bias_ref[t, h, pl.ds(n, 1), :] - minus)

        tail = [(tiles[0], 0, True), (tiles[0], 1, False), (tiles[1], 1, True)]

        def exact_max():
            def past_pair(j, m):
                s = {(k, t, h): scores(2 * j + k, t, h, False) for k in range(2) for t, h in combos}
                return {(t, h): jnp.maximum(m[t, h], jnp.max(jnp.maximum(s[0, t, h], s[1, t, h]),
                                                                 axis=0, keepdims=True)) for t, h in combos}

            m = {(t, h): jnp.full((1, blk), NEG, F32) for t, h in combos}
            for n, t, own in tail:
                for h in range(2):
                    m[t, h] = jnp.maximum(m[t, h], jnp.max(scores(n, t, h, own), axis=0, keepdims=True))
            return lax.fori_loop(0, i, past_pair, m)

        m = lax.cond(bound_is_safe, lambda: {c: bound_row for c in combos}, exact_max)

        def weighted_v(s, n, h):
            return _mm(vt_ref[h, n], jnp.exp(s).astype(BF16))

        def past_pair(j, pv):
            s = {(k, t, h): scores(2 * j + k, t, h, False, m[t, h]) for k in range(2) for t, h in combos}
            return {(t, h): pv[t, h] + weighted_v(s[0, t, h], 2 * j, h) + weighted_v(s[1, t, h], 2 * j + 1, h)
                    for t, h in combos}

        s = {(k, h): scores(n, t, h, own, m[t, h]) for k, (n, t, own) in enumerate(tail) for h in range(2)}
        pv = {(t, h): sum(weighted_v(s[k, h], n, h) for k, (n, tt, _) in enumerate(tail) if tt == t)
              for t, h in combos}
        pv = lax.fori_loop(0, i, past_pair, pv)
        for t in range(2):
            o_t = jnp.concatenate([pv[t, h][:HEAD_DIM] / pv[t, h][HEAD_DIM:HEAD_DIM + 1] for h in range(2)],
                                  axis=0)
            o_ref[0, rows[t], :] = o_t.T
        return 0

    lax.fori_loop(0, nb // 2, two_q_tiles, 0)


def _moba(pb, gq, gk):
    bsz, seq, _ = pb.shape
    n_pairs = B_HEADS // 2
    nb = seq // MOBA_BLOCK
    ones_rows = 16
    slab = lambda off: pl.BlockSpec((1, seq, LANES), lambda b, j: (b, 0, off + j))
    gain = pl.BlockSpec((1, LANES), lambda b, j: (0, 0))
    return pl.pallas_call(
        _moba_kernel,
        grid=(bsz, n_pairs),
        in_specs=[slab(0), slab(n_pairs), slab(2 * n_pairs), gain, gain],
        out_specs=pl.BlockSpec((1, seq, LANES), lambda b, j: (b, 0, j)),
        out_shape=jax.ShapeDtypeStruct((bsz, seq, B_WIDTH), F32),
        scratch_shapes=[pltpu.VMEM((seq, LANES), F32), pltpu.VMEM((seq, LANES), BF16),
                        pltpu.VMEM((2, seq, LANES), BF16), pltpu.VMEM((2, nb, LANES), F32),
                        pltpu.VMEM((2, nb, HEAD_DIM + ones_rows, MOBA_BLOCK), BF16),
                        pltpu.VMEM((2, 2, nb, MOBA_BLOCK), F32)],
        compiler_params=_params(2),
        name="moba",
    )(pb, pb, pb, jnp.tile(gq, 2).reshape(1, LANES), jnp.tile(gk, 2).reshape(1, LANES))


def _dilated_kernel(q_ref, k_ref, v_ref, gq_ref, gk_ref, o_ref,
                    qn_ref, kn_ref, vv_ref, qs_ref, kb_ref, vt_ref, og_ref, lse_ref):
    seq = q_ref.shape[1]
    n_blocks = seq // BAND
    ones_rows = vt_ref.shape[2] - HEAD_DIM
    _qk_norm_into(q_ref, qn_ref, gq_ref[...], ATTN_SCALE)
    _qk_norm_into(k_ref, kn_ref, gk_ref[...], 1.0)
    vv_ref[...] = v_ref[0].astype(F32)
    bound = _score_bound(gq_ref[...], gk_ref[...])
    bound_is_safe = bound <= MAX_SAFE_BOUND
    bound_row = jnp.full((1, BAND), bound, F32)

    is_a = _lane_is_a((BAND, LANES))
    key_j = lax.broadcasted_iota(jnp.int32, (2 * BAND, BAND), 0)
    qry_i = lax.broadcasted_iota(jnp.int32, (2 * BAND, BAND), 1)
    band_bias = jnp.where((key_j >= qry_i) & (key_j <= qry_i + BAND), 0.0, NEG)
    band_bias = jnp.concatenate([band_bias, band_bias], axis=1)
    band_bias_less_bound = band_bias - bound
    ones = jnp.ones((ones_rows, BAND), BF16)
    vt_rows = HEAD_DIM + ones_rows

    for g, dil in enumerate(DILATIONS):
        per_residue = n_blocks // dil
        stride = None if dil == 1 else dil

        def token_rows(gb, dil=dil, per_residue=per_residue, stride=stride):
            r, c = gb // per_residue, gb % per_residue
            return pl.ds(r + c * (BAND * dil), BAND, stride=stride)

        def gather(i, _, token_rows=token_rows):
            for gb in [DIL_GROUP * i + j for j in range(DIL_GROUP)]:
                rows = token_rows(gb)
                dst = pl.ds(pl.multiple_of(gb * BAND, BAND), BAND)
                q = qn_ref[rows, :]
                qs_ref[0, dst, :] = jnp.where(is_a, q, 0.0).astype(BF16)
                qs_ref[1, dst, :] = jnp.where(is_a, 0.0, q).astype(BF16)
                kb_ref[dst, :] = kn_ref[rows, :].astype(BF16)
                vt = vv_ref[rows, :].T.astype(BF16)
                for h in range(2):
                    vt_ref[h, gb, :HEAD_DIM, :] = vt[h * HEAD_DIM:(h + 1) * HEAD_DIM]
                    vt_ref[h, gb, HEAD_DIM:, :] = ones
            return 0

        lax.fori_loop(0, n_blocks // DIL_GROUP, gather, 0)

        def band_block(gb, use_bound, per_residue=per_residue, token_rows=token_rows, g=g):
            prev = jnp.maximum(gb - 1, 0)
            first = jnp.where(gb % per_residue == 0, NEG, 0.0)
            kb = jnp.concatenate([kb_ref[pl.ds(pl.multiple_of(prev * BAND, BAND), BAND), :],
                                  kb_ref[pl.ds(pl.multiple_of(gb * BAND, BAND), BAND), :]], axis=0)
            q_rows = pl.ds(pl.multiple_of(gb * BAND, BAND), BAND)

            def scores(bias):
                q_both = jnp.concatenate([qs_ref[0, q_rows, :], qs_ref[1, q_rows, :]], axis=0)
                s = _mm(kb, q_both, _NT) + bias
                return jnp.concatenate([s[:BAND] + first, s[BAND:]], axis=0)

            if use_bound:
                m = jnp.concatenate([bound_row, bound_row], axis=1)
                s = scores(band_bias_less_bound)
                yield
                p = jnp.exp(s)
            else:
                m = jnp.max(scores(band_bias), axis=0, keepdims=True)
                yield
                s = scores(band_bias)
                yield
                p = jnp.exp(s - m)
            vt_both = jnp.concatenate([jnp.concatenate([vt_ref[h, prev], vt_ref[h, gb]], axis=1)
                                       for h in range(2)], axis=0)
            pv = _mm(vt_both, p.astype(BF16))
            yield
            pv = [pv[h * vt_rows:(h + 1) * vt_rows, h * BAND:(h + 1) * BAND] for h in range(2)]
            l = [x[HEAD_DIM:HEAD_DIM + 1] for x in pv]
            o_t = jnp.concatenate([pv[h][:HEAD_DIM] / l[h] for h in range(2)], axis=0)
            lse_t = jnp.concatenate([jnp.broadcast_to(m[:, h * BAND:(h + 1) * BAND] + jnp.log(l[h]),
                                                      (HEAD_DIM, BAND)) for h in range(2)], axis=0)
            og_ref[g, token_rows(gb), :] = o_t.T
            lse_ref[g, token_rows(gb), :] = lse_t.T

        def group(i, _, use_bound, band_block=band_block):
            _round_robin([band_block(DIL_GROUP * i + j, use_bound) for j in range(DIL_GROUP)])
            return 0

        for use_bound in (True, False):
            @pl.when(bound_is_safe == use_bound)
            def _(use_bound=use_bound, group=group):
                lax.fori_loop(0, n_blocks // DIL_GROUP, functools.partial(group, use_bound=use_bound), 0)

    tile = 512

    def mix(i, _):
        rows = pl.ds(pl.multiple_of(i * tile, tile), tile)
        lse = [lse_ref[g, rows, :] for g in range(len(DILATIONS))]
        top = functools.reduce(jnp.maximum, lse)
        w = [jnp.exp(x - top) for x in lse]
        o_ref[0, rows, :] = sum(w[g] * og_ref[g, rows, :] for g in range(len(DILATIONS))) / sum(w)
        return 0

    lax.fori_loop(0, seq // tile, mix, 0)


def _dilated(pc, gq, gk):
    bsz, seq, _ = pc.shape
    n_pairs = C_HEADS // 2
    n_pat = len(DILATIONS)
    ones_rows = 16
    slab = lambda off: pl.BlockSpec((1, seq, LANES), lambda b, j: (b, 0, off + j))
    gain = pl.BlockSpec((1, LANES), lambda b, j: (0, 0))
    return pl.pallas_call(
        _dilated_kernel,
        grid=(bsz, n_pairs),
        in_specs=[slab(0), slab(n_pairs), slab(2 * n_pairs), gain, gain],
        out_specs=pl.BlockSpec((1, seq, LANES), lambda b, j: (b, 0, j)),
        out_shape=jax.ShapeDtypeStruct((bsz, seq, C_WIDTH), F32),
        scratch_shapes=[pltpu.VMEM((seq, LANES), F32), pltpu.VMEM((seq, LANES), F32),
                        pltpu.VMEM((seq, LANES), F32),
                        pltpu.VMEM((2, seq, LANES), BF16), pltpu.VMEM((seq, LANES), BF16),
                        pltpu.VMEM((2, seq // BAND, HEAD_DIM + ones_rows, BAND), BF16),
                        pltpu.VMEM((n_pat, seq, LANES), F32), pltpu.VMEM((n_pat, seq, LANES), F32)],
        compiler_params=_params(2),
        name="dilated",
    )(pc, pc, pc, jnp.tile(gq, 2).reshape(1, LANES), jnp.tile(gk, 2).reshape(1, LANES))


def kernel(x, norm_g, w_in, w_out, tshift_mu, decay_w0, decay_up, iclr_a0, iclr_up,
           k_k, k_a, r_k, lnx_g, lnx_b, moba_q_g, moba_k_g, dil_q_g, dil_k_g):
    bsz, seq, d_model = x.shape
    depth = norm_g.shape[0]
    assert seq % (max(DILATIONS) * BAND) == 0 and seq % MOBA_BLOCK == 0 and seq % RWKV_CHUNK == 0
    x2d = x.reshape(bsz * seq, d_model)
    w_in_bf = w_in.astype(BF16)
    w_out_bf = w_out.astype(BF16)
    pa, pb, pc, gate = _proj(x2d, in_args=(norm_g[0], w_in_bf[0]))
    for l in range(depth):
        pp, wz = _rwkv_pack_params(decay_w0[l], decay_up[l], iclr_a0[l], iclr_up[l],
                                   k_k[l], k_a[l], r_k[l], lnx_g[l], lnx_b[l])
        ya = _rwkv(pa.reshape(bsz, seq, A_PROJ), tshift_mu[l], pp, wz)
        yb = _moba(pb.reshape(bsz, seq, B_PROJ), moba_q_g[l], moba_k_g[l])
        yc = _dilated(pc.reshape(bsz, seq, C_PROJ), dil_q_g[l], dil_k_g[l])
        out_args = (ya.reshape(bsz * seq, A_WIDTH), yb.reshape(bsz * seq, B_WIDTH),
                    yc.reshape(bsz * seq, C_WIDTH), gate, w_out_bf[l])
        if l + 1 < depth:
            x2d, pa, pb, pc, gate = _proj(x2d, out_args, (norm_g[l + 1], w_in_bf[l + 1]))
        else:
            (x2d,) = _proj(x2d, out_args)
    return x2d.reshape(bsz, seq, d_model)
```

```python
import functools

import jax
import jax.numpy as jnp
from jax import lax
from jax.experimental import pallas as pl
from jax.experimental.pallas import tpu as pltpu

F32 = jnp.float32
BF16 = jnp.bfloat16

HEAD_DIM = 64
LANES = 128
A_HEADS, B_HEADS, C_HEADS = 6, 4, 6
A_WIDTH, B_WIDTH, C_WIDTH = A_HEADS * HEAD_DIM, B_HEADS * HEAD_DIM, C_HEADS * HEAD_DIM
LORA = 64
A_PROJ = 3 * A_WIDTH + 2 * LORA
B_PROJ = 3 * B_WIDTH
C_PROJ = 3 * C_WIDTH
MIX_WIDTH = A_WIDTH + B_WIDTH + C_WIDTH
PROJ_WIDTH = A_PROJ + B_PROJ + C_PROJ + MIX_WIDTH
MOBA_BLOCK = 256
MOBA_TOPK = 3
DILATIONS = (1, 4, 16)
DIL_RATIO = 4
BAND = 128
DIL_GROUP = 16
RMS_EPS = 1e-6
LNX_EPS = HEAD_DIM * 1e-5
ATTN_SCALE = HEAD_DIM ** -0.5
RWKV_CHUNK = 64
NEG = -1e30
MAX_SAFE_BOUND = 40.0
VMEM_LIMIT = 56 * 1024 * 1024


def _params(n_axes):
    return pltpu.CompilerParams(dimension_semantics=("arbitrary",) * n_axes,
                                vmem_limit_bytes=VMEM_LIMIT)


_NN = (((1,), (0,)), ((), ()))
_NT = (((1,), (1,)), ((), ()))
_TN = (((0,), (0,)), ((), ()))


def _mm(a, b, dims=_NN):
    return lax.dot_general(a, b, dims, preferred_element_type=F32)


def _split(x):
    hi = x.astype(BF16)
    lo = (x - hi.astype(F32)).astype(BF16)
    return hi, lo


def _mm3(a, b, dims=_NN):
    ah, al = _split(a)
    bh, bl = _split(b)
    return _mm(ah, bh, dims) + (_mm(ah, bl, dims) + _mm(al, bh, dims))


def _mm1(a, b, dims=_NN):
    return _mm(a.astype(BF16), b.astype(BF16), dims)


def _half_sum(x, is_a):
    sa = jnp.sum(jnp.where(is_a, x, 0.0), axis=-1, keepdims=True)
    sb = jnp.sum(jnp.where(is_a, 0.0, x), axis=-1, keepdims=True)
    return jnp.where(is_a, sa, sb)


def _half_sum_mxu(x):
    row = lax.broadcasted_iota(jnp.int32, (LANES, LANES), 0) // HEAD_DIM
    col = lax.broadcasted_iota(jnp.int32, (LANES, LANES), 1) // HEAD_DIM
    ones_bd = (row == col).astype(BF16)
    hi, lo = _split(x)
    return _mm(hi, ones_bd) + _mm(lo, ones_bd)


def _round_robin(chains):
    results = [None] * len(chains)
    live = list(range(len(chains)))
    while live:
        for i in list(live):
            try:
                next(chains[i])
            except StopIteration as done:
                results[i] = done.value
                live.remove(i)
    return results


def _lane_is_a(shape):
    return lax.broadcasted_iota(jnp.int32, shape, len(shape) - 1) < HEAD_DIM


_PROJ_WIDTHS = (A_PROJ, B_PROJ, C_PROJ, MIX_WIDTH)
_PROJ_DTYPES = (F32, BF16, BF16, BF16)


def _proj_kernel(*refs, has_out, has_in):
    refs = list(refs)
    x = refs.pop(0)[...]
    if has_out:
        ya_ref, yb_ref, yc_ref, gt_ref, w_out_ref = (refs.pop(0) for _ in range(5))
    if has_in:
        g_ref, w_in_ref = refs.pop(0), refs.pop(0)
    if has_out:
        lo = 0
        for y_ref in (ya_ref, yb_ref, yc_ref):
            hi = lo + y_ref.shape[-1]
            g = gt_ref[:, lo:hi].astype(F32)
            y = y_ref[...] * (g * jax.nn.sigmoid(g))
            x = x + _mm(y.astype(BF16), w_out_ref[lo:hi, :])
            lo = hi
        refs.pop(0)[...] = x
    if has_in:
        ms = jnp.mean(x * x, axis=-1, keepdims=True)
        h = (x * lax.rsqrt(ms + RMS_EPS) * g_ref[...]).astype(BF16)
        lo = 0
        for ref in refs:
            hi = lo + ref.shape[-1]
            ref[...] = _mm(h, w_in_ref[:, lo:hi]).astype(ref.dtype)
            lo = hi


def _proj(x2d, out_args=None, in_args=None, tm=512):
    m, d = x2d.shape
    row = lambda w: pl.BlockSpec((tm, w), lambda i: (i, 0))
    whole = lambda shape: pl.BlockSpec(shape, lambda i: (0, 0))
    args, in_specs, out_specs, out_shape = [x2d], [row(d)], [], []
    if out_args is not None:
        args += list(out_args)
        in_specs += [row(A_WIDTH), row(B_WIDTH), row(C_WIDTH), row(MIX_WIDTH), whole((MIX_WIDTH, d))]
        out_specs.append(row(d))
        out_shape.append(jax.ShapeDtypeStruct((m, d), F32))
    if in_args is not None:
        gain, w_in = in_args
        args += [gain.reshape(1, d), w_in]
        in_specs += [whole((1, d)), whole((d, PROJ_WIDTH))]
        out_specs += [row(w) for w in _PROJ_WIDTHS]
        out_shape += [jax.ShapeDtypeStruct((m, w), dt) for w, dt in zip(_PROJ_WIDTHS, _PROJ_DTYPES)]
    return pl.pallas_call(
        functools.partial(_proj_kernel, has_out=out_args is not None, has_in=in_args is not None),
        grid=(m // tm,),
        in_specs=in_specs, out_specs=out_specs, out_shape=out_shape,
        compiler_params=_params(1),
        name="proj",
    )(*args)


_P_W0, _P_A0, _P_KK, _P_KA, _P_RK, _P_LNG, _P_LNB = range(7)
_P_ROWS = 8


def _rwkv_kernel(pa_ref, mu_ref, pp_ref, wz_ref, o_ref, st_ref, prev_ref):
    n_rows, tile = pa_ref.shape[0], pa_ref.shape[1]
    c_len = RWKV_CHUNK
    two_c = 2 * c_len
    n_chunks = tile // c_len
    n_pairs = A_HEADS // 2
    is_a = _lane_is_a((c_len, LANES))

    row_c = lax.broadcasted_iota(jnp.int32, (c_len, A_PROJ), 0)
    ri = lax.broadcasted_iota(jnp.int32, (c_len, c_len), 0)
    ci = lax.broadcasted_iota(jnp.int32, (c_len, c_len), 1)
    tril_c = (ri >= ci).astype(BF16)
    r2 = lax.broadcasted_iota(jnp.int32, (c_len, two_c), 0)
    c2 = lax.broadcasted_iota(jnp.int32, (c_len, two_c), 1) % c_len
    m_strict = r2 > c2
    m_incl = r2 >= c2
    eye2 = (r2 == c2).astype(F32)

    def stack(x):
        return jnp.concatenate([jnp.where(is_a, x, 0.0), jnp.where(is_a, 0.0, x)], axis=0)

    @pl.when(pl.program_id(1) == 0)
    def _():
        st_ref[...] = jnp.zeros_like(st_ref)
        prev_ref[...] = jnp.zeros_like(prev_ref)

    def pair_chunk(b, j, r, k, v, lora_w, lora_a):
        pp = pp_ref[j]
        prow = lambda i: pp[i:i + 1, :]
        w0, a0, k_k, k_a, r_k = prow(_P_W0), prow(_P_A0), prow(_P_KK), prow(_P_KA), prow(_P_RK)
        ln_g, ln_b = prow(_P_LNG), prow(_P_LNB)
        w = -jax.nn.softplus(-(w0 + lora_w)) - 0.5
        lw = -jnp.exp(w)
        a = jax.nn.sigmoid(a0 + lora_a)
        kk = k * k_k
        kk = kk * lax.rsqrt(_half_sum(kk * kk, is_a) + 1e-12)
        k2 = k * (1.0 + (a - 1.0) * k_a)
        kka = kk * a

        l1 = lw.astype(BF16)
        rem = lw - l1.astype(F32)
        l2 = rem.astype(BF16)
        l3 = (rem - l2.astype(F32)).astype(BF16)
        g = _mm(tril_c, l1) + (_mm(tril_c, l2) + _mm(tril_c, l3))
        yield
        g_end = g[c_len - 1:c_len, :]
        e_pos = jnp.exp(g)
        e_neg = jnp.exp(-g)
        e_prev = jnp.exp(g - lw)
        e_tail = jnp.exp(g_end - g)

        ab = -kk * e_prev
        rb = r * e_pos
        ab2 = stack(ab)
        bt2 = stack(kka * e_neg)
        kt2 = stack(k2 * e_neg)
        bp2 = stack(kka * e_tail)
        kp2 = stack(k2 * e_tail)
        v2 = stack(v)

        mm = _mm1(jnp.concatenate([ab, rb], axis=0), jnp.concatenate([bt2, kt2], axis=0), _NT)
        yield
        l_b = jnp.where(m_strict, mm[:c_len, :two_c], 0.0)
        l_k = jnp.where(m_strict, mm[:c_len, two_c:], 0.0)
        r_b = jnp.where(m_incl, mm[c_len:, :two_c], 0.0)
        r_k2 = jnp.where(m_incl, mm[c_len:, two_c:], 0.0)

        t_inv = eye2 + l_b
        p = _mm1(l_b, stack(l_b))
        kv = _mm1(l_k, v2)
        yield
        steps = c_len.bit_length() - 2
        for i in range(steps):
            if i + 1 < steps:
                tp = _mm1(jnp.concatenate([t_inv, p], axis=0), stack(p))
                yield
                t_inv = t_inv + tp[:c_len]
                p = tp[c_len:]
            else:
                tp = _mm1(t_inv, stack(p))
                yield
                t_inv = t_inv + tp

        tw = _mm1(t_inv, jnp.concatenate([ab2, stack(kv)], axis=1))
        yield
        sv = st_ref[b, j]
        ws = _mm1(jnp.concatenate([tw[:, :LANES], rb], axis=0), sv, _NT)
        yield
        uv = jnp.concatenate([stack(ws[:c_len] + tw[:, LANES:]), v2], axis=0)
        y = ws[c_len:] + _mm1(jnp.concatenate([r_b, r_k2], axis=1), uv)
        st_ref[b, j] = sv * jnp.exp(g_end) + _mm1(uv, jnp.concatenate([bp2, kp2], axis=0), _TN)
        yield

        mean = _half_sum(y, is_a) * (1.0 / HEAD_DIM)
        yc = y - mean
        var = _half_sum(yc * yc, is_a) * (1.0 / HEAD_DIM)
        y = yc * lax.rsqrt(var + LNX_EPS) * ln_g + ln_b
        return y + _half_sum(r * k2 * r_k, is_a) * v

    mu = mu_ref[...]
    wz = wz_ref[...]

    def body(c, prev_rows):
        t0 = pl.multiple_of(c * c_len, c_len)
        chains, last_rows = [], []
        for b in range(n_rows):
            x = pa_ref[b, pl.ds(t0, c_len), :]
            prev = jnp.where(row_c == 0, prev_rows[b], pltpu.roll(x, 1, axis=0))
            xs = x + (prev - x) * mu
            slab = lambda i, xs=xs: xs[:, i * LANES:(i + 1) * LANES]
            z = slab(3 * n_pairs)
            lora = _mm3(jnp.where(is_a, jnp.tanh(z), z), wz)
            chains += [pair_chunk(b, j, slab(j), slab(n_pairs + j), slab(2 * n_pairs + j),
                                  lora[:, 2 * j * LANES:(2 * j + 1) * LANES],
                                  lora[:, (2 * j + 1) * LANES:(2 * j + 2) * LANES])
                       for j in range(n_pairs)]
            last_rows.append(x[c_len - 1:c_len, :])
        for i, y in enumerate(_round_robin(chains)):
            b, j = divmod(i, n_pairs)
            o_ref[b, pl.ds(t0, c_len), j * LANES:(j + 1) * LANES] = y
        return tuple(last_rows)

    last = lax.fori_loop(0, n_chunks, body, tuple(prev_ref[b, 0:1, :] for b in range(n_rows)))
    for b in range(n_rows):
        prev_ref[b, 0:1, :] = last[b]


def _rwkv(pa, mu, pp, wz, tile=512):
    bsz, seq, _ = pa.shape
    rows = next(r for r in (4, 2, 1) if bsz % r == 0)
    n_pairs = A_HEADS // 2
    whole = lambda shape: pl.BlockSpec(shape, lambda b, s: (0,) * len(shape))
    return pl.pallas_call(
        _rwkv_kernel,
        grid=(bsz // rows, seq // tile),
        in_specs=[pl.BlockSpec((rows, tile, A_PROJ), lambda b, s: (b, s, 0)),
                  whole((1, A_PROJ)), whole((n_pairs, _P_ROWS, LANES)), whole((LANES, n_pairs * 2 * LANES))],
        out_specs=pl.BlockSpec((rows, tile, A_WIDTH), lambda b, s: (b, s, 0)),
        out_shape=jax.ShapeDtypeStruct((bsz, seq, A_WIDTH), F32),
        scratch_shapes=[pltpu.VMEM((rows, n_pairs, LANES, LANES), F32), pltpu.VMEM((rows, 8, A_PROJ), F32)],
        compiler_params=_params(2),
        name="rwkv",
    )(pa, mu.reshape(1, A_PROJ), pp, wz)


def _rwkv_pack_params(w0, w_up, a0, a_up, k_k, k_a, r_k, ln_g, ln_b):
    n_pairs = A_HEADS // 2
    pair = lambda t, j: t[j * LANES:(j + 1) * LANES]
    pps, wzs = [], []
    zeros = jnp.zeros((LORA, LANES), F32)
    for j in range(n_pairs):
        rows = {_P_W0: w0, _P_A0: a0, _P_KK: k_k, _P_KA: k_a, _P_RK: r_k.reshape(-1), _P_LNG: ln_g, _P_LNB: ln_b}
        pps.append(jnp.stack([pair(rows[i], j) if i in rows else jnp.zeros((LANES,), F32)
                              for i in range(_P_ROWS)]))
        wzs.append(jnp.concatenate([w_up[:, j * LANES:(j + 1) * LANES], zeros], axis=0))
        wzs.append(jnp.concatenate([zeros, a_up[:, j * LANES:(j + 1) * LANES]], axis=0))
    return jnp.stack(pps), jnp.concatenate(wzs, axis=1)


def _qk_norm_into(src_ref, dst_ref, gain, scale, tile=512):
    seq = dst_ref.shape[0]

    def body(i, _):
        t0 = pl.multiple_of(i * tile, tile)
        x = src_ref[0, pl.ds(t0, tile), :].astype(F32)
        ms = _half_sum_mxu(x * x) * (1.0 / HEAD_DIM)
        dst_ref[pl.ds(t0, tile), :] = x * lax.rsqrt(ms + RMS_EPS) * (gain * scale)
        return 0

    lax.fori_loop(0, seq // tile, body, 0)


def _score_bound(gq, gk):
    return (HEAD_DIM * ATTN_SCALE) * jnp.max(jnp.abs(gq)) * jnp.max(jnp.abs(gk))


def _moba_kernel(q_ref, k_ref, v_ref, gq_ref, gk_ref, o_ref,
                 qn_ref, kb_ref, qs_ref, km_ref, vt_ref, bias_ref):
    seq = q_ref.shape[1]
    blk = MOBA_BLOCK
    nb = seq // blk
    ones_rows = vt_ref.shape[2] - HEAD_DIM
    is_a = _lane_is_a((blk, LANES))
    gq, gk = gq_ref[...], gk_ref[...]

    def prepare(n, _):
        t0 = pl.multiple_of(n * blk, blk)
        rows = pl.ds(t0, blk)
        q = q_ref[0, rows, :].astype(F32)
        qn = q * lax.rsqrt(_half_sum_mxu(q * q) * (1.0 / HEAD_DIM) + RMS_EPS) * gq
        qn_ref[rows, :] = qn
        qs = qn * ATTN_SCALE
        qs_ref[0, rows, :] = jnp.where(is_a, qs, 0.0).astype(BF16)
        qs_ref[1, rows, :] = jnp.where(is_a, 0.0, qs).astype(BF16)
        k = k_ref[0, rows, :].astype(F32)
        kn = k * lax.rsqrt(_half_sum_mxu(k * k) * (1.0 / HEAD_DIM) + RMS_EPS) * gk
        kb_ref[rows, :] = kn.astype(BF16)
        km = jnp.mean(kn, axis=0, keepdims=True)
        km_ref[0, pl.ds(n, 1), :] = jnp.where(is_a[:1], km, 0.0)
        km_ref[1, pl.ds(n, 1), :] = jnp.where(is_a[:1], 0.0, km)
        vt = v_ref[0, rows, :].astype(F32).T.astype(BF16)
        ones = jnp.ones((ones_rows, blk), BF16)
        for h in range(2):
            vt_ref[h, n, :HEAD_DIM, :] = vt[h * HEAD_DIM:(h + 1) * HEAD_DIM]
            vt_ref[h, n, HEAD_DIM:, :] = ones
        return 0

    lax.fori_loop(0, nb, prepare, 0)

    bound = _score_bound(gq, gk)
    bound_is_safe = bound <= MAX_SAFE_BOUND
    bound_row = jnp.full((1, blk), bound, F32)
    blk_row = lax.broadcasted_iota(jnp.int32, (nb, blk), 0)
    key_pos = lax.broadcasted_iota(jnp.int32, (blk, blk), 0)
    qry_pos = lax.broadcasted_iota(jnp.int32, (blk, blk), 1)
    causal = key_pos <= qry_pos

    def two_q_tiles(i, _):
        tiles = (2 * i, 2 * i + 1)
        rows = [pl.ds(pl.multiple_of(qt * blk, blk), blk) for qt in tiles]
        combos = [(t, h) for t in range(2) for h in range(2)]
        for t, h in combos:
            gate = _mm3(km_ref[h], qn_ref[rows[t], :], _NT)
            gate = jnp.where(blk_row < tiles[t], gate, -jnp.inf)
            bias = jnp.full((nb, blk), NEG, F32)
            for _ in range(MOBA_TOPK):
                top = jnp.max(gate, axis=0, keepdims=True)
                hit = (gate == top) & (top > -jnp.inf)
                first = jnp.min(jnp.where(hit, blk_row, nb), axis=0, keepdims=True)
                pick = blk_row == first
                bias = jnp.where(pick, 0.0, bias)
                gate = jnp.where(pick, -jnp.inf, gate)
            bias_ref[t, h] = bias

        qs = {(t, h): qs_ref[h, rows[t], :] for t, h in combos}

        def scores(n, t, h, own, minus=0.0):
            s = _mm(kb_ref[pl.ds(pl.multiple_of(n * blk, blk), blk), :], qs[t, h], _NT)
            if own:
                return jnp.where(causal, s - minus, NEG)
            return s + (bias_ref[t, h, pl.ds(n, 1), :] - minus)

        tail = [(tiles[0], 0, True), (tiles[0], 1, False), (tiles[1], 1, True)]

        def exact_max():
            def past_pair(j, m):
                s = {(k, t, h): scores(2 * j + k, t, h, False) for k in range(2) for t, h in combos}
                return {(t, h): jnp.maximum(m[t, h], jnp.max(jnp.maximum(s[0, t, h], s[1, t, h]),
                                                                 axis=0, keepdims=True)) for t, h in combos}

            m = {(t, h): jnp.full((1, blk), NEG, F32) for t, h in combos}
            for n, t, own in tail:
                for h in range(2):
                    m[t, h] = jnp.maximum(m[t, h], jnp.max(scores(n, t, h, own), axis=0, keepdims=True))
            return lax.fori_loop(0, i, past_pair, m)

        m = lax.cond(bound_is_safe, lambda: {c: bound_row for c in combos}, exact_max)

        def weighted_v(s, n, h):
            return _mm(vt_ref[h, n], jnp.exp(s).astype(BF16))

        def past_pair(j, pv):
            s = {(k, t, h): scores(2 * j + k, t, h, False, m[t, h]) for k in range(2) for t, h in combos}
            return {(t, h): pv[t, h] + weighted_v(s[0, t, h], 2 * j, h) + weighted_v(s[1, t, h], 2 * j + 1, h)
                    for t, h in combos}

        s = {(k, h): scores(n, t, h, own, m[t, h]) for k, (n, t, own) in enumerate(tail) for h in range(2)}
        pv = {(t, h): sum(weighted_v(s[k, h], n, h) for k, (n, tt, _) in enumerate(tail) if tt == t)
              for t, h in combos}
        pv = lax.fori_loop(0, i, past_pair, pv)
        for t in range(2):
            o_t = jnp.concatenate([pv[t, h][:HEAD_DIM] / pv[t, h][HEAD_DIM:HEAD_DIM + 1] for h in range(2)],
                                  axis=0)
            o_ref[0, rows[t], :] = o_t.T
        return 0

    lax.fori_loop(0, nb // 2, two_q_tiles, 0)


def _moba(pb, gq, gk):
    bsz, seq, _ = pb.shape
    n_pairs = B_HEADS // 2
    nb = seq // MOBA_BLOCK
    ones_rows = 16
    slab = lambda off: pl.BlockSpec((1, seq, LANES), lambda b, j: (b, 0, off + j))
    gain = pl.BlockSpec((1, LANES), lambda b, j: (0, 0))
    return pl.pallas_call(
        _moba_kernel,
        grid=(bsz, n_pairs),
        in_specs=[slab(0), slab(n_pairs), slab(2 * n_pairs), gain, gain],
        out_specs=pl.BlockSpec((1, seq, LANES), lambda b, j: (b, 0, j)),
        out_shape=jax.ShapeDtypeStruct((bsz, seq, B_WIDTH), F32),
        scratch_shapes=[pltpu.VMEM((seq, LANES), F32), pltpu.VMEM((seq, LANES), BF16),
                        pltpu.VMEM((2, seq, LANES), BF16), pltpu.VMEM((2, nb, LANES), F32),
                        pltpu.VMEM((2, nb, HEAD_DIM + ones_rows, MOBA_BLOCK), BF16),
                        pltpu.VMEM((2, 2, nb, MOBA_BLOCK), F32)],
        compiler_params=_params(2),
        name="moba",
    )(pb, pb, pb, jnp.tile(gq, 2).reshape(1, LANES), jnp.tile(gk, 2).reshape(1, LANES))


def _dilated_kernel(q_ref, k_ref, v_ref, gq_ref, gk_ref, o_ref,
                    qn_ref, kn_ref, vv_ref, q1_ref, k1_ref, v1_ref, qs_ref, kb_ref, vt_ref, og_ref, lse_ref):
    seq = q_ref.shape[1]
    n_blocks = seq // BAND
    ones_rows = vt_ref.shape[2] - HEAD_DIM
    _qk_norm_into(q_ref, qn_ref, gq_ref[...], ATTN_SCALE)
    _qk_norm_into(k_ref, kn_ref, gk_ref[...], 1.0)
    vv_ref[...] = v_ref[0].astype(F32)
    bound = _score_bound(gq_ref[...], gk_ref[...])
    bound_is_safe = bound <= MAX_SAFE_BOUND
    bound_row = jnp.full((1, BAND), bound, F32)

    is_a = _lane_is_a((BAND, LANES))
    key_j = lax.broadcasted_iota(jnp.int32, (2 * BAND, BAND), 0)
    qry_i = lax.broadcasted_iota(jnp.int32, (2 * BAND, BAND), 1)
    band_bias = jnp.where((key_j >= qry_i) & (key_j <= qry_i + BAND), 0.0, NEG)
    band_bias = jnp.concatenate([band_bias, band_bias], axis=1)
    band_bias_less_bound = band_bias - bound
    ones = jnp.ones((ones_rows, BAND), BF16)
    vt_rows = HEAD_DIM + ones_rows

    def own_rows(gb):
        return pl.ds(pl.multiple_of(gb * BAND, BAND), BAND)

    def rows_in_previous_level(g, gb):
        per_residue = n_blocks // DILATIONS[g]
        segment, c = gb // per_residue, gb % per_residue
        start = (segment // DIL_RATIO) * (seq // DILATIONS[g - 1]) + segment % DIL_RATIO
        return pl.ds(start + c * (BAND * DIL_RATIO), BAND, stride=DIL_RATIO)

    for g, dil in enumerate(DILATIONS):
        per_residue = n_blocks // dil
        sources = (qn_ref, kn_ref, vv_ref) if g <= 1 else (q1_ref, k1_ref, v1_ref)

        def gather(i, _, g=g, sources=sources):
            for gb in [DIL_GROUP * i + j for j in range(DIL_GROUP)]:
                rows = own_rows(gb) if g == 0 else rows_in_previous_level(g, gb)
                dst = own_rows(gb)
                q, k, v = (ref[rows, :] for ref in sources)
                if g == 1:
                    q1_ref[dst, :], k1_ref[dst, :], v1_ref[dst, :] = q, k, v
                qs_ref[0, dst, :] = jnp.where(is_a, q, 0.0).astype(BF16)
                qs_ref[1, dst, :] = jnp.where(is_a, 0.0, q).astype(BF16)
                kb_ref[dst, :] = k.astype(BF16)
                vt = v.T.astype(BF16)
                for h in range(2):
                    vt_ref[h, gb, :HEAD_DIM, :] = vt[h * HEAD_DIM:(h + 1) * HEAD_DIM]
                    vt_ref[h, gb, HEAD_DIM:, :] = ones
            return 0

        lax.fori_loop(0, n_blocks // DIL_GROUP, gather, 0)

        def band_block(gb, use_bound, per_residue=per_residue, g=g):
            prev = jnp.maximum(gb - 1, 0)
            first = jnp.where(gb % per_residue == 0, NEG, 0.0)
            kb = jnp.concatenate([kb_ref[own_rows(prev), :], kb_ref[own_rows(gb), :]], axis=0)

            def scores(bias):
                q_both = jnp.concatenate([qs_ref[0, own_rows(gb), :], qs_ref[1, own_rows(gb), :]], axis=0)
                s = _mm(kb, q_both, _NT) + bias
                return jnp.concatenate([s[:BAND] + first, s[BAND:]], axis=0)

            if use_bound:
                m = jnp.concatenate([bound_row, bound_row], axis=1)
                s = scores(band_bias_less_bound)
                yield
                p = jnp.exp(s)
            else:
                m = jnp.max(scores(band_bias), axis=0, keepdims=True)
                yield
                s = scores(band_bias)
                yield
                p = jnp.exp(s - m)
            vt_both = jnp.concatenate([jnp.concatenate([vt_ref[h, prev], vt_ref[h, gb]], axis=1)
                                       for h in range(2)], axis=0)
            pv = _mm(vt_both, p.astype(BF16))
            yield
            pv = [pv[h * vt_rows:(h + 1) * vt_rows, h * BAND:(h + 1) * BAND] for h in range(2)]
            l = [x[HEAD_DIM:HEAD_DIM + 1] for x in pv]
            o_t = jnp.concatenate([pv[h][:HEAD_DIM] / l[h] for h in range(2)], axis=0)
            lse_t = jnp.concatenate([jnp.broadcast_to(m[:, h * BAND:(h + 1) * BAND] + jnp.log(l[h]),
                                                      (HEAD_DIM, BAND)) for h in range(2)], axis=0)
            out_rows = own_rows(gb) if g <= 1 else rows_in_previous_level(g, gb)
            og_ref[g, out_rows, :] = o_t.T
            lse_ref[g, out_rows, :] = lse_t.T

        def group(i, _, use_bound, band_block=band_block):
            _round_robin([band_block(DIL_GROUP * i + j, use_bound) for j in range(DIL_GROUP)])
            return 0

        for use_bound in (True, False):
            @pl.when(bound_is_safe == use_bound)
            def _(use_bound=use_bound, group=group):
                lax.fori_loop(0, n_blocks // DIL_GROUP, functools.partial(group, use_bound=use_bound), 0)

    tile = 512
    segment_len = seq // DILATIONS[1]

    def mix(i, _):
        rows = pl.ds(pl.multiple_of(i * tile, tile), tile)
        start = i * tile
        tokens = pl.ds(start // segment_len + DIL_RATIO * (start % segment_len), tile, stride=DIL_RATIO)
        where = (tokens, rows, rows)
        lse = [lse_ref[g, where[g], :] for g in range(len(DILATIONS))]
        top = functools.reduce(jnp.maximum, lse)
        w = [jnp.exp(x - top) for x in lse]
        o_ref[0, tokens, :] = sum(w[g] * og_ref[g, where[g], :] for g in range(len(DILATIONS))) / sum(w)
        return 0

    lax.fori_loop(0, seq // tile, mix, 0)


def _dilated(pc, gq, gk):
    bsz, seq, _ = pc.shape
    n_pairs = C_HEADS // 2
    n_pat = len(DILATIONS)
    ones_rows = 16
    slab = lambda off: pl.BlockSpec((1, seq, LANES), lambda b, j: (b, 0, off + j))
    gain = pl.BlockSpec((1, LANES), lambda b, j: (0, 0))
    return pl.pallas_call(
        _dilated_kernel,
        grid=(bsz, n_pairs),
        in_specs=[slab(0), slab(n_pairs), slab(2 * n_pairs), gain, gain],
        out_specs=pl.BlockSpec((1, seq, LANES), lambda b, j: (b, 0, j)),
        out_shape=jax.ShapeDtypeStruct((bsz, seq, C_WIDTH), F32),
        scratch_shapes=[pltpu.VMEM((seq, LANES), F32)] * 6 + [
                        pltpu.VMEM((2, seq, LANES), BF16), pltpu.VMEM((seq, LANES), BF16),
                        pltpu.VMEM((2, seq // BAND, HEAD_DIM + ones_rows, BAND), BF16),
                        pltpu.VMEM((n_pat, seq, LANES), F32), pltpu.VMEM((n_pat, seq, LANES), F32)],
        compiler_params=_params(2),
        name="dilated",
    )(pc, pc, pc, jnp.tile(gq, 2).reshape(1, LANES), jnp.tile(gk, 2).reshape(1, LANES))


def kernel(x, norm_g, w_in, w_out, tshift_mu, decay_w0, decay_up, iclr_a0, iclr_up,
           k_k, k_a, r_k, lnx_g, lnx_b, moba_q_g, moba_k_g, dil_q_g, dil_k_g):
    bsz, seq, d_model = x.shape
    depth = norm_g.shape[0]
    assert seq % (max(DILATIONS) * BAND) == 0 and seq % MOBA_BLOCK == 0 and seq % RWKV_CHUNK == 0
    x2d = x.reshape(bsz * seq, d_model)
    w_in_bf = w_in.astype(BF16)
    w_out_bf = w_out.astype(BF16)
    pa, pb, pc, gate = _proj(x2d, in_args=(norm_g[0], w_in_bf[0]))
    for l in range(depth):
        pp, wz = _rwkv_pack_params(decay_w0[l], decay_up[l], iclr_a0[l], iclr_up[l],
                                   k_k[l], k_a[l], r_k[l], lnx_g[l], lnx_b[l])
        ya = _rwkv(pa.reshape(bsz, seq, A_PROJ), tshift_mu[l], pp, wz)
        yb = _moba(pb.reshape(bsz, seq, B_PROJ), moba_q_g[l], moba_k_g[l])
        yc = _dilated(pc.reshape(bsz, seq, C_PROJ), dil_q_g[l], dil_k_g[l])
        out_args = (ya.reshape(bsz * seq, A_WIDTH), yb.reshape(bsz * seq, B_WIDTH),
                    yc.reshape(bsz * seq, C_WIDTH), gate, w_out_bf[l])
        if l + 1 < depth:
            x2d, pa, pb, pc, gate = _proj(x2d, out_args, (norm_g[l + 1], w_in_bf[l + 1]))
        else:
            (x2d,) = _proj(x2d, out_args)
    return x2d.reshape(bsz, seq, d_model)
```

```python
import functools

import jax
import jax.numpy as jnp
from jax import lax
from jax.experimental import pallas as pl
from jax.experimental.pallas import tpu as pltpu

F32 = jnp.float32
BF16 = jnp.bfloat16

HEAD_DIM = 64
LANES = 128
A_HEADS, B_HEADS, C_HEADS = 6, 4, 6
A_WIDTH, B_WIDTH, C_WIDTH = A_HEADS * HEAD_DIM, B_HEADS * HEAD_DIM, C_HEADS * HEAD_DIM
LORA = 64
A_PROJ = 3 * A_WIDTH + 2 * LORA
B_PROJ = 3 * B_WIDTH
C_PROJ = 3 * C_WIDTH
MIX_WIDTH = A_WIDTH + B_WIDTH + C_WIDTH
PROJ_WIDTH = A_PROJ + B_PROJ + C_PROJ + MIX_WIDTH
MOBA_BLOCK = 256
MOBA_TOPK = 3
DILATIONS = (1, 4, 16)
DIL_RATIO = 4
BAND = 128
DIL_GROUP = 16
RMS_EPS = 1e-6
LNX_EPS = HEAD_DIM * 1e-5
ATTN_SCALE = HEAD_DIM ** -0.5
RWKV_CHUNK = 64
NEG = -1e30
MAX_SAFE_BOUND = 40.0
VMEM_LIMIT = 56 * 1024 * 1024


def _params(n_axes):
    return pltpu.CompilerParams(dimension_semantics=("arbitrary",) * n_axes,
                                vmem_limit_bytes=VMEM_LIMIT)


_NN = (((1,), (0,)), ((), ()))
_NT = (((1,), (1,)), ((), ()))
_TN = (((0,), (0,)), ((), ()))


def _mm(a, b, dims=_NN):
    return lax.dot_general(a, b, dims, preferred_element_type=F32)


def _split(x):
    hi = x.astype(BF16)
    lo = (x - hi.astype(F32)).astype(BF16)
    return hi, lo


def _mm3(a, b, dims=_NN):
    ah, al = _split(a)
    bh, bl = _split(b)
    return _mm(ah, bh, dims) + (_mm(ah, bl, dims) + _mm(al, bh, dims))


def _mm1(a, b, dims=_NN):
    return _mm(a.astype(BF16), b.astype(BF16), dims)


def _half_sum(x, is_a):
    sa = jnp.sum(jnp.where(is_a, x, 0.0), axis=-1, keepdims=True)
    sb = jnp.sum(jnp.where(is_a, 0.0, x), axis=-1, keepdims=True)
    return jnp.where(is_a, sa, sb)


def _half_sum_mxu(x):
    row = lax.broadcasted_iota(jnp.int32, (LANES, LANES), 0) // HEAD_DIM
    col = lax.broadcasted_iota(jnp.int32, (LANES, LANES), 1) // HEAD_DIM
    ones_bd = (row == col).astype(BF16)
    hi, lo = _split(x)
    return _mm(hi, ones_bd) + _mm(lo, ones_bd)


def _round_robin(chains):
    results = [None] * len(chains)
    live = list(range(len(chains)))
    while live:
        for i in list(live):
            try:
                next(chains[i])
            except StopIteration as done:
                results[i] = done.value
                live.remove(i)
    return results


def _lane_is_a(shape):
    return lax.broadcasted_iota(jnp.int32, shape, len(shape) - 1) < HEAD_DIM


_PROJ_WIDTHS = (A_PROJ, B_PROJ, C_PROJ, MIX_WIDTH)
_PROJ_DTYPES = (F32, BF16, BF16, BF16)


def _proj_kernel(*refs, has_out, has_in):
    refs = list(refs)
    x = refs.pop(0)[...]
    if has_out:
        ya_ref, yb_ref, yc_ref, gt_ref, w_out_ref = (refs.pop(0) for _ in range(5))
    if has_in:
        g_ref, w_in_ref = refs.pop(0), refs.pop(0)
    if has_out:
        gated = []
        lo = 0
        for y_ref in (ya_ref, yb_ref, yc_ref):
            hi = lo + y_ref.shape[-1]
            g = gt_ref[:, lo:hi].astype(F32)
            gated.append((y_ref[...] * (g * jax.nn.sigmoid(g))).astype(BF16))
            lo = hi
        x = x + _mm(jnp.concatenate(gated, axis=1), w_out_ref[...])
        refs.pop(0)[...] = x
    if has_in:
        ms = jnp.mean(x * x, axis=-1, keepdims=True)
        h = (x * lax.rsqrt(ms + RMS_EPS) * g_ref[...]).astype(BF16)
        lo = 0
        for ref in refs:
            hi = lo + ref.shape[-1]
            ref[...] = _mm(h, w_in_ref[:, lo:hi]).astype(ref.dtype)
            lo = hi


def _proj(x2d, out_args=None, in_args=None, tm=512):
    m, d = x2d.shape
    row = lambda w: pl.BlockSpec((tm, w), lambda i: (i, 0))
    whole = lambda shape: pl.BlockSpec(shape, lambda i: (0, 0))
    args, in_specs, out_specs, out_shape = [x2d], [row(d)], [], []
    if out_args is not None:
        args += list(out_args)
        in_specs += [row(A_WIDTH), row(B_WIDTH), row(C_WIDTH), row(MIX_WIDTH), whole((MIX_WIDTH, d))]
        out_specs.append(row(d))
        out_shape.append(jax.ShapeDtypeStruct((m, d), F32))
    if in_args is not None:
        gain, w_in = in_args
        args += [gain.reshape(1, d), w_in]
        in_specs += [whole((1, d)), whole((d, PROJ_WIDTH))]
        out_specs += [row(w) for w in _PROJ_WIDTHS]
        out_shape += [jax.ShapeDtypeStruct((m, w), dt) for w, dt in zip(_PROJ_WIDTHS, _PROJ_DTYPES)]
    return pl.pallas_call(
        functools.partial(_proj_kernel, has_out=out_args is not None, has_in=in_args is not None),
        grid=(m // tm,),
        in_specs=in_specs, out_specs=out_specs, out_shape=out_shape,
        compiler_params=_params(1),
        name="proj",
    )(*args)


_P_W0, _P_A0, _P_KK, _P_KA, _P_RK, _P_LNG, _P_LNB = range(7)
_P_ROWS = 8


def _rwkv_kernel(pa_ref, mu_ref, pp_ref, wz_ref, o_ref, st_ref, prev_ref):
    n_rows, tile = pa_ref.shape[0], pa_ref.shape[1]
    c_len = RWKV_CHUNK
    two_c = 2 * c_len
    n_chunks = tile // c_len
    n_pairs = A_HEADS // 2
    is_a = _lane_is_a((c_len, LANES))

    row_c = lax.broadcasted_iota(jnp.int32, (c_len, A_PROJ), 0)
    ri = lax.broadcasted_iota(jnp.int32, (c_len, c_len), 0)
    ci = lax.broadcasted_iota(jnp.int32, (c_len, c_len), 1)
    tril_c = (ri >= ci).astype(BF16)
    r2 = lax.broadcasted_iota(jnp.int32, (c_len, two_c), 0)
    c2 = lax.broadcasted_iota(jnp.int32, (c_len, two_c), 1) % c_len
    m_strict = r2 > c2
    m_incl = r2 >= c2
    eye2 = (r2 == c2).astype(F32)

    def stack(x):
        return jnp.concatenate([jnp.where(is_a, x, 0.0), jnp.where(is_a, 0.0, x)], axis=0)

    @pl.when(pl.program_id(1) == 0)
    def _():
        st_ref[...] = jnp.zeros_like(st_ref)
        prev_ref[...] = jnp.zeros_like(prev_ref)

    def pair_chunk(b, j, r, k, v, lora_w, lora_a):
        pp = pp_ref[j]
        prow = lambda i: pp[i:i + 1, :]
        w0, a0, k_k, k_a, r_k = prow(_P_W0), prow(_P_A0), prow(_P_KK), prow(_P_KA), prow(_P_RK)
        ln_g, ln_b = prow(_P_LNG), prow(_P_LNB)
        w = -jax.nn.softplus(-(w0 + lora_w)) - 0.5
        lw = -jnp.exp(w)
        a = jax.nn.sigmoid(a0 + lora_a)
        kk = k * k_k
        kk = kk * lax.rsqrt(_half_sum(kk * kk, is_a) + 1e-12)
        k2 = k * (1.0 + (a - 1.0) * k_a)
        kka = kk * a

        l1 = lw.astype(BF16)
        rem = lw - l1.astype(F32)
        l2 = rem.astype(BF16)
        l3 = (rem - l2.astype(F32)).astype(BF16)
        g = _mm(tril_c, l1) + (_mm(tril_c, l2) + _mm(tril_c, l3))
        yield
        g_end = g[c_len - 1:c_len, :]
        e_pos = jnp.exp(g)
        e_neg = jnp.exp(-g)
        e_prev = jnp.exp(g - lw)
        e_tail = jnp.exp(g_end - g)

        ab = -kk * e_prev
        rb = r * e_pos
        ab2 = stack(ab)
        bt2 = stack(kka * e_neg)
        kt2 = stack(k2 * e_neg)
        bp2 = stack(kka * e_tail)
        kp2 = stack(k2 * e_tail)
        v2 = stack(v)

        mm = _mm1(jnp.concatenate([ab, rb], axis=0), jnp.concatenate([bt2, kt2], axis=0), _NT)
        yield
        l_b = jnp.where(m_strict, mm[:c_len, :two_c], 0.0)
        l_k = jnp.where(m_strict, mm[:c_len, two_c:], 0.0)
        r_b = jnp.where(m_incl, mm[c_len:, :two_c], 0.0)
        r_k2 = jnp.where(m_incl, mm[c_len:, two_c:], 0.0)

        t_inv = eye2 + l_b
        p = _mm1(l_b, stack(l_b))
        kv = _mm1(l_k, v2)
        yield
        steps = c_len.bit_length() - 2
        for i in range(steps):
            if i + 1 < steps:
                tp = _mm1(jnp.concatenate([t_inv, p], axis=0), stack(p))
                yield
                t_inv = t_inv + tp[:c_len]
                p = tp[c_len:]
            else:
                tp = _mm1(t_inv, stack(p))
                yield
                t_inv = t_inv + tp

        tw = _mm1(t_inv, jnp.concatenate([ab2, stack(kv)], axis=1))
        yield
        sv = st_ref[b, j]
        ws = _mm1(jnp.concatenate([tw[:, :LANES], rb], axis=0), sv, _NT)
        yield
        uv = jnp.concatenate([stack(ws[:c_len] + tw[:, LANES:]), v2], axis=0)
        y = ws[c_len:] + _mm1(jnp.concatenate([r_b, r_k2], axis=1), uv)
        st_ref[b, j] = sv * jnp.exp(g_end) + _mm1(uv, jnp.concatenate([bp2, kp2], axis=0), _TN)
        yield

        mean = _half_sum(y, is_a) * (1.0 / HEAD_DIM)
        yc = y - mean
        var = _half_sum(yc * yc, is_a) * (1.0 / HEAD_DIM)
        y = yc * lax.rsqrt(var + LNX_EPS) * ln_g + ln_b
        return y + _half_sum(r * k2 * r_k, is_a) * v

    mu = mu_ref[...]
    wz = wz_ref[...]

    def body(c, prev_rows):
        t0 = pl.multiple_of(c * c_len, c_len)
        chains, last_rows = [], []
        for b in range(n_rows):
            x = pa_ref[b, pl.ds(t0, c_len), :]
            prev = jnp.where(row_c == 0, prev_rows[b], pltpu.roll(x, 1, axis=0))
            xs = x + (prev - x) * mu
            slab = lambda i, xs=xs: xs[:, i * LANES:(i + 1) * LANES]
            z = slab(3 * n_pairs)
            lora = _mm3(jnp.where(is_a, jnp.tanh(z), z), wz)
            chains += [pair_chunk(b, j, slab(j), slab(n_pairs + j), slab(2 * n_pairs + j),
                                  lora[:, 2 * j * LANES:(2 * j + 1) * LANES],
                                  lora[:, (2 * j + 1) * LANES:(2 * j + 2) * LANES])
                       for j in range(n_pairs)]
            last_rows.append(x[c_len - 1:c_len, :])
        for i, y in enumerate(_round_robin(chains)):
            b, j = divmod(i, n_pairs)
            o_ref[b, pl.ds(t0, c_len), j * LANES:(j + 1) * LANES] = y
        return tuple(last_rows)

    last = lax.fori_loop(0, n_chunks, body, tuple(prev_ref[b, 0:1, :] for b in range(n_rows)))
    for b in range(n_rows):
        prev_ref[b, 0:1, :] = last[b]


def _rwkv(pa, mu, pp, wz, tile=512):
    bsz, seq, _ = pa.shape
    rows = next(r for r in (4, 2, 1) if bsz % r == 0)
    n_pairs = A_HEADS // 2
    whole = lambda shape: pl.BlockSpec(shape, lambda b, s: (0,) * len(shape))
    return pl.pallas_call(
        _rwkv_kernel,
        grid=(bsz // rows, seq // tile),
        in_specs=[pl.BlockSpec((rows, tile, A_PROJ), lambda b, s: (b, s, 0)),
                  whole((1, A_PROJ)), whole((n_pairs, _P_ROWS, LANES)), whole((LANES, n_pairs * 2 * LANES))],
        out_specs=pl.BlockSpec((rows, tile, A_WIDTH), lambda b, s: (b, s, 0)),
        out_shape=jax.ShapeDtypeStruct((bsz, seq, A_WIDTH), F32),
        scratch_shapes=[pltpu.VMEM((rows, n_pairs, LANES, LANES), F32), pltpu.VMEM((rows, 8, A_PROJ), F32)],
        compiler_params=_params(2),
        name="rwkv",
    )(pa, mu.reshape(1, A_PROJ), pp, wz)


def _rwkv_pack_params(w0, w_up, a0, a_up, k_k, k_a, r_k, ln_g, ln_b):
    n_pairs = A_HEADS // 2
    pair = lambda t, j: t[j * LANES:(j + 1) * LANES]
    pps, wzs = [], []
    zeros = jnp.zeros((LORA, LANES), F32)
    for j in range(n_pairs):
        rows = {_P_W0: w0, _P_A0: a0, _P_KK: k_k, _P_KA: k_a, _P_RK: r_k.reshape(-1), _P_LNG: ln_g, _P_LNB: ln_b}
        pps.append(jnp.stack([pair(rows[i], j) if i in rows else jnp.zeros((LANES,), F32)
                              for i in range(_P_ROWS)]))
        wzs.append(jnp.concatenate([w_up[:, j * LANES:(j + 1) * LANES], zeros], axis=0))
        wzs.append(jnp.concatenate([zeros, a_up[:, j * LANES:(j + 1) * LANES]], axis=0))
    return jnp.stack(pps), jnp.concatenate(wzs, axis=1)


def _qk_norm(x, gain):
    ms = _half_sum_mxu(x * x) * (1.0 / HEAD_DIM)
    return x * lax.rsqrt(ms + RMS_EPS) * gain


def _score_bound(gq, gk):
    return (HEAD_DIM * ATTN_SCALE) * jnp.max(jnp.abs(gq)) * jnp.max(jnp.abs(gk))


def _moba_kernel(q_ref, k_ref, v_ref, gq_ref, gk_ref, o_ref,
                 qn_ref, kb_ref, qs_ref, km_ref, vt_ref, bias_ref):
    seq = q_ref.shape[1]
    blk = MOBA_BLOCK
    nb = seq // blk
    ones_rows = vt_ref.shape[2] - HEAD_DIM
    is_a = _lane_is_a((blk, LANES))
    gq, gk = gq_ref[...], gk_ref[...]

    def prepare(i, _):
        for n in (2 * i, 2 * i + 1):
            rows = pl.ds(pl.multiple_of(n * blk, blk), blk)
            qn = _qk_norm(q_ref[0, rows, :].astype(F32), gq)
            qn_ref[rows, :] = qn
            qs = qn * ATTN_SCALE
            qs_ref[0, rows, :] = jnp.where(is_a, qs, 0.0).astype(BF16)
            qs_ref[1, rows, :] = jnp.where(is_a, 0.0, qs).astype(BF16)
            kn = _qk_norm(k_ref[0, rows, :].astype(F32), gk)
            kb_ref[rows, :] = kn.astype(BF16)
            km = jnp.mean(kn, axis=0, keepdims=True)
            km_ref[0, pl.ds(n, 1), :] = jnp.where(is_a[:1], km, 0.0)
            km_ref[1, pl.ds(n, 1), :] = jnp.where(is_a[:1], 0.0, km)
            vt = v_ref[0, rows, :].astype(F32).T.astype(BF16)
            ones = jnp.ones((ones_rows, blk), BF16)
            for h in range(2):
                vt_ref[h, n, :HEAD_DIM, :] = vt[h * HEAD_DIM:(h + 1) * HEAD_DIM]
                vt_ref[h, n, HEAD_DIM:, :] = ones
        return 0

    lax.fori_loop(0, nb // 2, prepare, 0)

    bound = _score_bound(gq, gk)
    bound_is_safe = bound <= MAX_SAFE_BOUND
    bound_row = jnp.full((1, blk), bound, F32)
    blk_row = lax.broadcasted_iota(jnp.int32, (nb, blk), 0)
    key_pos = lax.broadcasted_iota(jnp.int32, (blk, blk), 0)
    qry_pos = lax.broadcasted_iota(jnp.int32, (blk, blk), 1)
    causal = key_pos <= qry_pos

    def two_q_tiles(i, _):
        tiles = (2 * i, 2 * i + 1)
        rows = [pl.ds(pl.multiple_of(qt * blk, blk), blk) for qt in tiles]
        combos = [(t, h) for t in range(2) for h in range(2)]
        for t, h in combos:
            gate = _mm3(km_ref[h], qn_ref[rows[t], :], _NT)
            gate = jnp.where(blk_row < tiles[t], gate, -jnp.inf)
            bias = jnp.full((nb, blk), NEG, F32)
            for _ in range(MOBA_TOPK):
                top = jnp.max(gate, axis=0, keepdims=True)
                hit = (gate == top) & (top > -jnp.inf)
                first = jnp.min(jnp.where(hit, blk_row, nb), axis=0, keepdims=True)
                pick = blk_row == first
                bias = jnp.where(pick, 0.0, bias)
                gate = jnp.where(pick, -jnp.inf, gate)
            bias_ref[t, h] = bias

        qs = {(t, h): qs_ref[h, rows[t], :] for t, h in combos}

        def scores(n, t, h, own, minus=0.0):
            s = _mm(kb_ref[pl.ds(pl.multiple_of(n * blk, blk), blk), :], qs[t, h], _NT)
            if own:
                return jnp.where(causal, s - minus, NEG)
            return s + (bias_ref[t, h, pl.ds(n, 1), :] - minus)

        tail = [(tiles[0], 0, True), (tiles[0], 1, False), (tiles[1], 1, True)]

        def exact_max():
            def past_pair(j, m):
                s = {(k, t, h): scores(2 * j + k, t, h, False) for k in range(2) for t, h in combos}
                return {(t, h): jnp.maximum(m[t, h], jnp.max(jnp.maximum(s[0, t, h], s[1, t, h]),
                                                                 axis=0, keepdims=True)) for t, h in combos}

            m = {(t, h): jnp.full((1, blk), NEG, F32) for t, h in combos}
            for n, t, own in tail:
                for h in range(2):
                    m[t, h] = jnp.maximum(m[t, h], jnp.max(scores(n, t, h, own), axis=0, keepdims=True))
            return lax.fori_loop(0, i, past_pair, m)

        m = lax.cond(bound_is_safe, lambda: {c: bound_row for c in combos}, exact_max)

        def weighted_v(s, n, h):
            return _mm(vt_ref[h, n], jnp.exp(s).astype(BF16))

        def past_pair(j, pv):
            s = {(k, t, h): scores(2 * j + k, t, h, False, m[t, h]) for k in range(2) for t, h in combos}
            return {(t, h): pv[t, h] + weighted_v(s[0, t, h], 2 * j, h) + weighted_v(s[1, t, h], 2 * j + 1, h)
                    for t, h in combos}

        s = {(k, h): scores(n, t, h, own, m[t, h]) for k, (n, t, own) in enumerate(tail) for h in range(2)}
        pv = {(t, h): sum(weighted_v(s[k, h], n, h) for k, (n, tt, _) in enumerate(tail) if tt == t)
              for t, h in combos}
        pv = lax.fori_loop(0, i, past_pair, pv)
        for t in range(2):
            o_t = jnp.concatenate([pv[t, h][:HEAD_DIM] / pv[t, h][HEAD_DIM:HEAD_DIM + 1] for h in range(2)],
                                  axis=0)
            o_ref[0, rows[t], :] = o_t.T
        return 0

    lax.fori_loop(0, nb // 2, two_q_tiles, 0)


def _moba(pb, gq, gk):
    bsz, seq, _ = pb.shape
    n_pairs = B_HEADS // 2
    nb = seq // MOBA_BLOCK
    ones_rows = 16
    slab = lambda off: pl.BlockSpec((1, seq, LANES), lambda b, j: (b, 0, off + j))
    gain = pl.BlockSpec((1, LANES), lambda b, j: (0, 0))
    return pl.pallas_call(
        _moba_kernel,
        grid=(bsz, n_pairs),
        in_specs=[slab(0), slab(n_pairs), slab(2 * n_pairs), gain, gain],
        out_specs=pl.BlockSpec((1, seq, LANES), lambda b, j: (b, 0, j)),
        out_shape=jax.ShapeDtypeStruct((bsz, seq, B_WIDTH), F32),
        scratch_shapes=[pltpu.VMEM((seq, LANES), F32), pltpu.VMEM((seq, LANES), BF16),
                        pltpu.VMEM((2, seq, LANES), BF16), pltpu.VMEM((2, nb, LANES), F32),
                        pltpu.VMEM((2, nb, HEAD_DIM + ones_rows, MOBA_BLOCK), BF16),
                        pltpu.VMEM((2, 2, nb, MOBA_BLOCK), F32)],
        compiler_params=_params(2),
        name="moba",
    )(pb, pb, pb, jnp.tile(gq, 2).reshape(1, LANES), jnp.tile(gk, 2).reshape(1, LANES))


def _dilated_kernel(q_ref, k_ref, v_ref, gq_ref, gk_ref, o_ref,
                    qn_ref, kn_ref, vv_ref, q1_ref, k1_ref, v1_ref, qs_ref, kb_ref, vt_ref, og_ref, lse_ref):
    seq = q_ref.shape[1]
    n_blocks = seq // BAND
    ones_rows = vt_ref.shape[2] - HEAD_DIM
    tile = 512

    def normalise(i, _):
        rows = pl.ds(pl.multiple_of(i * tile, tile), tile)
        qn_ref[rows, :] = _qk_norm(q_ref[0, rows, :].astype(F32), gq_ref[...] * ATTN_SCALE)
        kn_ref[rows, :] = _qk_norm(k_ref[0, rows, :].astype(F32), gk_ref[...])
        vv_ref[rows, :] = v_ref[0, rows, :].astype(F32)
        return 0

    lax.fori_loop(0, seq // tile, normalise, 0)
    bound = _score_bound(gq_ref[...], gk_ref[...])
    bound_is_safe = bound <= MAX_SAFE_BOUND
    bound_row = jnp.full((1, BAND), bound, F32)

    is_a = _lane_is_a((BAND, LANES))
    key_j = lax.broadcasted_iota(jnp.int32, (2 * BAND, BAND), 0)
    qry_i = lax.broadcasted_iota(jnp.int32, (2 * BAND, BAND), 1)
    band_bias = jnp.where((key_j >= qry_i) & (key_j <= qry_i + BAND), 0.0, NEG)
    band_bias = jnp.concatenate([band_bias, band_bias], axis=1)
    band_bias_less_bound = band_bias - bound
    ones = jnp.ones((ones_rows, BAND), BF16)
    vt_rows = HEAD_DIM + ones_rows

    def own_rows(gb):
        return pl.ds(pl.multiple_of(gb * BAND, BAND), BAND)

    def rows_in_previous_level(g, gb):
        per_residue = n_blocks // DILATIONS[g]
        segment, c = gb // per_residue, gb % per_residue
        start = (segment // DIL_RATIO) * (seq // DILATIONS[g - 1]) + segment % DIL_RATIO
        return pl.ds(start + c * (BAND * DIL_RATIO), BAND, stride=DIL_RATIO)

    for g, dil in enumerate(DILATIONS):
        per_residue = n_blocks // dil
        sources = (qn_ref, kn_ref, vv_ref) if g <= 1 else (q1_ref, k1_ref, v1_ref)

        def gather(i, _, g=g, sources=sources):
            for gb in [DIL_GROUP * i + j for j in range(DIL_GROUP)]:
                rows = own_rows(gb) if g == 0 else rows_in_previous_level(g, gb)
                dst = own_rows(gb)
                q, k, v = (ref[rows, :] for ref in sources)
                if g == 1:
                    q1_ref[dst, :], k1_ref[dst, :], v1_ref[dst, :] = q, k, v
                qs_ref[0, dst, :] = jnp.where(is_a, q, 0.0).astype(BF16)
                qs_ref[1, dst, :] = jnp.where(is_a, 0.0, q).astype(BF16)
                kb_ref[dst, :] = k.astype(BF16)
                vt = v.T.astype(BF16)
                for h in range(2):
                    vt_ref[h, gb, :HEAD_DIM, :] = vt[h * HEAD_DIM:(h + 1) * HEAD_DIM]
                    vt_ref[h, gb, HEAD_DIM:, :] = ones
            return 0

        lax.fori_loop(0, n_blocks // DIL_GROUP, gather, 0)

        def band_block(gb, use_bound, per_residue=per_residue, g=g):
            prev = jnp.maximum(gb - 1, 0)
            first = jnp.where(gb % per_residue == 0, NEG, 0.0)
            kb = jnp.concatenate([kb_ref[own_rows(prev), :], kb_ref[own_rows(gb), :]], axis=0)

            def scores(bias):
                q_both = jnp.concatenate([qs_ref[0, own_rows(gb), :], qs_ref[1, own_rows(gb), :]], axis=0)
                s = _mm(kb, q_both, _NT) + bias
                return jnp.concatenate([s[:BAND] + first, s[BAND:]], axis=0)

            if use_bound:
                m = jnp.concatenate([bound_row, bound_row], axis=1)
                s = scores(band_bias_less_bound)
                yield
                p = jnp.exp(s)
            else:
                m = jnp.max(scores(band_bias), axis=0, keepdims=True)
                yield
                s = scores(band_bias)
                yield
                p = jnp.exp(s - m)
            vt_both = jnp.concatenate([jnp.concatenate([vt_ref[h, prev], vt_ref[h, gb]], axis=1)
                                       for h in range(2)], axis=0)
            pv = _mm(vt_both, p.astype(BF16))
            yield
            pv = [pv[h * vt_rows:(h + 1) * vt_rows, h * BAND:(h + 1) * BAND] for h in range(2)]
            l = [x[HEAD_DIM:HEAD_DIM + 1] for x in pv]
            o_t = jnp.concatenate([pv[h][:HEAD_DIM] / l[h] for h in range(2)], axis=0)
            lse_t = jnp.concatenate([jnp.broadcast_to(m[:, h * BAND:(h + 1) * BAND] + jnp.log(l[h]),
                                                      (HEAD_DIM, BAND)) for h in range(2)], axis=0)
            out_rows = own_rows(gb) if g <= 1 else rows_in_previous_level(g, gb)
            og_ref[g, out_rows, :] = o_t.T
            lse_ref[g, out_rows, :] = lse_t.T

        def group(i, _, use_bound, band_block=band_block):
            _round_robin([band_block(DIL_GROUP * i + j, use_bound) for j in range(DIL_GROUP)])
            return 0

        for use_bound in (True, False):
            @pl.when(bound_is_safe == use_bound)
            def _(use_bound=use_bound, group=group):
                lax.fori_loop(0, n_blocks // DIL_GROUP, functools.partial(group, use_bound=use_bound), 0)

    segment_len = seq // DILATIONS[1]

    def mix(i, _):
        rows = pl.ds(pl.multiple_of(i * tile, tile), tile)
        start = i * tile
        tokens = pl.ds(start // segment_len + DIL_RATIO * (start % segment_len), tile, stride=DIL_RATIO)
        where = (tokens, rows, rows)
        lse = [lse_ref[g, where[g], :] for g in range(len(DILATIONS))]
        top = functools.reduce(jnp.maximum, lse)
        w = [jnp.exp(x - top) for x in lse]
        o_ref[0, tokens, :] = sum(w[g] * og_ref[g, where[g], :] for g in range(len(DILATIONS))) / sum(w)
        return 0

    lax.fori_loop(0, seq // tile, mix, 0)


def _dilated(pc, gq, gk):
    bsz, seq, _ = pc.shape
    n_pairs = C_HEADS // 2
    n_pat = len(DILATIONS)
    ones_rows = 16
    slab = lambda off: pl.BlockSpec((1, seq, LANES), lambda b, j: (b, 0, off + j))
    gain = pl.BlockSpec((1, LANES), lambda b, j: (0, 0))
    return pl.pallas_call(
        _dilated_kernel,
        grid=(bsz, n_pairs),
        in_specs=[slab(0), slab(n_pairs), slab(2 * n_pairs), gain, gain],
        out_specs=pl.BlockSpec((1, seq, LANES), lambda b, j: (b, 0, j)),
        out_shape=jax.ShapeDtypeStruct((bsz, seq, C_WIDTH), F32),
        scratch_shapes=[pltpu.VMEM((seq, LANES), F32)] * 6 + [
                        pltpu.VMEM((2, seq, LANES), BF16), pltpu.VMEM((seq, LANES), BF16),
                        pltpu.VMEM((2, seq // BAND, HEAD_DIM + ones_rows, BAND), BF16),
                        pltpu.VMEM((n_pat, seq, LANES), F32), pltpu.VMEM((n_pat, seq, LANES), F32)],
        compiler_params=_params(2),
        name="dilated",
    )(pc, pc, pc, jnp.tile(gq, 2).reshape(1, LANES), jnp.tile(gk, 2).reshape(1, LANES))


def kernel(x, norm_g, w_in, w_out, tshift_mu, decay_w0, decay_up, iclr_a0, iclr_up,
           k_k, k_a, r_k, lnx_g, lnx_b, moba_q_g, moba_k_g, dil_q_g, dil_k_g):
    bsz, seq, d_model = x.shape
    depth = norm_g.shape[0]
    assert seq % (max(DILATIONS) * BAND) == 0 and seq % MOBA_BLOCK == 0 and seq % RWKV_CHUNK == 0
    x2d = x.reshape(bsz * seq, d_model)
    w_in_bf = w_in.astype(BF16)
    w_out_bf = w_out.astype(BF16)
    pa, pb, pc, gate = _proj(x2d, in_args=(norm_g[0], w_in_bf[0]))
    for l in range(depth):
        pp, wz = _rwkv_pack_params(decay_w0[l], decay_up[l], iclr_a0[l], iclr_up[l],
                                   k_k[l], k_a[l], r_k[l], lnx_g[l], lnx_b[l])
        ya = _rwkv(pa.reshape(bsz, seq, A_PROJ), tshift_mu[l], pp, wz)
        yb = _moba(pb.reshape(bsz, seq, B_PROJ), moba_q_g[l], moba_k_g[l])
        yc = _dilated(pc.reshape(bsz, seq, C_PROJ), dil_q_g[l], dil_k_g[l])
        out_args = (ya.reshape(bsz * seq, A_WIDTH), yb.reshape(bsz * seq, B_WIDTH),
                    yc.reshape(bsz * seq, C_WIDTH), gate, w_out_bf[l])
        if l + 1 < depth:
            x2d, pa, pb, pc, gate = _proj(x2d, out_args, (norm_g[l + 1], w_in_bf[l + 1]))
        else:
            (x2d,) = _proj(x2d, out_args)
    return x2d.reshape(bsz, seq, d_model)
```

```python
import functools

import jax
import jax.numpy as jnp
from jax import lax
from jax.experimental import pallas as pl
from jax.experimental.pallas import tpu as pltpu

F32 = jnp.float32
BF16 = jnp.bfloat16

HEAD_DIM = 64
LANES = 128
A_HEADS, B_HEADS, C_HEADS = 6, 4, 6
A_WIDTH, B_WIDTH, C_WIDTH = A_HEADS * HEAD_DIM, B_HEADS * HEAD_DIM, C_HEADS * HEAD_DIM
LORA = 64
A_PROJ = 3 * A_WIDTH + 2 * LORA
B_PROJ = 3 * B_WIDTH
C_PROJ = 3 * C_WIDTH
MIX_WIDTH = A_WIDTH + B_WIDTH + C_WIDTH
PROJ_WIDTH = A_PROJ + B_PROJ + C_PROJ + MIX_WIDTH
MOBA_BLOCK = 256
MOBA_TOPK = 3
DILATIONS = (1, 4, 16)
DIL_RATIO = 4
BAND = 128
DIL_GROUP = 16
RMS_EPS = 1e-6
LNX_EPS = HEAD_DIM * 1e-5
ATTN_SCALE = HEAD_DIM ** -0.5
RWKV_CHUNK = 64
NEG = -1e30
MAX_SAFE_BOUND = 40.0
VMEM_LIMIT = 56 * 1024 * 1024


def _params(n_axes):
    return pltpu.CompilerParams(dimension_semantics=("arbitrary",) * n_axes,
                                vmem_limit_bytes=VMEM_LIMIT)


_NN = (((1,), (0,)), ((), ()))
_NT = (((1,), (1,)), ((), ()))
_TN = (((0,), (0,)), ((), ()))


def _mm(a, b, dims=_NN):
    return lax.dot_general(a, b, dims, preferred_element_type=F32)


def _split(x):
    hi = x.astype(BF16)
    lo = (x - hi.astype(F32)).astype(BF16)
    return hi, lo


def _mm3(a, b, dims=_NN):
    ah, al = _split(a)
    bh, bl = _split(b)
    return _mm(ah, bh, dims) + (_mm(ah, bl, dims) + _mm(al, bh, dims))


def _mm1(a, b, dims=_NN):
    return _mm(a.astype(BF16), b.astype(BF16), dims)


def _half_sum(x, is_a):
    sa = jnp.sum(jnp.where(is_a, x, 0.0), axis=-1, keepdims=True)
    sb = jnp.sum(jnp.where(is_a, 0.0, x), axis=-1, keepdims=True)
    return jnp.where(is_a, sa, sb)


def _half_sum_mxu(x):
    row = lax.broadcasted_iota(jnp.int32, (LANES, LANES), 0) // HEAD_DIM
    col = lax.broadcasted_iota(jnp.int32, (LANES, LANES), 1) // HEAD_DIM
    ones_bd = (row == col).astype(BF16)
    hi, lo = _split(x)
    return _mm(hi, ones_bd) + _mm(lo, ones_bd)


def _round_robin(chains):
    results = [None] * len(chains)
    live = list(range(len(chains)))
    while live:
        for i in list(live):
            try:
                next(chains[i])
            except StopIteration as done:
                results[i] = done.value
                live.remove(i)
    return results


def _lane_is_a(shape):
    return lax.broadcasted_iota(jnp.int32, shape, len(shape) - 1) < HEAD_DIM


_PROJ_WIDTHS = (A_PROJ, B_PROJ, C_PROJ, MIX_WIDTH)
_PROJ_DTYPES = (F32, BF16, BF16, BF16)


def _proj_kernel(*refs, has_out, has_in):
    refs = list(refs)
    x = refs.pop(0)[...]
    if has_out:
        ya_ref, yb_ref, yc_ref, gt_ref, w_out_ref = (refs.pop(0) for _ in range(5))
    if has_in:
        g_ref, w_in_ref = refs.pop(0), refs.pop(0)
    if has_out:
        lo = 0
        for y_ref in (ya_ref, yb_ref, yc_ref):
            hi = lo + y_ref.shape[-1]
            g = gt_ref[:, lo:hi].astype(F32)
            y = y_ref[...] * (g * jax.nn.sigmoid(g))
            x = x + _mm(y.astype(BF16), w_out_ref[lo:hi, :])
            lo = hi
        refs.pop(0)[...] = x
    if has_in:
        ms = jnp.mean(x * x, axis=-1, keepdims=True)
        h = (x * lax.rsqrt(ms + RMS_EPS) * g_ref[...]).astype(BF16)
        lo = 0
        for ref in refs:
            hi = lo + ref.shape[-1]
            ref[...] = _mm(h, w_in_ref[:, lo:hi]).astype(ref.dtype)
            lo = hi


def _proj(x2d, out_args=None, in_args=None, tm=512):
    m, d = x2d.shape
    row = lambda w: pl.BlockSpec((tm, w), lambda i: (i, 0))
    whole = lambda shape: pl.BlockSpec(shape, lambda i: (0, 0))
    args, in_specs, out_specs, out_shape = [x2d], [row(d)], [], []
    if out_args is not None:
        args += list(out_args)
        in_specs += [row(A_WIDTH), row(B_WIDTH), row(C_WIDTH), row(MIX_WIDTH), whole((MIX_WIDTH, d))]
        out_specs.append(row(d))
        out_shape.append(jax.ShapeDtypeStruct((m, d), F32))
    if in_args is not None:
        gain, w_in = in_args
        args += [gain.reshape(1, d), w_in]
        in_specs += [whole((1, d)), whole((d, PROJ_WIDTH))]
        out_specs += [row(w) for w in _PROJ_WIDTHS]
        out_shape += [jax.ShapeDtypeStruct((m, w), dt) for w, dt in zip(_PROJ_WIDTHS, _PROJ_DTYPES)]
    return pl.pallas_call(
        functools.partial(_proj_kernel, has_out=out_args is not None, has_in=in_args is not None),
        grid=(m // tm,),
        in_specs=in_specs, out_specs=out_specs, out_shape=out_shape,
        compiler_params=_params(1),
        name="proj",
    )(*args)


_P_W0, _P_A0, _P_KK, _P_KA, _P_RK, _P_LNG, _P_LNB = range(7)
_P_ROWS = 8


def _rwkv_kernel(pa_ref, mu_ref, pp_ref, wz_ref, o_ref, st_ref, prev_ref):
    n_rows, tile = pa_ref.shape[0], pa_ref.shape[1]
    c_len = RWKV_CHUNK
    two_c = 2 * c_len
    n_chunks = tile // c_len
    n_pairs = A_HEADS // 2
    is_a = _lane_is_a((c_len, LANES))

    row_c = lax.broadcasted_iota(jnp.int32, (c_len, A_PROJ), 0)
    ri = lax.broadcasted_iota(jnp.int32, (c_len, c_len), 0)
    ci = lax.broadcasted_iota(jnp.int32, (c_len, c_len), 1)
    tril_c = (ri >= ci).astype(BF16)
    r2 = lax.broadcasted_iota(jnp.int32, (c_len, two_c), 0)
    c2 = lax.broadcasted_iota(jnp.int32, (c_len, two_c), 1) % c_len
    m_strict = r2 > c2
    m_incl = r2 >= c2
    eye2 = (r2 == c2).astype(F32)

    def stack(x):
        return jnp.concatenate([jnp.where(is_a, x, 0.0), jnp.where(is_a, 0.0, x)], axis=0)

    @pl.when(pl.program_id(1) == 0)
    def _():
        st_ref[...] = jnp.zeros_like(st_ref)
        prev_ref[...] = jnp.zeros_like(prev_ref)

    def pair_chunk(b, j, r, k, v, lora_w, lora_a):
        pp = pp_ref[j]
        prow = lambda i: pp[i:i + 1, :]
        w0, a0, k_k, k_a, r_k = prow(_P_W0), prow(_P_A0), prow(_P_KK), prow(_P_KA), prow(_P_RK)
        ln_g, ln_b = prow(_P_LNG), prow(_P_LNB)
        w = -jax.nn.softplus(-(w0 + lora_w)) - 0.5
        lw = -jnp.exp(w)
        a = jax.nn.sigmoid(a0 + lora_a)
        kk = k * k_k
        kk = kk * lax.rsqrt(_half_sum(kk * kk, is_a) + 1e-12)
        k2 = k * (1.0 + (a - 1.0) * k_a)
        kka = kk * a

        l1 = lw.astype(BF16)
        rem = lw - l1.astype(F32)
        l2 = rem.astype(BF16)
        l3 = (rem - l2.astype(F32)).astype(BF16)
        g = _mm(tril_c, l1) + (_mm(tril_c, l2) + _mm(tril_c, l3))
        yield
        g_end = g[c_len - 1:c_len, :]
        e_pos = jnp.exp(g)
        e_neg = jnp.exp(-g)
        e_prev = jnp.exp(g - lw)
        e_tail = jnp.exp(g_end - g)

        ab = -kk * e_prev
        rb = r * e_pos
        ab2 = stack(ab)
        bt2 = stack(kka * e_neg)
        kt2 = stack(k2 * e_neg)
        bp2 = stack(kka * e_tail)
        kp2 = stack(k2 * e_tail)
        v2 = stack(v)

        mm = _mm1(jnp.concatenate([ab, rb], axis=0), jnp.concatenate([bt2, kt2], axis=0), _NT)
        yield
        l_b = jnp.where(m_strict, mm[:c_len, :two_c], 0.0)
        l_k = jnp.where(m_strict, mm[:c_len, two_c:], 0.0)
        r_b = jnp.where(m_incl, mm[c_len:, :two_c], 0.0)
        r_k2 = jnp.where(m_incl, mm[c_len:, two_c:], 0.0)

        t_inv = eye2 + l_b
        p = _mm1(l_b, stack(l_b))
        kv = _mm1(l_k, v2)
        yield
        steps = c_len.bit_length() - 2
        for i in range(steps):
            if i + 1 < steps:
                tp = _mm1(jnp.concatenate([t_inv, p], axis=0), stack(p))
                yield
                t_inv = t_inv + tp[:c_len]
                p = tp[c_len:]
            else:
                tp = _mm1(t_inv, stack(p))
                yield
                t_inv = t_inv + tp

        tw = _mm1(t_inv, jnp.concatenate([ab2, stack(kv)], axis=1))
        yield
        sv = st_ref[b, j]
        ws = _mm1(jnp.concatenate([tw[:, :LANES], rb], axis=0), sv, _NT)
        yield
        uv = jnp.concatenate([stack(ws[:c_len] + tw[:, LANES:]), v2], axis=0)
        y = ws[c_len:] + _mm1(jnp.concatenate([r_b, r_k2], axis=1), uv)
        st_ref[b, j] = sv * jnp.exp(g_end) + _mm1(uv, jnp.concatenate([bp2, kp2], axis=0), _TN)
        yield

        mean = _half_sum(y, is_a) * (1.0 / HEAD_DIM)
        yc = y - mean
        var = _half_sum(yc * yc, is_a) * (1.0 / HEAD_DIM)
        y = yc * lax.rsqrt(var + LNX_EPS) * ln_g + ln_b
        return y + _half_sum(r * k2 * r_k, is_a) * v

    mu = mu_ref[...]
    wz = wz_ref[...]

    def body(c, prev_rows):
        t0 = pl.multiple_of(c * c_len, c_len)
        chains, last_rows = [], []
        for b in range(n_rows):
            x = pa_ref[b, pl.ds(t0, c_len), :]
            prev = jnp.where(row_c == 0, prev_rows[b], pltpu.roll(x, 1, axis=0))
            xs = x + (prev - x) * mu
            slab = lambda i, xs=xs: xs[:, i * LANES:(i + 1) * LANES]
            z = slab(3 * n_pairs)
            lora = _mm3(jnp.where(is_a, jnp.tanh(z), z), wz)
            chains += [pair_chunk(b, j, slab(j), slab(n_pairs + j), slab(2 * n_pairs + j),
                                  lora[:, 2 * j * LANES:(2 * j + 1) * LANES],
                                  lora[:, (2 * j + 1) * LANES:(2 * j + 2) * LANES])
                       for j in range(n_pairs)]
            last_rows.append(x[c_len - 1:c_len, :])
        for i, y in enumerate(_round_robin(chains)):
            b, j = divmod(i, n_pairs)
            o_ref[b, pl.ds(t0, c_len), j * LANES:(j + 1) * LANES] = y
        return tuple(last_rows)

    last = lax.fori_loop(0, n_chunks, body, tuple(prev_ref[b, 0:1, :] for b in range(n_rows)))
    for b in range(n_rows):
        prev_ref[b, 0:1, :] = last[b]


def _rwkv(pa, mu, pp, wz, tile=512):
    bsz, seq, _ = pa.shape
    rows = next(r for r in (4, 2, 1) if bsz % r == 0)
    n_pairs = A_HEADS // 2
    whole = lambda shape: pl.BlockSpec(shape, lambda b, s: (0,) * len(shape))
    return pl.pallas_call(
        _rwkv_kernel,
        grid=(bsz // rows, seq // tile),
        in_specs=[pl.BlockSpec((rows, tile, A_PROJ), lambda b, s: (b, s, 0)),
                  whole((1, A_PROJ)), whole((n_pairs, _P_ROWS, LANES)), whole((LANES, n_pairs * 2 * LANES))],
        out_specs=pl.BlockSpec((rows, tile, A_WIDTH), lambda b, s: (b, s, 0)),
        out_shape=jax.ShapeDtypeStruct((bsz, seq, A_WIDTH), F32),
        scratch_shapes=[pltpu.VMEM((rows, n_pairs, LANES, LANES), F32), pltpu.VMEM((rows, 8, A_PROJ), F32)],
        compiler_params=_params(2),
        name="rwkv",
    )(pa, mu.reshape(1, A_PROJ), pp, wz)


def _rwkv_pack_params(w0, w_up, a0, a_up, k_k, k_a, r_k, ln_g, ln_b):
    n_pairs = A_HEADS // 2
    pair = lambda t, j: t[j * LANES:(j + 1) * LANES]
    pps, wzs = [], []
    zeros = jnp.zeros((LORA, LANES), F32)
    for j in range(n_pairs):
        rows = {_P_W0: w0, _P_A0: a0, _P_KK: k_k, _P_KA: k_a, _P_RK: r_k.reshape(-1), _P_LNG: ln_g, _P_LNB: ln_b}
        pps.append(jnp.stack([pair(rows[i], j) if i in rows else jnp.zeros((LANES,), F32)
                              for i in range(_P_ROWS)]))
        wzs.append(jnp.concatenate([w_up[:, j * LANES:(j + 1) * LANES], zeros], axis=0))
        wzs.append(jnp.concatenate([zeros, a_up[:, j * LANES:(j + 1) * LANES]], axis=0))
    return jnp.stack(pps), jnp.concatenate(wzs, axis=1)


def _qk_norm(x, gain):
    ms = _half_sum_mxu(x * x) * (1.0 / HEAD_DIM)
    return x * lax.rsqrt(ms + RMS_EPS) * gain


def _score_bound(gq, gk):
    return (HEAD_DIM * ATTN_SCALE) * jnp.max(jnp.abs(gq)) * jnp.max(jnp.abs(gk))


def _moba_kernel(q_ref, k_ref, v_ref, gq_ref, gk_ref, o_ref,
                 qn_ref, kb_ref, qs_ref, km_ref, vt_ref, bias_ref):
    seq = q_ref.shape[1]
    blk = MOBA_BLOCK
    nb = seq // blk
    ones_rows = vt_ref.shape[2] - HEAD_DIM
    is_a = _lane_is_a((blk, LANES))
    gq, gk = gq_ref[...], gk_ref[...]

    def prepare(i, _):
        for n in (2 * i, 2 * i + 1):
            rows = pl.ds(pl.multiple_of(n * blk, blk), blk)
            qn = _qk_norm(q_ref[0, rows, :].astype(F32), gq)
            qn_ref[rows, :] = qn
            qs = qn * ATTN_SCALE
            qs_ref[0, rows, :] = jnp.where(is_a, qs, 0.0).astype(BF16)
            qs_ref[1, rows, :] = jnp.where(is_a, 0.0, qs).astype(BF16)
            kn = _qk_norm(k_ref[0, rows, :].astype(F32), gk)
            kb_ref[rows, :] = kn.astype(BF16)
            km = jnp.mean(kn, axis=0, keepdims=True)
            km_ref[0, pl.ds(n, 1), :] = jnp.where(is_a[:1], km, 0.0)
            km_ref[1, pl.ds(n, 1), :] = jnp.where(is_a[:1], 0.0, km)
            vt = v_ref[0, rows, :].astype(F32).T.astype(BF16)
            ones = jnp.ones((ones_rows, blk), BF16)
            for h in range(2):
                vt_ref[h, n, :HEAD_DIM, :] = vt[h * HEAD_DIM:(h + 1) * HEAD_DIM]
                vt_ref[h, n, HEAD_DIM:, :] = ones
        return 0

    lax.fori_loop(0, nb // 2, prepare, 0)

    bound = _score_bound(gq, gk)
    bound_is_safe = bound <= MAX_SAFE_BOUND
    bound_row = jnp.full((1, blk), bound, F32)
    blk_row = lax.broadcasted_iota(jnp.int32, (nb, blk), 0)
    key_pos = lax.broadcasted_iota(jnp.int32, (blk, blk), 0)
    qry_pos = lax.broadcasted_iota(jnp.int32, (blk, blk), 1)
    causal = key_pos <= qry_pos

    combos = [(t, h) for t in range(2) for h in range(2)]

    def select_blocks(i, slot):
        for t, h in combos:
            qt = 2 * i + t
            gate = _mm3(km_ref[h], qn_ref[pl.ds(pl.multiple_of(qt * blk, blk), blk), :], _NT)
            gate = jnp.where(blk_row < qt, gate, -jnp.inf)
            bias = jnp.full((nb, blk), NEG, F32)
            for _ in range(MOBA_TOPK):
                top = jnp.max(gate, axis=0, keepdims=True)
                hit = (gate == top) & (top > -jnp.inf)
                first = jnp.min(jnp.where(hit, blk_row, nb), axis=0, keepdims=True)
                pick = blk_row == first
                bias = jnp.where(pick, 0.0, bias)
                gate = jnp.where(pick, -jnp.inf, gate)
            bias_ref[slot, t, h] = bias

    select_blocks(0, 0)

    def two_q_tiles(i, _):
        tiles = (2 * i, 2 * i + 1)
        rows = [pl.ds(pl.multiple_of(qt * blk, blk), blk) for qt in tiles]
        slot = i % 2

        qs = {(t, h): qs_ref[h, rows[t], :] for t, h in combos}

        def scores(n, t, h, own, minus=0.0):
            s = _mm(kb_ref[pl.ds(pl.multiple_of(n * blk, blk), blk), :], qs[t, h], _NT)
            if own:
                return jnp.where(causal, s - minus, NEG)
            return s + (bias_ref[slot, t, h, pl.ds(n, 1), :] - minus)

        tail = [(tiles[0], 0, True), (tiles[0], 1, False), (tiles[1], 1, True)]

        def exact_max():
            def past_pair(j, m):
                s = {(k, t, h): scores(2 * j + k, t, h, False) for k in range(2) for t, h in combos}
                return {(t, h): jnp.maximum(m[t, h], jnp.max(jnp.maximum(s[0, t, h], s[1, t, h]),
                                                                 axis=0, keepdims=True)) for t, h in combos}

            m = {(t, h): jnp.full((1, blk), NEG, F32) for t, h in combos}
            for n, t, own in tail:
                for h in range(2):
                    m[t, h] = jnp.maximum(m[t, h], jnp.max(scores(n, t, h, own), axis=0, keepdims=True))
            return lax.fori_loop(0, i, past_pair, m)

        m = lax.cond(bound_is_safe, lambda: {c: bound_row for c in combos}, exact_max)

        def weighted_v(s, n, h):
            return _mm(vt_ref[h, n], jnp.exp(s).astype(BF16))

        def past_pair(j, pv):
            s = {(k, t, h): scores(2 * j + k, t, h, False, m[t, h]) for k in range(2) for t, h in combos}
            return {(t, h): pv[t, h] + weighted_v(s[0, t, h], 2 * j, h) + weighted_v(s[1, t, h], 2 * j + 1, h)
                    for t, h in combos}

        select_blocks(jnp.minimum(i + 1, nb // 2 - 1), 1 - slot)
        s = {(k, h): scores(n, t, h, own, m[t, h]) for k, (n, t, own) in enumerate(tail) for h in range(2)}
        pv = {(t, h): sum(weighted_v(s[k, h], n, h) for k, (n, tt, _) in enumerate(tail) if tt == t)
              for t, h in combos}
        pv = lax.fori_loop(0, i, past_pair, pv)
        for t in range(2):
            o_t = jnp.concatenate([pv[t, h][:HEAD_DIM] / pv[t, h][HEAD_DIM:HEAD_DIM + 1] for h in range(2)],
                                  axis=0)
            o_ref[0, rows[t], :] = o_t.T
        return 0

    lax.fori_loop(0, nb // 2, two_q_tiles, 0)


def _moba(pb, gq, gk):
    bsz, seq, _ = pb.shape
    n_pairs = B_HEADS // 2
    nb = seq // MOBA_BLOCK
    ones_rows = 16
    slab = lambda off: pl.BlockSpec((1, seq, LANES), lambda b, j: (b, 0, off + j))
    gain = pl.BlockSpec((1, LANES), lambda b, j: (0, 0))
    return pl.pallas_call(
        _moba_kernel,
        grid=(bsz, n_pairs),
        in_specs=[slab(0), slab(n_pairs), slab(2 * n_pairs), gain, gain],
        out_specs=pl.BlockSpec((1, seq, LANES), lambda b, j: (b, 0, j)),
        out_shape=jax.ShapeDtypeStruct((bsz, seq, B_WIDTH), F32),
        scratch_shapes=[pltpu.VMEM((seq, LANES), F32), pltpu.VMEM((seq, LANES), BF16),
                        pltpu.VMEM((2, seq, LANES), BF16), pltpu.VMEM((2, nb, LANES), F32),
                        pltpu.VMEM((2, nb, HEAD_DIM + ones_rows, MOBA_BLOCK), BF16),
                        pltpu.VMEM((2, 2, 2, nb, MOBA_BLOCK), F32)],
        compiler_params=_params(2),
        name="moba",
    )(pb, pb, pb, jnp.tile(gq, 2).reshape(1, LANES), jnp.tile(gk, 2).reshape(1, LANES))


def _dilated_kernel(q_ref, k_ref, v_ref, gq_ref, gk_ref, o_ref,
                    qn_ref, kn_ref, vv_ref, q1_ref, k1_ref, v1_ref, qs_ref, kb_ref, vt_ref, og_ref, lse_ref):
    seq = q_ref.shape[1]
    n_blocks = seq // BAND
    ones_rows = vt_ref.shape[2] - HEAD_DIM
    tile = 512

    def normalise(i, _):
        rows = pl.ds(pl.multiple_of(i * tile, tile), tile)
        qn_ref[rows, :] = _qk_norm(q_ref[0, rows, :].astype(F32), gq_ref[...] * ATTN_SCALE)
        kn_ref[rows, :] = _qk_norm(k_ref[0, rows, :].astype(F32), gk_ref[...])
        vv_ref[rows, :] = v_ref[0, rows, :].astype(F32)
        return 0

    lax.fori_loop(0, seq // tile, normalise, 0)
    bound = _score_bound(gq_ref[...], gk_ref[...])
    bound_is_safe = bound <= MAX_SAFE_BOUND
    bound_row = jnp.full((1, BAND), bound, F32)

    is_a = _lane_is_a((BAND, LANES))
    key_j = lax.broadcasted_iota(jnp.int32, (2 * BAND, BAND), 0)
    qry_i = lax.broadcasted_iota(jnp.int32, (2 * BAND, BAND), 1)
    band_bias = jnp.where((key_j >= qry_i) & (key_j <= qry_i + BAND), 0.0, NEG)
    band_bias = jnp.concatenate([band_bias, band_bias], axis=1)
    band_bias_less_bound = band_bias - bound
    ones = jnp.ones((ones_rows, BAND), BF16)
    vt_rows = HEAD_DIM + ones_rows

    def own_rows(gb):
        return pl.ds(pl.multiple_of(gb * BAND, BAND), BAND)

    def rows_in_previous_level(g, gb):
        per_residue = n_blocks // DILATIONS[g]
        segment, c = gb // per_residue, gb % per_residue
        start = (segment // DIL_RATIO) * (seq // DILATIONS[g - 1]) + segment % DIL_RATIO
        return pl.ds(start + c * (BAND * DIL_RATIO), BAND, stride=DIL_RATIO)

    for g, dil in enumerate(DILATIONS):
        per_residue = n_blocks // dil
        sources = (qn_ref, kn_ref, vv_ref) if g <= 1 else (q1_ref, k1_ref, v1_ref)

        def gather(i, _, g=g, sources=sources):
            for gb in [DIL_GROUP * i + j for j in range(DIL_GROUP)]:
                rows = own_rows(gb) if g == 0 else rows_in_previous_level(g, gb)
                dst = own_rows(gb)
                q, k, v = (ref[rows, :] for ref in sources)
                if g == 1:
                    q1_ref[dst, :], k1_ref[dst, :], v1_ref[dst, :] = q, k, v
                qs_ref[0, dst, :] = jnp.where(is_a, q, 0.0).astype(BF16)
                qs_ref[1, dst, :] = jnp.where(is_a, 0.0, q).astype(BF16)
                kb_ref[dst, :] = k.astype(BF16)
                vt = v.T.astype(BF16)
                for h in range(2):
                    vt_ref[h, gb, :HEAD_DIM, :] = vt[h * HEAD_DIM:(h + 1) * HEAD_DIM]
                    vt_ref[h, gb, HEAD_DIM:, :] = ones
            return 0

        lax.fori_loop(0, n_blocks // DIL_GROUP, gather, 0)

        def band_block(gb, use_bound, per_residue=per_residue, g=g):
            prev = jnp.maximum(gb - 1, 0)
            first = jnp.where(gb % per_residue == 0, NEG, 0.0)
            kb = jnp.concatenate([kb_ref[own_rows(prev), :], kb_ref[own_rows(gb), :]], axis=0)

            def scores(bias):
                q_both = jnp.concatenate([qs_ref[0, own_rows(gb), :], qs_ref[1, own_rows(gb), :]], axis=0)
                s = _mm(kb, q_both, _NT) + bias
                return jnp.concatenate([s[:BAND] + first, s[BAND:]], axis=0)

            if use_bound:
                m = jnp.concatenate([bound_row, bound_row], axis=1)
                s = scores(band_bias_less_bound)
                yield
                p = jnp.exp(s)
            else:
                m = jnp.max(scores(band_bias), axis=0, keepdims=True)
                yield
                s = scores(band_bias)
                yield
                p = jnp.exp(s - m)
            vt_both = jnp.concatenate([jnp.concatenate([vt_ref[h, prev], vt_ref[h, gb]], axis=1)
                                       for h in range(2)], axis=0)
            pv = _mm(vt_both, p.astype(BF16))
            yield
            pv = [pv[h * vt_rows:(h + 1) * vt_rows, h * BAND:(h + 1) * BAND] for h in range(2)]
            l = [x[HEAD_DIM:HEAD_DIM + 1] for x in pv]
            o_t = jnp.concatenate([pv[h][:HEAD_DIM] / l[h] for h in range(2)], axis=0)
            lse_t = jnp.concatenate([jnp.broadcast_to(m[:, h * BAND:(h + 1) * BAND] + jnp.log(l[h]),
                                                      (HEAD_DIM, BAND)) for h in range(2)], axis=0)
            out_rows = own_rows(gb) if g <= 1 else rows_in_previous_level(g, gb)
            og_ref[g, out_rows, :] = o_t.T
            lse_ref[g, out_rows, :] = lse_t.T

        def group(i, _, use_bound, band_block=band_block):
            _round_robin([band_block(DIL_GROUP * i + j, use_bound) for j in range(DIL_GROUP)])
            return 0

        for use_bound in (True, False):
            @pl.when(bound_is_safe == use_bound)
            def _(use_bound=use_bound, group=group):
                lax.fori_loop(0, n_blocks // DIL_GROUP, functools.partial(group, use_bound=use_bound), 0)

    segment_len = seq // DILATIONS[1]

    def mix(i, _):
        rows = pl.ds(pl.multiple_of(i * tile, tile), tile)
        start = i * tile
        tokens = pl.ds(start // segment_len + DIL_RATIO * (start % segment_len), tile, stride=DIL_RATIO)
        where = (tokens, rows, rows)
        lse = [lse_ref[g, where[g], :] for g in range(len(DILATIONS))]
        top = functools.reduce(jnp.maximum, lse)
        w = [jnp.exp(x - top) for x in lse]
        o_ref[0, tokens, :] = sum(w[g] * og_ref[g, where[g], :] for g in range(len(DILATIONS))) / sum(w)
        return 0

    lax.fori_loop(0, seq // tile, mix, 0)


def _dilated(pc, gq, gk):
    bsz, seq, _ = pc.shape
    n_pairs = C_HEADS // 2
    n_pat = len(DILATIONS)
    ones_rows = 16
    slab = lambda off: pl.BlockSpec((1, seq, LANES), lambda b, j: (b, 0, off + j))
    gain = pl.BlockSpec((1, LANES), lambda b, j: (0, 0))
    return pl.pallas_call(
        _dilated_kernel,
        grid=(bsz, n_pairs),
        in_specs=[slab(0), slab(n_pairs), slab(2 * n_pairs), gain, gain],
        out_specs=pl.BlockSpec((1, seq, LANES), lambda b, j: (b, 0, j)),
        out_shape=jax.ShapeDtypeStruct((bsz, seq, C_WIDTH), F32),
        scratch_shapes=[pltpu.VMEM((seq, LANES), F32)] * 6 + [
                        pltpu.VMEM((2, seq, LANES), BF16), pltpu.VMEM((seq, LANES), BF16),
                        pltpu.VMEM((2, seq // BAND, HEAD_DIM + ones_rows, BAND), BF16),
                        pltpu.VMEM((n_pat, seq, LANES), F32), pltpu.VMEM((n_pat, seq, LANES), F32)],
        compiler_params=_params(2),
        name="dilated",
    )(pc, pc, pc, jnp.tile(gq, 2).reshape(1, LANES), jnp.tile(gk, 2).reshape(1, LANES))


def kernel(x, norm_g, w_in, w_out, tshift_mu, decay_w0, decay_up, iclr_a0, iclr_up,
           k_k, k_a, r_k, lnx_g, lnx_b, moba_q_g, moba_k_g, dil_q_g, dil_k_g):
    bsz, seq, d_model = x.shape
    depth = norm_g.shape[0]
    assert seq % (max(DILATIONS) * BAND) == 0 and seq % MOBA_BLOCK == 0 and seq % RWKV_CHUNK == 0
    x2d = x.reshape(bsz * seq, d_model)
    w_in_bf = w_in.astype(BF16)
    w_out_bf = w_out.astype(BF16)
    pa, pb, pc, gate = _proj(x2d, in_args=(norm_g[0], w_in_bf[0]))
    for l in range(depth):
        pp, wz = _rwkv_pack_params(decay_w0[l], decay_up[l], iclr_a0[l], iclr_up[l],
                                   k_k[l], k_a[l], r_k[l], lnx_g[l], lnx_b[l])
        ya = _rwkv(pa.reshape(bsz, seq, A_PROJ), tshift_mu[l], pp, wz)
        yb = _moba(pb.reshape(bsz, seq, B_PROJ), moba_q_g[l], moba_k_g[l])
        yc = _dilated(pc.reshape(bsz, seq, C_PROJ), dil_q_g[l], dil_k_g[l])
        out_args = (ya.reshape(bsz * seq, A_WIDTH), yb.reshape(bsz * seq, B_WIDTH),
                    yc.reshape(bsz * seq, C_WIDTH), gate, w_out_bf[l])
        if l + 1 < depth:
            x2d, pa, pb, pc, gate = _proj(x2d, out_args, (norm_g[l + 1], w_in_bf[l + 1]))
        else:
            (x2d,) = _proj(x2d, out_args)
    return x2d.reshape(bsz, seq, d_model)
```

```python
import functools

import jax
import jax.numpy as jnp
from jax import lax
from jax.experimental import pallas as pl
from jax.experimental.pallas import tpu as pltpu

F32 = jnp.float32
BF16 = jnp.bfloat16

HEAD_DIM = 64
LANES = 128
A_HEADS, B_HEADS, C_HEADS = 6, 4, 6
A_WIDTH, B_WIDTH, C_WIDTH = A_HEADS * HEAD_DIM, B_HEADS * HEAD_DIM, C_HEADS * HEAD_DIM
LORA = 64
A_PROJ = 3 * A_WIDTH + 2 * LORA
B_PROJ = 3 * B_WIDTH
C_PROJ = 3 * C_WIDTH
MIX_WIDTH = A_WIDTH + B_WIDTH + C_WIDTH
PROJ_WIDTH = A_PROJ + B_PROJ + C_PROJ + MIX_WIDTH
MOBA_BLOCK = 256
MOBA_TOPK = 3
MOBA_TILES = 4
DILATIONS = (1, 4, 16)
DIL_RATIO = 4
BAND = 128
DIL_GROUP = 16
RMS_EPS = 1e-6
LNX_EPS = HEAD_DIM * 1e-5
ATTN_SCALE = HEAD_DIM ** -0.5
RWKV_CHUNK = 64
NEG = -1e30
MAX_SAFE_BOUND = 40.0
VMEM_LIMIT = 56 * 1024 * 1024


def _params(n_axes):
    return pltpu.CompilerParams(dimension_semantics=("arbitrary",) * n_axes,
                                vmem_limit_bytes=VMEM_LIMIT)


_NN = (((1,), (0,)), ((), ()))
_NT = (((1,), (1,)), ((), ()))
_TN = (((0,), (0,)), ((), ()))


def _mm(a, b, dims=_NN):
    return lax.dot_general(a, b, dims, preferred_element_type=F32)


def _split(x):
    hi = x.astype(BF16)
    lo = (x - hi.astype(F32)).astype(BF16)
    return hi, lo


def _mm3(a, b, dims=_NN):
    ah, al = _split(a)
    bh, bl = _split(b)
    return _mm(ah, bh, dims) + (_mm(ah, bl, dims) + _mm(al, bh, dims))


def _mm1(a, b, dims=_NN):
    return _mm(a.astype(BF16), b.astype(BF16), dims)


def _half_sum(x, is_a):
    sa = jnp.sum(jnp.where(is_a, x, 0.0), axis=-1, keepdims=True)
    sb = jnp.sum(jnp.where(is_a, 0.0, x), axis=-1, keepdims=True)
    return jnp.where(is_a, sa, sb)


def _half_sum_mxu(x):
    row = lax.broadcasted_iota(jnp.int32, (LANES, LANES), 0) // HEAD_DIM
    col = lax.broadcasted_iota(jnp.int32, (LANES, LANES), 1) // HEAD_DIM
    ones_bd = (row == col).astype(BF16)
    hi, lo = _split(x)
    return _mm(hi, ones_bd) + _mm(lo, ones_bd)


def _round_robin(chains):
    results = [None] * len(chains)
    live = list(range(len(chains)))
    while live:
        for i in list(live):
            try:
                next(chains[i])
            except StopIteration as done:
                results[i] = done.value
                live.remove(i)
    return results


def _lane_is_a(shape):
    return lax.broadcasted_iota(jnp.int32, shape, len(shape) - 1) < HEAD_DIM


_PROJ_WIDTHS = (A_PROJ, B_PROJ, C_PROJ, MIX_WIDTH)
_PROJ_DTYPES = (F32, BF16, BF16, BF16)


def _proj_kernel(*refs, has_out, has_in):
    refs = list(refs)
    x = refs.pop(0)[...]
    if has_out:
        ya_ref, yb_ref, yc_ref, gt_ref, w_out_ref = (refs.pop(0) for _ in range(5))
    if has_in:
        g_ref, w_in_ref = refs.pop(0), refs.pop(0)
    if has_out:
        lo = 0
        for y_ref in (ya_ref, yb_ref, yc_ref):
            hi = lo + y_ref.shape[-1]
            g = gt_ref[:, lo:hi].astype(F32)
            y = y_ref[...] * (g * jax.nn.sigmoid(g))
            x = x + _mm(y.astype(BF16), w_out_ref[lo:hi, :])
            lo = hi
        refs.pop(0)[...] = x
    if has_in:
        ms = jnp.mean(x * x, axis=-1, keepdims=True)
        h = (x * lax.rsqrt(ms + RMS_EPS) * g_ref[...]).astype(BF16)
        lo = 0
        for ref in refs:
            hi = lo + ref.shape[-1]
            ref[...] = _mm(h, w_in_ref[:, lo:hi]).astype(ref.dtype)
            lo = hi


def _proj(x2d, out_args=None, in_args=None, tm=512):
    m, d = x2d.shape
    row = lambda w: pl.BlockSpec((tm, w), lambda i: (i, 0))
    whole = lambda shape: pl.BlockSpec(shape, lambda i: (0, 0))
    args, in_specs, out_specs, out_shape = [x2d], [row(d)], [], []
    if out_args is not None:
        args += list(out_args)
        in_specs += [row(A_WIDTH), row(B_WIDTH), row(C_WIDTH), row(MIX_WIDTH), whole((MIX_WIDTH, d))]
        out_specs.append(row(d))
        out_shape.append(jax.ShapeDtypeStruct((m, d), F32))
    if in_args is not None:
        gain, w_in = in_args
        args += [gain.reshape(1, d), w_in]
        in_specs += [whole((1, d)), whole((d, PROJ_WIDTH))]
        out_specs += [row(w) for w in _PROJ_WIDTHS]
        out_shape += [jax.ShapeDtypeStruct((m, w), dt) for w, dt in zip(_PROJ_WIDTHS, _PROJ_DTYPES)]
    return pl.pallas_call(
        functools.partial(_proj_kernel, has_out=out_args is not None, has_in=in_args is not None),
        grid=(m // tm,),
        in_specs=in_specs, out_specs=out_specs, out_shape=out_shape,
        compiler_params=_params(1),
        name="proj",
    )(*args)


_P_W0, _P_A0, _P_KK, _P_KA, _P_RK, _P_LNG, _P_LNB = range(7)
_P_ROWS = 8


def _rwkv_kernel(pa_ref, mu_ref, pp_ref, wz_ref, o_ref, st_ref, prev_ref):
    n_rows, tile = pa_ref.shape[0], pa_ref.shape[1]
    c_len = RWKV_CHUNK
    two_c = 2 * c_len
    n_chunks = tile // c_len
    n_pairs = A_HEADS // 2
    is_a = _lane_is_a((c_len, LANES))

    row_c = lax.broadcasted_iota(jnp.int32, (c_len, A_PROJ), 0)
    ri = lax.broadcasted_iota(jnp.int32, (c_len, c_len), 0)
    ci = lax.broadcasted_iota(jnp.int32, (c_len, c_len), 1)
    tril_c = (ri >= ci).astype(BF16)
    r2 = lax.broadcasted_iota(jnp.int32, (c_len, two_c), 0)
    c2 = lax.broadcasted_iota(jnp.int32, (c_len, two_c), 1) % c_len
    m_strict = r2 > c2
    m_incl = r2 >= c2
    eye2 = (r2 == c2).astype(F32)

    def stack(x):
        return jnp.concatenate([jnp.where(is_a, x, 0.0), jnp.where(is_a, 0.0, x)], axis=0)

    @pl.when(pl.program_id(1) == 0)
    def _():
        st_ref[...] = jnp.zeros_like(st_ref)
        prev_ref[...] = jnp.zeros_like(prev_ref)

    def pair_chunk(b, j, r, k, v, lora_w, lora_a):
        pp = pp_ref[j]
        prow = lambda i: pp[i:i + 1, :]
        w0, a0, k_k, k_a, r_k = prow(_P_W0), prow(_P_A0), prow(_P_KK), prow(_P_KA), prow(_P_RK)
        ln_g, ln_b = prow(_P_LNG), prow(_P_LNB)
        w = -jax.nn.softplus(-(w0 + lora_w)) - 0.5
        lw = -jnp.exp(w)
        a = jax.nn.sigmoid(a0 + lora_a)
        kk = k * k_k
        kk = kk * lax.rsqrt(_half_sum(kk * kk, is_a) + 1e-12)
        k2 = k * (1.0 + (a - 1.0) * k_a)
        kka = kk * a

        l1 = lw.astype(BF16)
        rem = lw - l1.astype(F32)
        l2 = rem.astype(BF16)
        l3 = (rem - l2.astype(F32)).astype(BF16)
        g = _mm(tril_c, l1) + (_mm(tril_c, l2) + _mm(tril_c, l3))
        yield
        g_end = g[c_len - 1:c_len, :]
        e_pos = jnp.exp(g)
        e_neg = jnp.exp(-g)
        e_prev = jnp.exp(g - lw)
        e_tail = jnp.exp(g_end - g)

        ab = -kk * e_prev
        rb = r * e_pos
        ab2 = stack(ab)
        bt2 = stack(kka * e_neg)
        kt2 = stack(k2 * e_neg)
        bp2 = stack(kka * e_tail)
        kp2 = stack(k2 * e_tail)
        v2 = stack(v)

        mm = _mm1(jnp.concatenate([ab, rb], axis=0), jnp.concatenate([bt2, kt2], axis=0), _NT)
        yield
        l_b = jnp.where(m_strict, mm[:c_len, :two_c], 0.0)
        l_k = jnp.where(m_strict, mm[:c_len, two_c:], 0.0)
        r_b = jnp.where(m_incl, mm[c_len:, :two_c], 0.0)
        r_k2 = jnp.where(m_incl, mm[c_len:, two_c:], 0.0)

        t_inv = eye2 + l_b
        p = _mm1(l_b, stack(l_b))
        kv = _mm1(l_k, v2)
        yield
        steps = c_len.bit_length() - 2
        for i in range(steps):
            if i + 1 < steps:
                tp = _mm1(jnp.concatenate([t_inv, p], axis=0), stack(p))
                yield
                t_inv = t_inv + tp[:c_len]
                p = tp[c_len:]
            else:
                tp = _mm1(t_inv, stack(p))
                yield
                t_inv = t_inv + tp

        tw = _mm1(t_inv, jnp.concatenate([ab2, stack(kv)], axis=1))
        yield
        sv = st_ref[b, j]
        ws = _mm1(jnp.concatenate([tw[:, :LANES], rb], axis=0), sv, _NT)
        yield
        uv = jnp.concatenate([stack(ws[:c_len] + tw[:, LANES:]), v2], axis=0)
        y = ws[c_len:] + _mm1(jnp.concatenate([r_b, r_k2], axis=1), uv)
        st_ref[b, j] = sv * jnp.exp(g_end) + _mm1(uv, jnp.concatenate([bp2, kp2], axis=0), _TN)
        yield

        mean = _half_sum(y, is_a) * (1.0 / HEAD_DIM)
        yc = y - mean
        var = _half_sum(yc * yc, is_a) * (1.0 / HEAD_DIM)
        y = yc * lax.rsqrt(var + LNX_EPS) * ln_g + ln_b
        return y + _half_sum(r * k2 * r_k, is_a) * v

    mu = mu_ref[...]
    wz = wz_ref[...]

    def body(c, prev_rows):
        t0 = pl.multiple_of(c * c_len, c_len)
        chains, last_rows = [], []
        for b in range(n_rows):
            x = pa_ref[b, pl.ds(t0, c_len), :]
            prev = jnp.where(row_c == 0, prev_rows[b], pltpu.roll(x, 1, axis=0))
            xs = x + (prev - x) * mu
            slab = lambda i, xs=xs: xs[:, i * LANES:(i + 1) * LANES]
            z = slab(3 * n_pairs)
            lora = _mm3(jnp.where(is_a, jnp.tanh(z), z), wz)
            chains += [pair_chunk(b, j, slab(j), slab(n_pairs + j), slab(2 * n_pairs + j),
                                  lora[:, 2 * j * LANES:(2 * j + 1) * LANES],
                                  lora[:, (2 * j + 1) * LANES:(2 * j + 2) * LANES])
                       for j in range(n_pairs)]
            last_rows.append(x[c_len - 1:c_len, :])
        for i, y in enumerate(_round_robin(chains)):
            b, j = divmod(i, n_pairs)
            o_ref[b, pl.ds(t0, c_len), j * LANES:(j + 1) * LANES] = y
        return tuple(last_rows)

    last = lax.fori_loop(0, n_chunks, body, tuple(prev_ref[b, 0:1, :] for b in range(n_rows)))
    for b in range(n_rows):
        prev_ref[b, 0:1, :] = last[b]


def _rwkv(pa, mu, pp, wz, tile=512):
    bsz, seq, _ = pa.shape
    rows = next(r for r in (4, 2, 1) if bsz % r == 0)
    n_pairs = A_HEADS // 2
    whole = lambda shape: pl.BlockSpec(shape, lambda b, s: (0,) * len(shape))
    return pl.pallas_call(
        _rwkv_kernel,
        grid=(bsz // rows, seq // tile),
        in_specs=[pl.BlockSpec((rows, tile, A_PROJ), lambda b, s: (b, s, 0)),
                  whole((1, A_PROJ)), whole((n_pairs, _P_ROWS, LANES)), whole((LANES, n_pairs * 2 * LANES))],
        out_specs=pl.BlockSpec((rows, tile, A_WIDTH), lambda b, s: (b, s, 0)),
        out_shape=jax.ShapeDtypeStruct((bsz, seq, A_WIDTH), F32),
        scratch_shapes=[pltpu.VMEM((rows, n_pairs, LANES, LANES), F32), pltpu.VMEM((rows, 8, A_PROJ), F32)],
        compiler_params=_params(2),
        name="rwkv",
    )(pa, mu.reshape(1, A_PROJ), pp, wz)


def _rwkv_pack_params(w0, w_up, a0, a_up, k_k, k_a, r_k, ln_g, ln_b):
    n_pairs = A_HEADS // 2
    pair = lambda t, j: t[j * LANES:(j + 1) * LANES]
    pps, wzs = [], []
    zeros = jnp.zeros((LORA, LANES), F32)
    for j in range(n_pairs):
        rows = {_P_W0: w0, _P_A0: a0, _P_KK: k_k, _P_KA: k_a, _P_RK: r_k.reshape(-1), _P_LNG: ln_g, _P_LNB: ln_b}
        pps.append(jnp.stack([pair(rows[i], j) if i in rows else jnp.zeros((LANES,), F32)
                              for i in range(_P_ROWS)]))
        wzs.append(jnp.concatenate([w_up[:, j * LANES:(j + 1) * LANES], zeros], axis=0))
        wzs.append(jnp.concatenate([zeros, a_up[:, j * LANES:(j + 1) * LANES]], axis=0))
    return jnp.stack(pps), jnp.concatenate(wzs, axis=1)


def _qk_norm(x, gain):
    ms = _half_sum_mxu(x * x) * (1.0 / HEAD_DIM)
    return x * lax.rsqrt(ms + RMS_EPS) * gain


def _score_bound(gq, gk):
    return (HEAD_DIM * ATTN_SCALE) * jnp.max(jnp.abs(gq)) * jnp.max(jnp.abs(gk))


def _moba_kernel(q_ref, k_ref, v_ref, gq_ref, gk_ref, o_ref,
                 qn_ref, kb_ref, qs_ref, km_ref, vt_ref, bias_ref):
    seq = q_ref.shape[1]
    blk = MOBA_BLOCK
    nb = seq // blk
    ones_rows = vt_ref.shape[2] - HEAD_DIM
    is_a = _lane_is_a((blk, LANES))
    gq, gk = gq_ref[...], gk_ref[...]

    def prepare(i, _):
        for n in (2 * i, 2 * i + 1):
            rows = pl.ds(pl.multiple_of(n * blk, blk), blk)
            qn = _qk_norm(q_ref[0, rows, :].astype(F32), gq)
            qn_ref[rows, :] = qn
            qs = qn * ATTN_SCALE
            qs_ref[0, rows, :] = jnp.where(is_a, qs, 0.0).astype(BF16)
            qs_ref[1, rows, :] = jnp.where(is_a, 0.0, qs).astype(BF16)
            kn = _qk_norm(k_ref[0, rows, :].astype(F32), gk)
            kb_ref[rows, :] = kn.astype(BF16)
            km = jnp.mean(kn, axis=0, keepdims=True)
            km_ref[0, pl.ds(n, 1), :] = jnp.where(is_a[:1], km, 0.0)
            km_ref[1, pl.ds(n, 1), :] = jnp.where(is_a[:1], 0.0, km)
            vt = v_ref[0, rows, :].astype(F32).T.astype(BF16)
            ones = jnp.ones((ones_rows, blk), BF16)
            for h in range(2):
                vt_ref[h, n, :HEAD_DIM, :] = vt[h * HEAD_DIM:(h + 1) * HEAD_DIM]
                vt_ref[h, n, HEAD_DIM:, :] = ones
        return 0

    lax.fori_loop(0, nb // 2, prepare, 0)

    bound = _score_bound(gq, gk)
    bound_is_safe = bound <= MAX_SAFE_BOUND
    bound_row = jnp.full((1, blk), bound, F32)
    blk_row = lax.broadcasted_iota(jnp.int32, (nb, blk), 0)
    key_pos = lax.broadcasted_iota(jnp.int32, (blk, blk), 0)
    qry_pos = lax.broadcasted_iota(jnp.int32, (blk, blk), 1)
    causal = key_pos <= qry_pos

    n_tiles = MOBA_TILES
    combos = [(t, h) for t in range(n_tiles) for h in range(2)]

    def select_blocks(i, slot):
        for t, h in combos:
            qt = n_tiles * i + t
            gate = _mm3(km_ref[h], qn_ref[pl.ds(pl.multiple_of(qt * blk, blk), blk), :], _NT)
            gate = jnp.where(blk_row < qt, gate, -jnp.inf)
            bias = jnp.full((nb, blk), NEG, F32)
            for _ in range(MOBA_TOPK):
                top = jnp.max(gate, axis=0, keepdims=True)
                hit = (gate == top) & (top > -jnp.inf)
                first = jnp.min(jnp.where(hit, blk_row, nb), axis=0, keepdims=True)
                pick = blk_row == first
                bias = jnp.where(pick, 0.0, bias)
                gate = jnp.where(pick, -jnp.inf, gate)
            bias_ref[slot, t, h] = bias

    select_blocks(0, 0)

    def q_tile_group(i, _):
        tiles = [n_tiles * i + t for t in range(n_tiles)]
        rows = [pl.ds(pl.multiple_of(qt * blk, blk), blk) for qt in tiles]
        slot = i % 2

        qs = {(t, h): qs_ref[h, rows[t], :] for t, h in combos}

        def scores(n, t, h, own, minus=0.0):
            s = _mm(kb_ref[pl.ds(pl.multiple_of(n * blk, blk), blk), :], qs[t, h], _NT)
            if own:
                return jnp.where(causal, s - minus, NEG)
            return s + (bias_ref[slot, t, h, pl.ds(n, 1), :] - minus)

        tail = [(tiles[n], t, n == t) for t in range(n_tiles) for n in range(t + 1)]
        n_past_pairs = (n_tiles // 2) * i

        def exact_max():
            def past_pair(j, m):
                s = {(k, t, h): scores(2 * j + k, t, h, False) for k in range(2) for t, h in combos}
                return {(t, h): jnp.maximum(m[t, h], jnp.max(jnp.maximum(s[0, t, h], s[1, t, h]),
                                                                 axis=0, keepdims=True)) for t, h in combos}

            m = {(t, h): jnp.full((1, blk), NEG, F32) for t, h in combos}
            for n, t, own in tail:
                for h in range(2):
                    m[t, h] = jnp.maximum(m[t, h], jnp.max(scores(n, t, h, own), axis=0, keepdims=True))
            return lax.fori_loop(0, n_past_pairs, past_pair, m)

        m = lax.cond(bound_is_safe, lambda: {c: bound_row for c in combos}, exact_max)

        def weighted_v(s, n, h):
            return _mm(vt_ref[h, n], jnp.exp(s).astype(BF16))

        def past_pair(j, pv):
            s = {(k, t, h): scores(2 * j + k, t, h, False, m[t, h]) for k in range(2) for t, h in combos}
            return {(t, h): pv[t, h] + weighted_v(s[0, t, h], 2 * j, h) + weighted_v(s[1, t, h], 2 * j + 1, h)
                    for t, h in combos}

        select_blocks(jnp.minimum(i + 1, nb // n_tiles - 1), 1 - slot)
        s = {(k, h): scores(n, t, h, own, m[t, h]) for k, (n, t, own) in enumerate(tail) for h in range(2)}
        pv = {(t, h): sum(weighted_v(s[k, h], n, h) for k, (n, tt, _) in enumerate(tail) if tt == t)
              for t, h in combos}
        pv = lax.fori_loop(0, n_past_pairs, past_pair, pv)
        for t in range(n_tiles):
            o_t = jnp.concatenate([pv[t, h][:HEAD_DIM] / pv[t, h][HEAD_DIM:HEAD_DIM + 1] for h in range(2)],
                                  axis=0)
            o_ref[0, rows[t], :] = o_t.T
        return 0

    lax.fori_loop(0, nb // n_tiles, q_tile_group, 0)


def _moba(pb, gq, gk):
    bsz, seq, _ = pb.shape
    n_pairs = B_HEADS // 2
    nb = seq // MOBA_BLOCK
    ones_rows = 16
    slab = lambda off: pl.BlockSpec((1, seq, LANES), lambda b, j: (b, 0, off + j))
    gain = pl.BlockSpec((1, LANES), lambda b, j: (0, 0))
    return pl.pallas_call(
        _moba_kernel,
        grid=(bsz, n_pairs),
        in_specs=[slab(0), slab(n_pairs), slab(2 * n_pairs), gain, gain],
        out_specs=pl.BlockSpec((1, seq, LANES), lambda b, j: (b, 0, j)),
        out_shape=jax.ShapeDtypeStruct((bsz, seq, B_WIDTH), F32),
        scratch_shapes=[pltpu.VMEM((seq, LANES), F32), pltpu.VMEM((seq, LANES), BF16),
                        pltpu.VMEM((2, seq, LANES), BF16), pltpu.VMEM((2, nb, LANES), F32),
                        pltpu.VMEM((2, nb, HEAD_DIM + ones_rows, MOBA_BLOCK), BF16),
                        pltpu.VMEM((2, MOBA_TILES, 2, nb, MOBA_BLOCK), F32)],
        compiler_params=_params(2),
        name="moba",
    )(pb, pb, pb, jnp.tile(gq, 2).reshape(1, LANES), jnp.tile(gk, 2).reshape(1, LANES))


def _dilated_kernel(q_ref, k_ref, v_ref, gq_ref, gk_ref, o_ref,
                    qn_ref, kn_ref, vv_ref, q1_ref, k1_ref, v1_ref, qs_ref, kb_ref, vt_ref, og_ref, lse_ref):
    seq = q_ref.shape[1]
    n_blocks = seq // BAND
    ones_rows = vt_ref.shape[2] - HEAD_DIM
    tile = 512

    def normalise(i, _):
        rows = pl.ds(pl.multiple_of(i * tile, tile), tile)
        qn_ref[rows, :] = _qk_norm(q_ref[0, rows, :].astype(F32), gq_ref[...] * ATTN_SCALE)
        kn_ref[rows, :] = _qk_norm(k_ref[0, rows, :].astype(F32), gk_ref[...])
        vv_ref[rows, :] = v_ref[0, rows, :].astype(F32)
        return 0

    lax.fori_loop(0, seq // tile, normalise, 0)
    bound = _score_bound(gq_ref[...], gk_ref[...])
    bound_is_safe = bound <= MAX_SAFE_BOUND
    bound_row = jnp.full((1, BAND), bound, F32)

    is_a = _lane_is_a((BAND, LANES))
    key_j = lax.broadcasted_iota(jnp.int32, (2 * BAND, BAND), 0)
    qry_i = lax.broadcasted_iota(jnp.int32, (2 * BAND, BAND), 1)
    band_bias = jnp.where((key_j >= qry_i) & (key_j <= qry_i + BAND), 0.0, NEG)
    band_bias = jnp.concatenate([band_bias, band_bias], axis=1)
    band_bias_less_bound = band_bias - bound
    ones = jnp.ones((ones_rows, BAND), BF16)
    vt_rows = HEAD_DIM + ones_rows

    def own_rows(gb):
        return pl.ds(pl.multiple_of(gb * BAND, BAND), BAND)

    def rows_in_previous_level(g, gb):
        per_residue = n_blocks // DILATIONS[g]
        segment, c = gb // per_residue, gb % per_residue
        start = (segment // DIL_RATIO) * (seq // DILATIONS[g - 1]) + segment % DIL_RATIO
        return pl.ds(start + c * (BAND * DIL_RATIO), BAND, stride=DIL_RATIO)

    for g, dil in enumerate(DILATIONS):
        per_residue = n_blocks // dil
        sources = (qn_ref, kn_ref, vv_ref) if g <= 1 else (q1_ref, k1_ref, v1_ref)

        def gather(i, _, g=g, sources=sources):
            for gb in [DIL_GROUP * i + j for j in range(DIL_GROUP)]:
                rows = own_rows(gb) if g == 0 else rows_in_previous_level(g, gb)
                dst = own_rows(gb)
                q, k, v = (ref[rows, :] for ref in sources)
                if g == 1:
                    q1_ref[dst, :], k1_ref[dst, :], v1_ref[dst, :] = q, k, v
                qs_ref[0, dst, :] = jnp.where(is_a, q, 0.0).astype(BF16)
                qs_ref[1, dst, :] = jnp.where(is_a, 0.0, q).astype(BF16)
                kb_ref[dst, :] = k.astype(BF16)
                vt = v.T.astype(BF16)
                for h in range(2):
                    vt_ref[h, gb, :HEAD_DIM, :] = vt[h * HEAD_DIM:(h + 1) * HEAD_DIM]
                    vt_ref[h, gb, HEAD_DIM:, :] = ones
            return 0

        lax.fori_loop(0, n_blocks // DIL_GROUP, gather, 0)

        def band_block(gb, use_bound, per_residue=per_residue, g=g):
            prev = jnp.maximum(gb - 1, 0)
            first = jnp.where(gb % per_residue == 0, NEG, 0.0)
            kb = jnp.concatenate([kb_ref[own_rows(prev), :], kb_ref[own_rows(gb), :]], axis=0)

            def scores(bias):
                q_both = jnp.concatenate([qs_ref[0, own_rows(gb), :], qs_ref[1, own_rows(gb), :]], axis=0)
                s = _mm(kb, q_both, _NT) + bias
                return jnp.concatenate([s[:BAND] + first, s[BAND:]], axis=0)

            if use_bound:
                m = jnp.concatenate([bound_row, bound_row], axis=1)
                s = scores(band_bias_less_bound)
                yield
                p = jnp.exp(s)
            else:
                m = jnp.max(scores(band_bias), axis=0, keepdims=True)
                yield
                s = scores(band_bias)
                yield
                p = jnp.exp(s - m)
            vt_both = jnp.concatenate([jnp.concatenate([vt_ref[h, prev], vt_ref[h, gb]], axis=1)
                                       for h in range(2)], axis=0)
            pv = _mm(vt_both, p.astype(BF16))
            yield
            pv = [pv[h * vt_rows:(h + 1) * vt_rows, h * BAND:(h + 1) * BAND] for h in range(2)]
            l = [x[HEAD_DIM:HEAD_DIM + 1] for x in pv]
            o_t = jnp.concatenate([pv[h][:HEAD_DIM] / l[h] for h in range(2)], axis=0)
            lse_t = jnp.concatenate([jnp.broadcast_to(m[:, h * BAND:(h + 1) * BAND] + jnp.log(l[h]),
                                                      (HEAD_DIM, BAND)) for h in range(2)], axis=0)
            out_rows = own_rows(gb) if g <= 1 else rows_in_previous_level(g, gb)
            og_ref[g, out_rows, :] = o_t.T
            lse_ref[g, out_rows, :] = lse_t.T

        def group(i, _, use_bound, band_block=band_block):
            _round_robin([band_block(DIL_GROUP * i + j, use_bound) for j in range(DIL_GROUP)])
            return 0

        for use_bound in (True, False):
            @pl.when(bound_is_safe == use_bound)
            def _(use_bound=use_bound, group=group):
                lax.fori_loop(0, n_blocks // DIL_GROUP, functools.partial(group, use_bound=use_bound), 0)

    segment_len = seq // DILATIONS[1]

    def mix(i, _):
        rows = pl.ds(pl.multiple_of(i * tile, tile), tile)
        start = i * tile
        tokens = pl.ds(start // segment_len + DIL_RATIO * (start % segment_len), tile, stride=DIL_RATIO)
        where = (tokens, rows, rows)
        lse = [lse_ref[g, where[g], :] for g in range(len(DILATIONS))]
        top = functools.reduce(jnp.maximum, lse)
        w = [jnp.exp(x - top) for x in lse]
        o_ref[0, tokens, :] = sum(w[g] * og_ref[g, where[g], :] for g in range(len(DILATIONS))) / sum(w)
        return 0

    lax.fori_loop(0, seq // tile, mix, 0)


def _dilated(pc, gq, gk):
    bsz, seq, _ = pc.shape
    n_pairs = C_HEADS // 2
    n_pat = len(DILATIONS)
    ones_rows = 16
    slab = lambda off: pl.BlockSpec((1, seq, LANES), lambda b, j: (b, 0, off + j))
    gain = pl.BlockSpec((1, LANES), lambda b, j: (0, 0))
    return pl.pallas_call(
        _dilated_kernel,
        grid=(bsz, n_pairs),
        in_specs=[slab(0), slab(n_pairs), slab(2 * n_pairs), gain, gain],
        out_specs=pl.BlockSpec((1, seq, LANES), lambda b, j: (b, 0, j)),
        out_shape=jax.ShapeDtypeStruct((bsz, seq, C_WIDTH), F32),
        scratch_shapes=[pltpu.VMEM((seq, LANES), F32)] * 6 + [
                        pltpu.VMEM((2, seq, LANES), BF16), pltpu.VMEM((seq, LANES), BF16),
                        pltpu.VMEM((2, seq // BAND, HEAD_DIM + ones_rows, BAND), BF16),
                        pltpu.VMEM((n_pat, seq, LANES), F32), pltpu.VMEM((n_pat, seq, LANES), F32)],
        compiler_params=_params(2),
        name="dilated",
    )(pc, pc, pc, jnp.tile(gq, 2).reshape(1, LANES), jnp.tile(gk, 2).reshape(1, LANES))


def kernel(x, norm_g, w_in, w_out, tshift_mu, decay_w0, decay_up, iclr_a0, iclr_up,
           k_k, k_a, r_k, lnx_g, lnx_b, moba_q_g, moba_k_g, dil_q_g, dil_k_g):
    bsz, seq, d_model = x.shape
    depth = norm_g.shape[0]
    assert seq % (max(DILATIONS) * BAND) == 0 and seq % MOBA_BLOCK == 0 and seq % RWKV_CHUNK == 0
    x2d = x.reshape(bsz * seq, d_model)
    w_in_bf = w_in.astype(BF16)
    w_out_bf = w_out.astype(BF16)
    pa, pb, pc, gate = _proj(x2d, in_args=(norm_g[0], w_in_bf[0]))
    for l in range(depth):
        pp, wz = _rwkv_pack_params(decay_w0[l], decay_up[l], iclr_a0[l], iclr_up[l],
                                   k_k[l], k_a[l], r_k[l], lnx_g[l], lnx_b[l])
        ya = _rwkv(pa.reshape(bsz, seq, A_PROJ), tshift_mu[l], pp, wz)
        yb = _moba(pb.reshape(bsz, seq, B_PROJ), moba_q_g[l], moba_k_g[l])
        yc = _dilated(pc.reshape(bsz, seq, C_PROJ), dil_q_g[l], dil_k_g[l])
        out_args = (ya.reshape(bsz * seq, A_WIDTH), yb.reshape(bsz * seq, B_WIDTH),
                    yc.reshape(bsz * seq, C_WIDTH), gate, w_out_bf[l])
        if l + 1 < depth:
            x2d, pa, pb, pc, gate = _proj(x2d, out_args, (norm_g[l + 1], w_in_bf[l + 1]))
        else:
            (x2d,) = _proj(x2d, out_args)
    return x2d.reshape(bsz, seq, d_model)
```

```python
import functools

import jax
import jax.numpy as jnp
from jax import lax
from jax.experimental import pallas as pl
from jax.experimental.pallas import tpu as pltpu

F32 = jnp.float32
BF16 = jnp.bfloat16

HEAD_DIM = 64
LANES = 128
A_HEADS, B_HEADS, C_HEADS = 6, 4, 6
A_WIDTH, B_WIDTH, C_WIDTH = A_HEADS * HEAD_DIM, B_HEADS * HEAD_DIM, C_HEADS * HEAD_DIM
LORA = 64
A_PROJ = 3 * A_WIDTH + 2 * LORA
B_PROJ = 3 * B_WIDTH
C_PROJ = 3 * C_WIDTH
MIX_WIDTH = A_WIDTH + B_WIDTH + C_WIDTH
PROJ_WIDTH = A_PROJ + B_PROJ + C_PROJ + MIX_WIDTH
MOBA_BLOCK = 256
MOBA_TOPK = 3
MOBA_TILES = 4
DILATIONS = (1, 4, 16)
DIL_RATIO = 4
BAND = 128
DIL_GROUP = 16
RMS_EPS = 1e-6
LNX_EPS = HEAD_DIM * 1e-5
ATTN_SCALE = HEAD_DIM ** -0.5
RWKV_CHUNK = 64
NEG = -1e30
MAX_SAFE_BOUND = 40.0
VMEM_LIMIT = 56 * 1024 * 1024


def _params(n_axes):
    return pltpu.CompilerParams(dimension_semantics=("arbitrary",) * n_axes,
                                vmem_limit_bytes=VMEM_LIMIT)


_NN = (((1,), (0,)), ((), ()))
_NT = (((1,), (1,)), ((), ()))
_TN = (((0,), (0,)), ((), ()))


def _mm(a, b, dims=_NN):
    return lax.dot_general(a, b, dims, preferred_element_type=F32)


def _split(x):
    hi = x.astype(BF16)
    lo = (x - hi.astype(F32)).astype(BF16)
    return hi, lo


def _mm3(a, b, dims=_NN):
    ah, al = _split(a)
    bh, bl = _split(b)
    return _mm(ah, bh, dims) + (_mm(ah, bl, dims) + _mm(al, bh, dims))


def _mm1(a, b, dims=_NN):
    return _mm(a.astype(BF16), b.astype(BF16), dims)


def _half_sum(x, is_a):
    sa = jnp.sum(jnp.where(is_a, x, 0.0), axis=-1, keepdims=True)
    sb = jnp.sum(jnp.where(is_a, 0.0, x), axis=-1, keepdims=True)
    return jnp.where(is_a, sa, sb)


def _half_sum_mxu(x):
    row = lax.broadcasted_iota(jnp.int32, (LANES, LANES), 0) // HEAD_DIM
    col = lax.broadcasted_iota(jnp.int32, (LANES, LANES), 1) // HEAD_DIM
    ones_bd = (row == col).astype(BF16)
    hi, lo = _split(x)
    return _mm(hi, ones_bd) + _mm(lo, ones_bd)


def _round_robin(chains):
    results = [None] * len(chains)
    live = list(range(len(chains)))
    while live:
        for i in list(live):
            try:
                next(chains[i])
            except StopIteration as done:
                results[i] = done.value
                live.remove(i)
    return results


def _lane_is_a(shape):
    return lax.broadcasted_iota(jnp.int32, shape, len(shape) - 1) < HEAD_DIM


_PROJ_WIDTHS = (A_PROJ, B_PROJ, C_PROJ, MIX_WIDTH)
_PROJ_DTYPES = (F32, BF16, BF16, BF16)


def _proj_kernel(*refs, has_out, has_in):
    refs = list(refs)
    x = refs.pop(0)[...]
    if has_out:
        ya_ref, yb_ref, yc_ref, gt_ref, w_out_ref = (refs.pop(0) for _ in range(5))
    if has_in:
        g_ref, w_in_ref = refs.pop(0), refs.pop(0)
    if has_out:
        lo = 0
        for y_ref in (ya_ref, yb_ref, yc_ref):
            hi = lo + y_ref.shape[-1]
            g = gt_ref[:, lo:hi].astype(F32)
            y = y_ref[...] * (g * jax.nn.sigmoid(g))
            x = x + _mm(y.astype(BF16), w_out_ref[lo:hi, :])
            lo = hi
        refs.pop(0)[...] = x
    if has_in:
        ms = jnp.mean(x * x, axis=-1, keepdims=True)
        h = (x * lax.rsqrt(ms + RMS_EPS) * g_ref[...]).astype(BF16)
        lo = 0
        for ref in refs:
            hi = lo + ref.shape[-1]
            ref[...] = _mm(h, w_in_ref[:, lo:hi]).astype(ref.dtype)
            lo = hi


def _proj(x2d, out_args=None, in_args=None, tm=512):
    m, d = x2d.shape
    row = lambda w: pl.BlockSpec((tm, w), lambda i: (i, 0))
    whole = lambda shape: pl.BlockSpec(shape, lambda i: (0, 0))
    args, in_specs, out_specs, out_shape = [x2d], [row(d)], [], []
    if out_args is not None:
        args += list(out_args)
        in_specs += [row(A_WIDTH), row(B_WIDTH), row(C_WIDTH), row(MIX_WIDTH), whole((MIX_WIDTH, d))]
        out_specs.append(row(d))
        out_shape.append(jax.ShapeDtypeStruct((m, d), F32))
    if in_args is not None:
        gain, w_in = in_args
        args += [gain.reshape(1, d), w_in]
        in_specs += [whole((1, d)), whole((d, PROJ_WIDTH))]
        out_specs += [row(w) for w in _PROJ_WIDTHS]
        out_shape += [jax.ShapeDtypeStruct((m, w), dt) for w, dt in zip(_PROJ_WIDTHS, _PROJ_DTYPES)]
    return pl.pallas_call(
        functools.partial(_proj_kernel, has_out=out_args is not None, has_in=in_args is not None),
        grid=(m // tm,),
        in_specs=in_specs, out_specs=out_specs, out_shape=out_shape,
        compiler_params=_params(1),
        name="proj",
    )(*args)


_P_W0, _P_A0, _P_KK, _P_KA, _P_RK, _P_LNG, _P_LNB = range(7)
_P_ROWS = 8


def _rwkv_kernel(pa_ref, mu_ref, pp_ref, wz_ref, o_ref, st_ref, prev_ref):
    n_rows, tile = pa_ref.shape[0], pa_ref.shape[1]
    c_len = RWKV_CHUNK
    two_c = 2 * c_len
    n_chunks = tile // c_len
    n_pairs = A_HEADS // 2
    is_a = _lane_is_a((c_len, LANES))

    row_c = lax.broadcasted_iota(jnp.int32, (c_len, A_PROJ), 0)
    ri = lax.broadcasted_iota(jnp.int32, (c_len, c_len), 0)
    ci = lax.broadcasted_iota(jnp.int32, (c_len, c_len), 1)
    tril_c = (ri >= ci).astype(BF16)
    r2 = lax.broadcasted_iota(jnp.int32, (c_len, two_c), 0)
    c2 = lax.broadcasted_iota(jnp.int32, (c_len, two_c), 1) % c_len
    m_strict = r2 > c2
    m_incl = r2 >= c2
    eye2 = (r2 == c2).astype(F32)

    def stack(x):
        return jnp.concatenate([jnp.where(is_a, x, 0.0), jnp.where(is_a, 0.0, x)], axis=0)

    @pl.when(pl.program_id(1) == 0)
    def _():
        st_ref[...] = jnp.zeros_like(st_ref)
        prev_ref[...] = jnp.zeros_like(prev_ref)

    def pair_chunk(b, j, r, k, v, lora_w, lora_a):
        pp = pp_ref[j]
        prow = lambda i: pp[i:i + 1, :]
        w0, a0, k_k, k_a, r_k = prow(_P_W0), prow(_P_A0), prow(_P_KK), prow(_P_KA), prow(_P_RK)
        ln_g, ln_b = prow(_P_LNG), prow(_P_LNB)
        w = -jax.nn.softplus(-(w0 + lora_w)) - 0.5
        lw = -jnp.exp(w)
        a = jax.nn.sigmoid(a0 + lora_a)
        kk = k * k_k
        kk = kk * lax.rsqrt(_half_sum(kk * kk, is_a) + 1e-12)
        k2 = k * (1.0 + (a - 1.0) * k_a)
        kka = kk * a

        l1 = lw.astype(BF16)
        rem = lw - l1.astype(F32)
        l2 = rem.astype(BF16)
        l3 = (rem - l2.astype(F32)).astype(BF16)
        g = _mm(tril_c, l1) + (_mm(tril_c, l2) + _mm(tril_c, l3))
        yield
        g_end = g[c_len - 1:c_len, :]
        e_pos = jnp.exp(g)
        e_neg = jnp.exp(-g)
        e_prev = jnp.exp(g - lw)
        e_tail = jnp.exp(g_end - g)

        ab = -kk * e_prev
        rb = r * e_pos
        ab2 = stack(ab)
        bt2 = stack(kka * e_neg)
        kt2 = stack(k2 * e_neg)
        bp2 = stack(kka * e_tail)
        kp2 = stack(k2 * e_tail)
        v2 = stack(v)

        mm = _mm1(jnp.concatenate([ab, rb], axis=0), jnp.concatenate([bt2, kt2], axis=0), _NT)
        yield
        l_b = jnp.where(m_strict, mm[:c_len, :two_c], 0.0)
        l_k = jnp.where(m_strict, mm[:c_len, two_c:], 0.0)
        r_b = jnp.where(m_incl, mm[c_len:, :two_c], 0.0)
        r_k2 = jnp.where(m_incl, mm[c_len:, two_c:], 0.0)

        t_inv = eye2 + l_b
        p = _mm1(l_b, stack(l_b))
        kv = _mm1(l_k, v2)
        yield
        steps = c_len.bit_length() - 2
        for i in range(steps):
            if i + 1 < steps:
                tp = _mm1(jnp.concatenate([t_inv, p], axis=0), stack(p))
                yield
                t_inv = t_inv + tp[:c_len]
                p = tp[c_len:]
            else:
                tp = _mm1(t_inv, stack(p))
                yield
                t_inv = t_inv + tp

        tw = _mm1(t_inv, jnp.concatenate([ab2, stack(kv)], axis=1))
        yield
        sv = st_ref[b, j]
        ws = _mm1(jnp.concatenate([tw[:, :LANES], rb], axis=0), sv, _NT)
        yield
        uv = jnp.concatenate([stack(ws[:c_len] + tw[:, LANES:]), v2], axis=0)
        y = ws[c_len:] + _mm1(jnp.concatenate([r_b, r_k2], axis=1), uv)
        st_ref[b, j] = sv * jnp.exp(g_end) + _mm1(uv, jnp.concatenate([bp2, kp2], axis=0), _TN)
        yield

        mean = _half_sum(y, is_a) * (1.0 / HEAD_DIM)
        yc = y - mean
        var = _half_sum(yc * yc, is_a) * (1.0 / HEAD_DIM)
        y = yc * lax.rsqrt(var + LNX_EPS) * ln_g + ln_b
        return y + _half_sum(r * k2 * r_k, is_a) * v

    mu = mu_ref[...]
    wz = wz_ref[...]

    def body(c, prev_rows):
        t0 = pl.multiple_of(c * c_len, c_len)
        chains, last_rows = [], []
        for b in range(n_rows):
            x = pa_ref[b, pl.ds(t0, c_len), :]
            prev = jnp.where(row_c == 0, prev_rows[b], pltpu.roll(x, 1, axis=0))
            xs = x + (prev - x) * mu
            slab = lambda i, xs=xs: xs[:, i * LANES:(i + 1) * LANES]
            z = slab(3 * n_pairs)
            lora = _mm3(jnp.where(is_a, jnp.tanh(z), z), wz)
            chains += [pair_chunk(b, j, slab(j), slab(n_pairs + j), slab(2 * n_pairs + j),
                                  lora[:, 2 * j * LANES:(2 * j + 1) * LANES],
                                  lora[:, (2 * j + 1) * LANES:(2 * j + 2) * LANES])
                       for j in range(n_pairs)]
            last_rows.append(x[c_len - 1:c_len, :])
        for i, y in enumerate(_round_robin(chains)):
            b, j = divmod(i, n_pairs)
            o_ref[b, pl.ds(t0, c_len), j * LANES:(j + 1) * LANES] = y
        return tuple(last_rows)

    last = lax.fori_loop(0, n_chunks, body, tuple(prev_ref[b, 0:1, :] for b in range(n_rows)), unroll=2)
    for b in range(n_rows):
        prev_ref[b, 0:1, :] = last[b]


def _rwkv(pa, mu, pp, wz, tile=512):
    bsz, seq, _ = pa.shape
    rows = next(r for r in (4, 2, 1) if bsz % r == 0)
    n_pairs = A_HEADS // 2
    whole = lambda shape: pl.BlockSpec(shape, lambda b, s: (0,) * len(shape))
    return pl.pallas_call(
        _rwkv_kernel,
        grid=(bsz // rows, seq // tile),
        in_specs=[pl.BlockSpec((rows, tile, A_PROJ), lambda b, s: (b, s, 0)),
                  whole((1, A_PROJ)), whole((n_pairs, _P_ROWS, LANES)), whole((LANES, n_pairs * 2 * LANES))],
        out_specs=pl.BlockSpec((rows, tile, A_WIDTH), lambda b, s: (b, s, 0)),
        out_shape=jax.ShapeDtypeStruct((bsz, seq, A_WIDTH), F32),
        scratch_shapes=[pltpu.VMEM((rows, n_pairs, LANES, LANES), F32), pltpu.VMEM((rows, 8, A_PROJ), F32)],
        compiler_params=_params(2),
        name="rwkv",
    )(pa, mu.reshape(1, A_PROJ), pp, wz)


def _rwkv_pack_params(w0, w_up, a0, a_up, k_k, k_a, r_k, ln_g, ln_b):
    n_pairs = A_HEADS // 2
    pair = lambda t, j: t[j * LANES:(j + 1) * LANES]
    pps, wzs = [], []
    zeros = jnp.zeros((LORA, LANES), F32)
    for j in range(n_pairs):
        rows = {_P_W0: w0, _P_A0: a0, _P_KK: k_k, _P_KA: k_a, _P_RK: r_k.reshape(-1), _P_LNG: ln_g, _P_LNB: ln_b}
        pps.append(jnp.stack([pair(rows[i], j) if i in rows else jnp.zeros((LANES,), F32)
                              for i in range(_P_ROWS)]))
        wzs.append(jnp.concatenate([w_up[:, j * LANES:(j + 1) * LANES], zeros], axis=0))
        wzs.append(jnp.concatenate([zeros, a_up[:, j * LANES:(j + 1) * LANES]], axis=0))
    return jnp.stack(pps), jnp.concatenate(wzs, axis=1)


def _qk_norm(x, gain):
    ms = _half_sum_mxu(x * x) * (1.0 / HEAD_DIM)
    return x * lax.rsqrt(ms + RMS_EPS) * gain


def _score_bound(gq, gk):
    return (HEAD_DIM * ATTN_SCALE) * jnp.max(jnp.abs(gq)) * jnp.max(jnp.abs(gk))


def _moba_kernel(q_ref, k_ref, v_ref, gq_ref, gk_ref, o_ref,
                 qn_ref, kb_ref, qs_ref, km_ref, vt_ref, bias_ref):
    seq = q_ref.shape[1]
    blk = MOBA_BLOCK
    nb = seq // blk
    ones_rows = vt_ref.shape[2] - HEAD_DIM
    is_a = _lane_is_a((blk, LANES))
    gq, gk = gq_ref[...], gk_ref[...]

    def prepare(i, _):
        for n in (2 * i, 2 * i + 1):
            rows = pl.ds(pl.multiple_of(n * blk, blk), blk)
            qn = _qk_norm(q_ref[0, rows, :].astype(F32), gq)
            qn_ref[rows, :] = qn
            qs = qn * ATTN_SCALE
            qs_ref[0, rows, :] = jnp.where(is_a, qs, 0.0).astype(BF16)
            qs_ref[1, rows, :] = jnp.where(is_a, 0.0, qs).astype(BF16)
            kn = _qk_norm(k_ref[0, rows, :].astype(F32), gk)
            kb_ref[rows, :] = kn.astype(BF16)
            km = jnp.mean(kn, axis=0, keepdims=True)
            km_ref[0, pl.ds(n, 1), :] = jnp.where(is_a[:1], km, 0.0)
            km_ref[1, pl.ds(n, 1), :] = jnp.where(is_a[:1], 0.0, km)
            vt = v_ref[0, rows, :].astype(F32).T.astype(BF16)
            ones = jnp.ones((ones_rows, blk), BF16)
            for h in range(2):
                vt_ref[h, n, :HEAD_DIM, :] = vt[h * HEAD_DIM:(h + 1) * HEAD_DIM]
                vt_ref[h, n, HEAD_DIM:, :] = ones
        return 0

    lax.fori_loop(0, nb // 2, prepare, 0)

    bound = _score_bound(gq, gk)
    bound_is_safe = bound <= MAX_SAFE_BOUND
    bound_row = jnp.full((1, blk), bound, F32)
    blk_row = lax.broadcasted_iota(jnp.int32, (nb, blk), 0)
    key_pos = lax.broadcasted_iota(jnp.int32, (blk, blk), 0)
    qry_pos = lax.broadcasted_iota(jnp.int32, (blk, blk), 1)
    causal = key_pos <= qry_pos

    n_tiles = MOBA_TILES
    combos = [(t, h) for t in range(n_tiles) for h in range(2)]

    def select_blocks(i, slot):
        for t, h in combos:
            qt = n_tiles * i + t
            gate = _mm3(km_ref[h], qn_ref[pl.ds(pl.multiple_of(qt * blk, blk), blk), :], _NT)
            gate = jnp.where(blk_row < qt, gate, -jnp.inf)
            bias = jnp.full((nb, blk), NEG, F32)
            for _ in range(MOBA_TOPK):
                top = jnp.max(gate, axis=0, keepdims=True)
                hit = (gate == top) & (top > -jnp.inf)
                first = jnp.min(jnp.where(hit, blk_row, nb), axis=0, keepdims=True)
                pick = blk_row == first
                bias = jnp.where(pick, 0.0, bias)
                gate = jnp.where(pick, -jnp.inf, gate)
            bias_ref[slot, t, h] = bias

    select_blocks(0, 0)

    def q_tile_group(i, _):
        tiles = [n_tiles * i + t for t in range(n_tiles)]
        rows = [pl.ds(pl.multiple_of(qt * blk, blk), blk) for qt in tiles]
        slot = i % 2

        qs = {(t, h): qs_ref[h, rows[t], :] for t, h in combos}

        def scores(n, t, h, own, minus=0.0):
            s = _mm(kb_ref[pl.ds(pl.multiple_of(n * blk, blk), blk), :], qs[t, h], _NT)
            if own:
                return jnp.where(causal, s - minus, NEG)
            return s + (bias_ref[slot, t, h, pl.ds(n, 1), :] - minus)

        tail = [(tiles[n], t, n == t) for t in range(n_tiles) for n in range(t + 1)]
        n_past_pairs = (n_tiles // 2) * i

        def exact_max():
            def past_pair(j, m):
                s = {(k, t, h): scores(2 * j + k, t, h, False) for k in range(2) for t, h in combos}
                return {(t, h): jnp.maximum(m[t, h], jnp.max(jnp.maximum(s[0, t, h], s[1, t, h]),
                                                                 axis=0, keepdims=True)) for t, h in combos}

            m = {(t, h): jnp.full((1, blk), NEG, F32) for t, h in combos}
            for n, t, own in tail:
                for h in range(2):
                    m[t, h] = jnp.maximum(m[t, h], jnp.max(scores(n, t, h, own), axis=0, keepdims=True))
            return lax.fori_loop(0, n_past_pairs, past_pair, m)

        m = lax.cond(bound_is_safe, lambda: {c: bound_row for c in combos}, exact_max)

        def weighted_v(s, n, h):
            return _mm(vt_ref[h, n], jnp.exp(s).astype(BF16))

        def past_pair(j, pv):
            s = {(k, t, h): scores(2 * j + k, t, h, False, m[t, h]) for k in range(2) for t, h in combos}
            return {(t, h): pv[t, h] + weighted_v(s[0, t, h], 2 * j, h) + weighted_v(s[1, t, h], 2 * j + 1, h)
                    for t, h in combos}

        select_blocks(jnp.minimum(i + 1, nb // n_tiles - 1), 1 - slot)
        s = {(k, h): scores(n, t, h, own, m[t, h]) for k, (n, t, own) in enumerate(tail) for h in range(2)}
        pv = {(t, h): sum(weighted_v(s[k, h], n, h) for k, (n, tt, _) in enumerate(tail) if tt == t)
              for t, h in combos}
        pv = lax.fori_loop(0, n_past_pairs, past_pair, pv)
        for t in range(n_tiles):
            o_t = jnp.concatenate([pv[t, h][:HEAD_DIM] / pv[t, h][HEAD_DIM:HEAD_DIM + 1] for h in range(2)],
                                  axis=0)
            o_ref[0, rows[t], :] = o_t.T
        return 0

    lax.fori_loop(0, nb // n_tiles, q_tile_group, 0)


def _moba(pb, gq, gk):
    bsz, seq, _ = pb.shape
    n_pairs = B_HEADS // 2
    nb = seq // MOBA_BLOCK
    ones_rows = 16
    slab = lambda off: pl.BlockSpec((1, seq, LANES), lambda b, j: (b, 0, off + j))
    gain = pl.BlockSpec((1, LANES), lambda b, j: (0, 0))
    return pl.pallas_call(
        _moba_kernel,
        grid=(bsz, n_pairs),
        in_specs=[slab(0), slab(n_pairs), slab(2 * n_pairs), gain, gain],
        out_specs=pl.BlockSpec((1, seq, LANES), lambda b, j: (b, 0, j)),
        out_shape=jax.ShapeDtypeStruct((bsz, seq, B_WIDTH), F32),
        scratch_shapes=[pltpu.VMEM((seq, LANES), F32), pltpu.VMEM((seq, LANES), BF16),
                        pltpu.VMEM((2, seq, LANES), BF16), pltpu.VMEM((2, nb, LANES), F32),
                        pltpu.VMEM((2, nb, HEAD_DIM + ones_rows, MOBA_BLOCK), BF16),
                        pltpu.VMEM((2, MOBA_TILES, 2, nb, MOBA_BLOCK), F32)],
        compiler_params=_params(2),
        name="moba",
    )(pb, pb, pb, jnp.tile(gq, 2).reshape(1, LANES), jnp.tile(gk, 2).reshape(1, LANES))


def _dilated_kernel(q_ref, k_ref, v_ref, gq_ref, gk_ref, o_ref,
                    qn_ref, kn_ref, vv_ref, q1_ref, k1_ref, v1_ref, qs_ref, kb_ref, vt_ref, og_ref, lse_ref):
    seq = q_ref.shape[1]
    n_blocks = seq // BAND
    ones_rows = vt_ref.shape[2] - HEAD_DIM
    tile = 512

    def normalise(i, _):
        rows = pl.ds(pl.multiple_of(i * tile, tile), tile)
        qn_ref[rows, :] = _qk_norm(q_ref[0, rows, :].astype(F32), gq_ref[...] * ATTN_SCALE)
        kn_ref[rows, :] = _qk_norm(k_ref[0, rows, :].astype(F32), gk_ref[...])
        vv_ref[rows, :] = v_ref[0, rows, :].astype(F32)
        return 0

    lax.fori_loop(0, seq // tile, normalise, 0)
    bound = _score_bound(gq_ref[...], gk_ref[...])
    bound_is_safe = bound <= MAX_SAFE_BOUND
    bound_row = jnp.full((1, BAND), bound, F32)

    is_a = _lane_is_a((BAND, LANES))
    key_j = lax.broadcasted_iota(jnp.int32, (2 * BAND, BAND), 0)
    qry_i = lax.broadcasted_iota(jnp.int32, (2 * BAND, BAND), 1)
    band_bias = jnp.where((key_j >= qry_i) & (key_j <= qry_i + BAND), 0.0, NEG)
    band_bias = jnp.concatenate([band_bias, band_bias], axis=1)
    band_bias_less_bound = band_bias - bound
    ones = jnp.ones((ones_rows, BAND), BF16)
    vt_rows = HEAD_DIM + ones_rows

    def own_rows(gb):
        return pl.ds(pl.multiple_of(gb * BAND, BAND), BAND)

    def rows_in_previous_level(g, gb):
        per_residue = n_blocks // DILATIONS[g]
        segment, c = gb // per_residue, gb % per_residue
        start = (segment // DIL_RATIO) * (seq // DILATIONS[g - 1]) + segment % DIL_RATIO
        return pl.ds(start + c * (BAND * DIL_RATIO), BAND, stride=DIL_RATIO)

    for g, dil in enumerate(DILATIONS):
        per_residue = n_blocks // dil
        sources = (qn_ref, kn_ref, vv_ref) if g <= 1 else (q1_ref, k1_ref, v1_ref)

        def gather(i, _, g=g, sources=sources):
            for gb in [DIL_GROUP * i + j for j in range(DIL_GROUP)]:
                rows = own_rows(gb) if g == 0 else rows_in_previous_level(g, gb)
                dst = own_rows(gb)
                q, k, v = (ref[rows, :] for ref in sources)
                if g == 1:
                    q1_ref[dst, :], k1_ref[dst, :], v1_ref[dst, :] = q, k, v
                qs_ref[0, dst, :] = jnp.where(is_a, q, 0.0).astype(BF16)
                qs_ref[1, dst, :] = jnp.where(is_a, 0.0, q).astype(BF16)
                kb_ref[dst, :] = k.astype(BF16)
                vt = v.T.astype(BF16)
                for h in range(2):
                    vt_ref[h, gb, :HEAD_DIM, :] = vt[h * HEAD_DIM:(h + 1) * HEAD_DIM]
                    vt_ref[h, gb, HEAD_DIM:, :] = ones
            return 0

        lax.fori_loop(0, n_blocks // DIL_GROUP, gather, 0)

        def band_block(gb, use_bound, per_residue=per_residue, g=g):
            prev = jnp.maximum(gb - 1, 0)
            first = jnp.where(gb % per_residue == 0, NEG, 0.0)
            kb = jnp.concatenate([kb_ref[own_rows(prev), :], kb_ref[own_rows(gb), :]], axis=0)

            def scores(bias):
                q_both = jnp.concatenate([qs_ref[0, own_rows(gb), :], qs_ref[1, own_rows(gb), :]], axis=0)
                s = _mm(kb, q_both, _NT) + bias
                return jnp.concatenate([s[:BAND] + first, s[BAND:]], axis=0)

            if use_bound:
                m = jnp.concatenate([bound_row, bound_row], axis=1)
                s = scores(band_bias_less_bound)
                yield
                p = jnp.exp(s)
            else:
                m = jnp.max(scores(band_bias), axis=0, keepdims=True)
                yield
                s = scores(band_bias)
                yield
                p = jnp.exp(s - m)
            vt_both = jnp.concatenate([jnp.concatenate([vt_ref[h, prev], vt_ref[h, gb]], axis=1)
                                       for h in range(2)], axis=0)
            pv = _mm(vt_both, p.astype(BF16))
            yield
            pv = [pv[h * vt_rows:(h + 1) * vt_rows, h * BAND:(h + 1) * BAND] for h in range(2)]
            l = [x[HEAD_DIM:HEAD_DIM + 1] for x in pv]
            o_t = jnp.concatenate([pv[h][:HEAD_DIM] / l[h] for h in range(2)], axis=0)
            lse_t = jnp.concatenate([jnp.broadcast_to(m[:, h * BAND:(h + 1) * BAND] + jnp.log(l[h]),
                                                      (HEAD_DIM, BAND)) for h in range(2)], axis=0)
            out_rows = own_rows(gb) if g <= 1 else rows_in_previous_level(g, gb)
            og_ref[g, out_rows, :] = o_t.T
            lse_ref[g, out_rows, :] = lse_t.T

        def group(i, _, use_bound, band_block=band_block):
            _round_robin([band_block(DIL_GROUP * i + j, use_bound) for j in range(DIL_GROUP)])
            return 0

        for use_bound in (True, False):
            @pl.when(bound_is_safe == use_bound)
            def _(use_bound=use_bound, group=group):
                lax.fori_loop(0, n_blocks // DIL_GROUP, functools.partial(group, use_bound=use_bound), 0)

    segment_len = seq // DILATIONS[1]

    def mix(i, _):
        rows = pl.ds(pl.multiple_of(i * tile, tile), tile)
        start = i * tile
        tokens = pl.ds(start // segment_len + DIL_RATIO * (start % segment_len), tile, stride=DIL_RATIO)
        where = (tokens, rows, rows)
        lse = [lse_ref[g, where[g], :] for g in range(len(DILATIONS))]
        top = functools.reduce(jnp.maximum, lse)
        w = [jnp.exp(x - top) for x in lse]
        o_ref[0, tokens, :] = sum(w[g] * og_ref[g, where[g], :] for g in range(len(DILATIONS))) / sum(w)
        return 0

    lax.fori_loop(0, seq // tile, mix, 0)


def _dilated(pc, gq, gk):
    bsz, seq, _ = pc.shape
    n_pairs = C_HEADS // 2
    n_pat = len(DILATIONS)
    ones_rows = 16
    slab = lambda off: pl.BlockSpec((1, seq, LANES), lambda b, j: (b, 0, off + j))
    gain = pl.BlockSpec((1, LANES), lambda b, j: (0, 0))
    return pl.pallas_call(
        _dilated_kernel,
        grid=(bsz, n_pairs),
        in_specs=[slab(0), slab(n_pairs), slab(2 * n_pairs), gain, gain],
        out_specs=pl.BlockSpec((1, seq, LANES), lambda b, j: (b, 0, j)),
        out_shape=jax.ShapeDtypeStruct((bsz, seq, C_WIDTH), F32),
        scratch_shapes=[pltpu.VMEM((seq, LANES), F32)] * 6 + [
                        pltpu.VMEM((2, seq, LANES), BF16), pltpu.VMEM((seq, LANES), BF16),
                        pltpu.VMEM((2, seq // BAND, HEAD_DIM + ones_rows, BAND), BF16),
                        pltpu.VMEM((n_pat, seq, LANES), F32), pltpu.VMEM((n_pat, seq, LANES), F32)],
        compiler_params=_params(2),
        name="dilated",
    )(pc, pc, pc, jnp.tile(gq, 2).reshape(1, LANES), jnp.tile(gk, 2).reshape(1, LANES))


def kernel(x, norm_g, w_in, w_out, tshift_mu, decay_w0, decay_up, iclr_a0, iclr_up,
           k_k, k_a, r_k, lnx_g, lnx_b, moba_q_g, moba_k_g, dil_q_g, dil_k_g):
    bsz, seq, d_model = x.shape
    depth = norm_g.shape[0]
    assert seq % (max(DILATIONS) * BAND) == 0 and seq % MOBA_BLOCK == 0 and seq % RWKV_CHUNK == 0
    x2d = x.reshape(bsz * seq, d_model)
    w_in_bf = w_in.astype(BF16)
    w_out_bf = w_out.astype(BF16)
    pa, pb, pc, gate = _proj(x2d, in_args=(norm_g[0], w_in_bf[0]))
    for l in range(depth):
        pp, wz = _rwkv_pack_params(decay_w0[l], decay_up[l], iclr_a0[l], iclr_up[l],
                                   k_k[l], k_a[l], r_k[l], lnx_g[l], lnx_b[l])
        ya = _rwkv(pa.reshape(bsz, seq, A_PROJ), tshift_mu[l], pp, wz)
        yb = _moba(pb.reshape(bsz, seq, B_PROJ), moba_q_g[l], moba_k_g[l])
        yc = _dilated(pc.reshape(bsz, seq, C_PROJ), dil_q_g[l], dil_k_g[l])
        out_args = (ya.reshape(bsz * seq, A_WIDTH), yb.reshape(bsz * seq, B_WIDTH),
                    yc.reshape(bsz * seq, C_WIDTH), gate, w_out_bf[l])
        if l + 1 < depth:
            x2d, pa, pb, pc, gate = _proj(x2d, out_args, (norm_g[l + 1], w_in_bf[l + 1]))
        else:
            (x2d,) = _proj(x2d, out_args)
    return x2d.reshape(bsz, seq, d_model)
```

```python
import functools

import jax
import jax.numpy as jnp
from jax import lax
from jax.experimental import pallas as pl
from jax.experimental.pallas import tpu as pltpu

F32 = jnp.float32
BF16 = jnp.bfloat16

HEAD_DIM = 64
LANES = 128
A_HEADS, B_HEADS, C_HEADS = 6, 4, 6
A_WIDTH, B_WIDTH, C_WIDTH = A_HEADS * HEAD_DIM, B_HEADS * HEAD_DIM, C_HEADS * HEAD_DIM
LORA = 64
A_PROJ = 3 * A_WIDTH + 2 * LORA
B_PROJ = 3 * B_WIDTH
C_PROJ = 3 * C_WIDTH
MIX_WIDTH = A_WIDTH + B_WIDTH + C_WIDTH
PROJ_WIDTH = A_PROJ + B_PROJ + C_PROJ + MIX_WIDTH
MOBA_BLOCK = 256
MOBA_TOPK = 3
MOBA_TILES = 4
DILATIONS = (1, 4, 16)
DIL_RATIO = 4
BAND = 128
DIL_GROUP = 16
RMS_EPS = 1e-6
LNX_EPS = HEAD_DIM * 1e-5
ATTN_SCALE = HEAD_DIM ** -0.5
RWKV_CHUNK = 64
NEG = -1e30
MAX_SAFE_BOUND = 40.0
VMEM_LIMIT = 56 * 1024 * 1024
BF16_SUBLANES = 16
ROW_TILE = 512


def _params(n_axes):
    return pltpu.CompilerParams(dimension_semantics=("arbitrary",) * n_axes,
                                vmem_limit_bytes=VMEM_LIMIT)


_NN = (((1,), (0,)), ((), ()))
_NT = (((1,), (1,)), ((), ()))
_TN = (((0,), (0,)), ((), ()))


def _mm(a, b, dims=_NN):
    return lax.dot_general(a, b, dims, preferred_element_type=F32)


def _split(x):
    hi = x.astype(BF16)
    lo = (x - hi.astype(F32)).astype(BF16)
    return hi, lo


def _mm3(a, b, dims=_NN):
    ah, al = _split(a)
    bh, bl = _split(b)
    return _mm(ah, bh, dims) + (_mm(ah, bl, dims) + _mm(al, bh, dims))


def _mm1(a, b, dims=_NN):
    return _mm(a.astype(BF16), b.astype(BF16), dims)


def _half_sum(x, is_a):
    sa = jnp.sum(jnp.where(is_a, x, 0.0), axis=-1, keepdims=True)
    sb = jnp.sum(jnp.where(is_a, 0.0, x), axis=-1, keepdims=True)
    return jnp.where(is_a, sa, sb)


def _half_sum_mxu(x):
    row = lax.broadcasted_iota(jnp.int32, (LANES, LANES), 0) // HEAD_DIM
    col = lax.broadcasted_iota(jnp.int32, (LANES, LANES), 1) // HEAD_DIM
    ones_bd = (row == col).astype(BF16)
    hi, lo = _split(x)
    return _mm(hi, ones_bd) + _mm(lo, ones_bd)


def _round_robin(chains):
    results = [None] * len(chains)
    live = list(range(len(chains)))
    while live:
        for i in list(live):
            try:
                next(chains[i])
            except StopIteration as done:
                results[i] = done.value
                live.remove(i)
    return results


def _lane_is_a(shape):
    return lax.broadcasted_iota(jnp.int32, shape, len(shape) - 1) < HEAD_DIM


_PROJ_WIDTHS = (A_PROJ, B_PROJ, C_PROJ, MIX_WIDTH)
_PROJ_DTYPES = (F32, BF16, BF16, BF16)


def _proj_kernel(*refs, has_out, has_in):
    refs = list(refs)
    x = refs.pop(0)[...]
    if has_out:
        ya_ref, yb_ref, yc_ref, gt_ref, w_out_ref = (refs.pop(0) for _ in range(5))
    if has_in:
        g_ref, w_in_ref = refs.pop(0), refs.pop(0)
    if has_out:
        gated = []
        lo = 0
        for y_ref in (ya_ref, yb_ref, yc_ref):
            hi = lo + y_ref.shape[-1]
            g = gt_ref[:, lo:hi].astype(F32)
            gated.append((y_ref[...] * (g * jax.nn.sigmoid(g))).astype(BF16))
            lo = hi
        half = MIX_WIDTH // 2
        split = half - gated[0].shape[-1]
        x = x + _mm(jnp.concatenate([gated[0], gated[1][:, :split]], axis=1), w_out_ref[:half, :])
        x = x + _mm(jnp.concatenate([gated[1][:, split:], gated[2]], axis=1), w_out_ref[half:, :])
        refs.pop(0)[...] = x
    if has_in:
        ms = jnp.mean(x * x, axis=-1, keepdims=True)
        h = (x * lax.rsqrt(ms + RMS_EPS) * g_ref[...]).astype(BF16)
        lo = 0
        for ref in refs:
            hi = lo + ref.shape[-1]
            ref[...] = _mm(h, w_in_ref[:, lo:hi]).astype(ref.dtype)
            lo = hi


def _proj(x2d, out_args=None, in_args=None, tm=ROW_TILE):
    m, d = x2d.shape
    row = lambda w: pl.BlockSpec((tm, w), lambda i: (i, 0))
    whole = lambda shape: pl.BlockSpec(shape, lambda i: (0, 0))
    args, in_specs, out_specs, out_shape = [x2d], [row(d)], [], []
    if out_args is not None:
        args += list(out_args)
        in_specs += [row(A_WIDTH), row(B_WIDTH), row(C_WIDTH), row(MIX_WIDTH), whole((MIX_WIDTH, d))]
        out_specs.append(row(d))
        out_shape.append(jax.ShapeDtypeStruct((m, d), F32))
    if in_args is not None:
        gain, w_in = in_args
        args += [gain.reshape(1, d), w_in]
        in_specs += [whole((1, d)), whole((d, PROJ_WIDTH))]
        out_specs += [row(w) for w in _PROJ_WIDTHS]
        out_shape += [jax.ShapeDtypeStruct((m, w), dt) for w, dt in zip(_PROJ_WIDTHS, _PROJ_DTYPES)]
    return pl.pallas_call(
        functools.partial(_proj_kernel, has_out=out_args is not None, has_in=in_args is not None),
        grid=(m // tm,),
        in_specs=in_specs, out_specs=out_specs, out_shape=out_shape,
        compiler_params=_params(1),
        name="proj",
    )(*args)


_P_W0, _P_A0, _P_KK, _P_KA, _P_RK, _P_LNG, _P_LNB = range(7)
_P_ROWS = 8


def _rwkv_kernel(pa_ref, mu_ref, pp_ref, wz_ref, o_ref, st_ref, prev_ref):
    n_rows, tile = pa_ref.shape[0], pa_ref.shape[1]
    c_len = RWKV_CHUNK
    two_c = 2 * c_len
    n_chunks = tile // c_len
    n_pairs = A_HEADS // 2
    is_a = _lane_is_a((c_len, LANES))

    row_c = lax.broadcasted_iota(jnp.int32, (c_len, A_PROJ), 0)
    ri = lax.broadcasted_iota(jnp.int32, (c_len, c_len), 0)
    ci = lax.broadcasted_iota(jnp.int32, (c_len, c_len), 1)
    tril_c = (ri >= ci).astype(BF16)
    r2 = lax.broadcasted_iota(jnp.int32, (c_len, two_c), 0)
    c2 = lax.broadcasted_iota(jnp.int32, (c_len, two_c), 1) % c_len
    m_strict = r2 > c2
    m_incl = r2 >= c2
    eye2 = (r2 == c2).astype(F32)

    def stack(x):
        return jnp.concatenate([jnp.where(is_a, x, 0.0), jnp.where(is_a, 0.0, x)], axis=0)

    @pl.when(pl.program_id(1) == 0)
    def _():
        st_ref[...] = jnp.zeros_like(st_ref)
        prev_ref[...] = jnp.zeros_like(prev_ref)

    def pair_chunk(b, j, r, k, v, lora_w, lora_a):
        pp = pp_ref[j]
        prow = lambda i: pp[i:i + 1, :]
        w0, a0, k_k, k_a, r_k = prow(_P_W0), prow(_P_A0), prow(_P_KK), prow(_P_KA), prow(_P_RK)
        ln_g, ln_b = prow(_P_LNG), prow(_P_LNB)
        w = -jax.nn.softplus(-(w0 + lora_w)) - 0.5
        lw = -jnp.exp(w)
        a = jax.nn.sigmoid(a0 + lora_a)
        kk = k * k_k
        kk = kk * lax.rsqrt(_half_sum(kk * kk, is_a) + 1e-12)
        k2 = k * (1.0 + (a - 1.0) * k_a)
        kka = kk * a

        l1 = lw.astype(BF16)
        rem = lw - l1.astype(F32)
        l2 = rem.astype(BF16)
        l3 = (rem - l2.astype(F32)).astype(BF16)
        g = _mm(tril_c, l1) + (_mm(tril_c, l2) + _mm(tril_c, l3))
        yield
        g_end = g[c_len - 1:c_len, :]
        e_pos = jnp.exp(g)
        e_neg = jnp.exp(-g)
        e_prev = jnp.exp(g - lw)
        e_tail = jnp.exp(g_end - g)

        ab = -kk * e_prev
        rb = r * e_pos
        ab2 = stack(ab)
        bt2 = stack(kka * e_neg)
        kt2 = stack(k2 * e_neg)
        bp2 = stack(kka * e_tail)
        kp2 = stack(k2 * e_tail)
        v2 = stack(v)

        mm = _mm1(jnp.concatenate([ab, rb], axis=0), jnp.concatenate([bt2, kt2], axis=0), _NT)
        yield
        l_b = jnp.where(m_strict, mm[:c_len, :two_c], 0.0)
        l_k = jnp.where(m_strict, mm[:c_len, two_c:], 0.0)
        r_b = jnp.where(m_incl, mm[c_len:, :two_c], 0.0)
        r_k2 = jnp.where(m_incl, mm[c_len:, two_c:], 0.0)

        t_inv = eye2 + l_b
        p = _mm1(l_b, stack(l_b))
        kv = _mm1(l_k, v2)
        yield
        steps = c_len.bit_length() - 2
        for i in range(steps):
            if i + 1 < steps:
                tp = _mm1(jnp.concatenate([t_inv, p], axis=0), stack(p))
                yield
                t_inv = t_inv + tp[:c_len]
                p = tp[c_len:]
            else:
                tp = _mm1(t_inv, stack(p))
                yield
                t_inv = t_inv + tp

        tw = _mm1(t_inv, jnp.concatenate([ab2, stack(kv)], axis=1))
        yield
        sv = st_ref[b, j]
        ws = _mm1(jnp.concatenate([tw[:, :LANES], rb], axis=0), sv, _NT)
        yield
        uv = jnp.concatenate([stack(ws[:c_len] + tw[:, LANES:]), v2], axis=0)
        y = ws[c_len:] + _mm1(jnp.concatenate([r_b, r_k2], axis=1), uv)
        st_ref[b, j] = sv * jnp.exp(g_end) + _mm1(uv, jnp.concatenate([bp2, kp2], axis=0), _TN)
        yield

        mean = _half_sum(y, is_a) * (1.0 / HEAD_DIM)
        yc = y - mean
        var = _half_sum(yc * yc, is_a) * (1.0 / HEAD_DIM)
        y = yc * lax.rsqrt(var + LNX_EPS) * ln_g + ln_b
        return y + _half_sum(r * k2 * r_k, is_a) * v

    mu = mu_ref[...]
    wz = wz_ref[...]

    def body(c, prev_rows):
        t0 = pl.multiple_of(c * c_len, c_len)
        chains, last_rows = [], []
        for b in range(n_rows):
            x = pa_ref[b, pl.ds(t0, c_len), :]
            prev = jnp.where(row_c == 0, prev_rows[b], pltpu.roll(x, 1, axis=0))
            xs = x + (prev - x) * mu
            slab = lambda i, xs=xs: xs[:, i * LANES:(i + 1) * LANES]
            z = slab(3 * n_pairs)
            lora = _mm3(jnp.where(is_a, jnp.tanh(z), z), wz)
            chains += [pair_chunk(b, j, slab(j), slab(n_pairs + j), slab(2 * n_pairs + j),
                                  lora[:, 2 * j * LANES:(2 * j + 1) * LANES],
                                  lora[:, (2 * j + 1) * LANES:(2 * j + 2) * LANES])
                       for j in range(n_pairs)]
            last_rows.append(x[c_len - 1:c_len, :])
        for i, y in enumerate(_round_robin(chains)):
            b, j = divmod(i, n_pairs)
            o_ref[b, pl.ds(t0, c_len), j * LANES:(j + 1) * LANES] = y
        return tuple(last_rows)

    last = lax.fori_loop(0, n_chunks, body, tuple(prev_ref[b, 0:1, :] for b in range(n_rows)), unroll=2)
    for b in range(n_rows):
        prev_ref[b, 0:1, :] = last[b]


def _rwkv(pa, mu, pp, wz, tile=ROW_TILE):
    bsz, seq, _ = pa.shape
    rows = next(r for r in (4, 2, 1) if bsz % r == 0)
    n_pairs = A_HEADS // 2
    whole = lambda shape: pl.BlockSpec(shape, lambda b, s: (0,) * len(shape))
    return pl.pallas_call(
        _rwkv_kernel,
        grid=(bsz // rows, seq // tile),
        in_specs=[pl.BlockSpec((rows, tile, A_PROJ), lambda b, s: (b, s, 0)),
                  whole((1, A_PROJ)), whole((n_pairs, _P_ROWS, LANES)), whole((LANES, n_pairs * 2 * LANES))],
        out_specs=pl.BlockSpec((rows, tile, A_WIDTH), lambda b, s: (b, s, 0)),
        out_shape=jax.ShapeDtypeStruct((bsz, seq, A_WIDTH), F32),
        scratch_shapes=[pltpu.VMEM((rows, n_pairs, LANES, LANES), F32), pltpu.VMEM((rows, 8, A_PROJ), F32)],
        compiler_params=_params(2),
        name="rwkv",
    )(pa, mu.reshape(1, A_PROJ), pp, wz)


def _rwkv_pack_params(w0, w_up, a0, a_up, k_k, k_a, r_k, ln_g, ln_b):
    n_pairs = A_HEADS // 2
    pair = lambda t, j: t[j * LANES:(j + 1) * LANES]
    pps, wzs = [], []
    zeros = jnp.zeros((LORA, LANES), F32)
    for j in range(n_pairs):
        rows = {_P_W0: w0, _P_A0: a0, _P_KK: k_k, _P_KA: k_a, _P_RK: r_k.reshape(-1), _P_LNG: ln_g, _P_LNB: ln_b}
        pps.append(jnp.stack([pair(rows[i], j) if i in rows else jnp.zeros((LANES,), F32)
                              for i in range(_P_ROWS)]))
        wzs.append(jnp.concatenate([w_up[:, j * LANES:(j + 1) * LANES], zeros], axis=0))
        wzs.append(jnp.concatenate([zeros, a_up[:, j * LANES:(j + 1) * LANES]], axis=0))
    return jnp.stack(pps), jnp.concatenate(wzs, axis=1)


def _qk_norm(x, gain):
    ms = _half_sum_mxu(x * x) * (1.0 / HEAD_DIM)
    return x * lax.rsqrt(ms + RMS_EPS) * gain


def _score_bound(gq, gk):
    return (HEAD_DIM * ATTN_SCALE) * jnp.max(jnp.abs(gq)) * jnp.max(jnp.abs(gk))


def _moba_kernel(q_ref, k_ref, v_ref, gq_ref, gk_ref, o_ref,
                 qn_ref, kb_ref, qs_ref, km_ref, vt_ref, bias_ref):
    seq = q_ref.shape[1]
    blk = MOBA_BLOCK
    nb = seq // blk
    ones_rows = vt_ref.shape[2] - HEAD_DIM
    is_a = _lane_is_a((blk, LANES))
    gq, gk = gq_ref[...], gk_ref[...]

    def prepare(i, _):
        for n in (2 * i, 2 * i + 1):
            rows = pl.ds(pl.multiple_of(n * blk, blk), blk)
            qn = _qk_norm(q_ref[0, rows, :].astype(F32), gq)
            qn_ref[rows, :] = qn
            qs = qn * ATTN_SCALE
            qs_ref[0, rows, :] = jnp.where(is_a, qs, 0.0).astype(BF16)
            qs_ref[1, rows, :] = jnp.where(is_a, 0.0, qs).astype(BF16)
            kn = _qk_norm(k_ref[0, rows, :].astype(F32), gk)
            kb_ref[rows, :] = kn.astype(BF16)
            km = jnp.mean(kn, axis=0, keepdims=True)
            km_ref[0, pl.ds(n, 1), :] = jnp.where(is_a[:1], km, 0.0)
            km_ref[1, pl.ds(n, 1), :] = jnp.where(is_a[:1], 0.0, km)
            vt = v_ref[0, rows, :].astype(F32).T.astype(BF16)
            ones = jnp.ones((ones_rows, blk), BF16)
            for h in range(2):
                vt_ref[h, n, :HEAD_DIM, :] = vt[h * HEAD_DIM:(h + 1) * HEAD_DIM]
                vt_ref[h, n, HEAD_DIM:, :] = ones
        return 0

    lax.fori_loop(0, nb // 2, prepare, 0)

    bound = _score_bound(gq, gk)
    bound_is_safe = bound <= MAX_SAFE_BOUND
    bound_row = jnp.full((1, blk), bound, F32)
    blk_row = lax.broadcasted_iota(jnp.int32, (nb, blk), 0)
    key_pos = lax.broadcasted_iota(jnp.int32, (blk, blk), 0)
    qry_pos = lax.broadcasted_iota(jnp.int32, (blk, blk), 1)
    causal = key_pos <= qry_pos

    n_tiles = MOBA_TILES
    combos = [(t, h) for t in range(n_tiles) for h in range(2)]

    def select_blocks(i, slot):
        for t, h in combos:
            qt = n_tiles * i + t
            gate = _mm3(km_ref[h], qn_ref[pl.ds(pl.multiple_of(qt * blk, blk), blk), :], _NT)
            gate = jnp.where(blk_row < qt, gate, -jnp.inf)
            bias = jnp.full((nb, blk), NEG, F32)
            for _ in range(MOBA_TOPK):
                top = jnp.max(gate, axis=0, keepdims=True)
                hit = (gate == top) & (top > -jnp.inf)
                first = jnp.min(jnp.where(hit, blk_row, nb), axis=0, keepdims=True)
                pick = blk_row == first
                bias = jnp.where(pick, 0.0, bias)
                gate = jnp.where(pick, -jnp.inf, gate)
            bias_ref[slot, t, h] = bias

    select_blocks(0, 0)

    def q_tile_group(i, _):
        tiles = [n_tiles * i + t for t in range(n_tiles)]
        rows = [pl.ds(pl.multiple_of(qt * blk, blk), blk) for qt in tiles]
        slot = i % 2

        qs = {(t, h): qs_ref[h, rows[t], :] for t, h in combos}

        def scores(n, t, h, own, minus=0.0):
            s = _mm(kb_ref[pl.ds(pl.multiple_of(n * blk, blk), blk), :], qs[t, h], _NT)
            if own:
                return jnp.where(causal, s - minus, NEG)
            return s + (bias_ref[slot, t, h, pl.ds(n, 1), :] - minus)

        tail = [(tiles[n], t, n == t) for t in range(n_tiles) for n in range(t + 1)]
        n_past_pairs = (n_tiles // 2) * i

        def exact_max():
            def past_pair(j, m):
                s = {(k, t, h): scores(2 * j + k, t, h, False) for k in range(2) for t, h in combos}
                return {(t, h): jnp.maximum(m[t, h], jnp.max(jnp.maximum(s[0, t, h], s[1, t, h]),
                                                                 axis=0, keepdims=True)) for t, h in combos}

            m = {(t, h): jnp.full((1, blk), NEG, F32) for t, h in combos}
            for n, t, own in tail:
                for h in range(2):
                    m[t, h] = jnp.maximum(m[t, h], jnp.max(scores(n, t, h, own), axis=0, keepdims=True))
            return lax.fori_loop(0, n_past_pairs, past_pair, m)

        m = lax.cond(bound_is_safe, lambda: {c: bound_row for c in combos}, exact_max)

        def weighted_v(s, n, h):
            return _mm(vt_ref[h, n], jnp.exp(s).astype(BF16))

        def past_pair(j, pv):
            s = {(k, t, h): scores(2 * j + k, t, h, False, m[t, h]) for k in range(2) for t, h in combos}
            return {(t, h): pv[t, h] + weighted_v(s[0, t, h], 2 * j, h) + weighted_v(s[1, t, h], 2 * j + 1, h)
                    for t, h in combos}

        select_blocks(jnp.minimum(i + 1, nb // n_tiles - 1), 1 - slot)
        s = {(k, h): scores(n, t, h, own, m[t, h]) for k, (n, t, own) in enumerate(tail) for h in range(2)}
        pv = {(t, h): sum(weighted_v(s[k, h], n, h) for k, (n, tt, _) in enumerate(tail) if tt == t)
              for t, h in combos}
        pv = lax.fori_loop(0, n_past_pairs, past_pair, pv)
        for t in range(n_tiles):
            o_t = jnp.concatenate([pv[t, h][:HEAD_DIM] / pv[t, h][HEAD_DIM:HEAD_DIM + 1] for h in range(2)],
                                  axis=0)
            o_ref[0, rows[t], :] = o_t.T
        return 0

    lax.fori_loop(0, nb // n_tiles, q_tile_group, 0)


def _moba(pb, gq, gk):
    bsz, seq, _ = pb.shape
    n_pairs = B_HEADS // 2
    nb = seq // MOBA_BLOCK
    ones_rows = BF16_SUBLANES
    slab = lambda off: pl.BlockSpec((1, seq, LANES), lambda b, j: (b, 0, off + j))
    gain = pl.BlockSpec((1, LANES), lambda b, j: (0, 0))
    return pl.pallas_call(
        _moba_kernel,
        grid=(bsz, n_pairs),
        in_specs=[slab(0), slab(n_pairs), slab(2 * n_pairs), gain, gain],
        out_specs=pl.BlockSpec((1, seq, LANES), lambda b, j: (b, 0, j)),
        out_shape=jax.ShapeDtypeStruct((bsz, seq, B_WIDTH), F32),
        scratch_shapes=[pltpu.VMEM((seq, LANES), F32), pltpu.VMEM((seq, LANES), BF16),
                        pltpu.VMEM((2, seq, LANES), BF16), pltpu.VMEM((2, nb, LANES), F32),
                        pltpu.VMEM((2, nb, HEAD_DIM + ones_rows, MOBA_BLOCK), BF16),
                        pltpu.VMEM((2, MOBA_TILES, 2, nb, MOBA_BLOCK), F32)],
        compiler_params=_params(2),
        name="moba",
    )(pb, pb, pb, jnp.tile(gq, 2).reshape(1, LANES), jnp.tile(gk, 2).reshape(1, LANES))


def _dilated_kernel(q_ref, k_ref, v_ref, gq_ref, gk_ref, o_ref,
                    qn_ref, kn_ref, vv_ref, q1_ref, k1_ref, v1_ref, qs_ref, kb_ref, vt_ref, og_ref, lse_ref):
    seq = q_ref.shape[1]
    n_blocks = seq // BAND
    ones_rows = vt_ref.shape[2] - HEAD_DIM
    tile = ROW_TILE

    def normalise(i, _):
        rows = pl.ds(pl.multiple_of(i * tile, tile), tile)
        qn_ref[rows, :] = _qk_norm(q_ref[0, rows, :].astype(F32), gq_ref[...] * ATTN_SCALE)
        kn_ref[rows, :] = _qk_norm(k_ref[0, rows, :].astype(F32), gk_ref[...])
        vv_ref[rows, :] = v_ref[0, rows, :].astype(F32)
        return 0

    lax.fori_loop(0, seq // tile, normalise, 0)
    bound = _score_bound(gq_ref[...], gk_ref[...])
    bound_is_safe = bound <= MAX_SAFE_BOUND
    bound_row = jnp.full((1, BAND), bound, F32)

    is_a = _lane_is_a((BAND, LANES))
    key_j = lax.broadcasted_iota(jnp.int32, (2 * BAND, BAND), 0)
    qry_i = lax.broadcasted_iota(jnp.int32, (2 * BAND, BAND), 1)
    band_bias = jnp.where((key_j >= qry_i) & (key_j <= qry_i + BAND), 0.0, NEG)
    band_bias = jnp.concatenate([band_bias, band_bias], axis=1)
    band_bias_less_bound = band_bias - bound
    ones = jnp.ones((ones_rows, BAND), BF16)
    vt_rows = HEAD_DIM + ones_rows

    def own_rows(gb):
        return pl.ds(pl.multiple_of(gb * BAND, BAND), BAND)

    def rows_in_previous_level(g, gb):
        per_residue = n_blocks // DILATIONS[g]
        segment, c = gb // per_residue, gb % per_residue
        start = (segment // DIL_RATIO) * (seq // DILATIONS[g - 1]) + segment % DIL_RATIO
        return pl.ds(start + c * (BAND * DIL_RATIO), BAND, stride=DIL_RATIO)

    for g, dil in enumerate(DILATIONS):
        per_residue = n_blocks // dil
        sources = (qn_ref, kn_ref, vv_ref) if g <= 1 else (q1_ref, k1_ref, v1_ref)

        def gather(i, _, g=g, sources=sources):
            for gb in [DIL_GROUP * i + j for j in range(DIL_GROUP)]:
                rows = own_rows(gb) if g == 0 else rows_in_previous_level(g, gb)
                dst = own_rows(gb)
                q, k, v = (ref[rows, :] for ref in sources)
                if g == 1:
                    q1_ref[dst, :], k1_ref[dst, :], v1_ref[dst, :] = q, k, v
                qs_ref[0, dst, :] = jnp.where(is_a, q, 0.0).astype(BF16)
                qs_ref[1, dst, :] = jnp.where(is_a, 0.0, q).astype(BF16)
                kb_ref[dst, :] = k.astype(BF16)
                vt = v.T.astype(BF16)
                for h in range(2):
                    vt_ref[h, gb, :HEAD_DIM, :] = vt[h * HEAD_DIM:(h + 1) * HEAD_DIM]
                    vt_ref[h, gb, HEAD_DIM:, :] = ones
            return 0

        lax.fori_loop(0, n_blocks // DIL_GROUP, gather, 0)

        def band_block(gb, use_bound, per_residue=per_residue, g=g):
            prev = jnp.maximum(gb - 1, 0)
            first = jnp.where(gb % per_residue == 0, NEG, 0.0)
            kb = jnp.concatenate([kb_ref[own_rows(prev), :], kb_ref[own_rows(gb), :]], axis=0)

            def scores(bias):
                q_both = jnp.concatenate([qs_ref[0, own_rows(gb), :], qs_ref[1, own_rows(gb), :]], axis=0)
                s = _mm(kb, q_both, _NT) + bias
                return jnp.concatenate([s[:BAND] + first, s[BAND:]], axis=0)

            if use_bound:
                m = jnp.concatenate([bound_row, bound_row], axis=1)
                s = scores(band_bias_less_bound)
                yield
                p = jnp.exp(s)
            else:
                m = jnp.max(scores(band_bias), axis=0, keepdims=True)
                yield
                s = scores(band_bias)
                yield
                p = jnp.exp(s - m)
            vt_both = jnp.concatenate([jnp.concatenate([vt_ref[h, prev], vt_ref[h, gb]], axis=1)
                                       for h in range(2)], axis=0)
            pv = _mm(vt_both, p.astype(BF16))
            yield
            pv = [pv[h * vt_rows:(h + 1) * vt_rows, h * BAND:(h + 1) * BAND] for h in range(2)]
            l = [x[HEAD_DIM:HEAD_DIM + 1] for x in pv]
            o_t = jnp.concatenate([pv[h][:HEAD_DIM] / l[h] for h in range(2)], axis=0)
            lse_t = jnp.concatenate([jnp.broadcast_to(m[:, h * BAND:(h + 1) * BAND] + jnp.log(l[h]),
                                                      (HEAD_DIM, BAND)) for h in range(2)], axis=0)
            out_rows = own_rows(gb) if g <= 1 else rows_in_previous_level(g, gb)
            og_ref[g, out_rows, :] = o_t.T
            lse_ref[g, out_rows, :] = lse_t.T

        def group(i, _, use_bound, band_block=band_block):
            _round_robin([band_block(DIL_GROUP * i + j, use_bound) for j in range(DIL_GROUP)])
            return 0

        for use_bound in (True, False):
            @pl.when(bound_is_safe == use_bound)
            def _(use_bound=use_bound, group=group):
                lax.fori_loop(0, n_blocks // DIL_GROUP, functools.partial(group, use_bound=use_bound), 0)

    segment_len = seq // DILATIONS[1]

    def mix(i, _):
        rows = pl.ds(pl.multiple_of(i * tile, tile), tile)
        start = i * tile
        tokens = pl.ds(start // segment_len + DIL_RATIO * (start % segment_len), tile, stride=DIL_RATIO)
        where = (tokens, rows, rows)
        lse = [lse_ref[g, where[g], :] for g in range(len(DILATIONS))]
        top = functools.reduce(jnp.maximum, lse)
        w = [jnp.exp(x - top) for x in lse]
        o_ref[0, tokens, :] = sum(w[g] * og_ref[g, where[g], :] for g in range(len(DILATIONS))) / sum(w)
        return 0

    lax.fori_loop(0, seq // tile, mix, 0)


def _dilated(pc, gq, gk):
    bsz, seq, _ = pc.shape
    n_pairs = C_HEADS // 2
    n_pat = len(DILATIONS)
    ones_rows = BF16_SUBLANES
    slab = lambda off: pl.BlockSpec((1, seq, LANES), lambda b, j: (b, 0, off + j))
    gain = pl.BlockSpec((1, LANES), lambda b, j: (0, 0))
    return pl.pallas_call(
        _dilated_kernel,
        grid=(bsz, n_pairs),
        in_specs=[slab(0), slab(n_pairs), slab(2 * n_pairs), gain, gain],
        out_specs=pl.BlockSpec((1, seq, LANES), lambda b, j: (b, 0, j)),
        out_shape=jax.ShapeDtypeStruct((bsz, seq, C_WIDTH), F32),
        scratch_shapes=[pltpu.VMEM((seq, LANES), F32)] * 6 + [
                        pltpu.VMEM((2, seq, LANES), BF16), pltpu.VMEM((seq, LANES), BF16),
                        pltpu.VMEM((2, seq // BAND, HEAD_DIM + ones_rows, BAND), BF16),
                        pltpu.VMEM((n_pat, seq, LANES), F32), pltpu.VMEM((n_pat, seq, LANES), F32)],
        compiler_params=_params(2),
        name="dilated",
    )(pc, pc, pc, jnp.tile(gq, 2).reshape(1, LANES), jnp.tile(gk, 2).reshape(1, LANES))


def kernel(x, norm_g, w_in, w_out, tshift_mu, decay_w0, decay_up, iclr_a0, iclr_up,
           k_k, k_a, r_k, lnx_g, lnx_b, moba_q_g, moba_k_g, dil_q_g, dil_k_g):
    bsz, seq, d_model = x.shape
    depth = norm_g.shape[0]
    assert all(b == DIL_RATIO * a for a, b in zip(DILATIONS, DILATIONS[1:])) and len(DILATIONS) == 3
    assert seq % (DIL_GROUP * BAND) == 0 and seq % (max(DILATIONS) * BAND) == 0, "dilated tiling"
    assert seq % (MOBA_TILES * MOBA_BLOCK) == 0 and MOBA_TILES % 2 == 0, "moba tiling"
    assert seq % ROW_TILE == 0 and ROW_TILE % RWKV_CHUNK == 0 and (bsz * seq) % ROW_TILE == 0, "row tiling"
    x2d = x.reshape(bsz * seq, d_model)
    w_in_bf = w_in.astype(BF16)
    w_out_bf = w_out.astype(BF16)
    pa, pb, pc, gate = _proj(x2d, in_args=(norm_g[0], w_in_bf[0]))
    for l in range(depth):
        pp, wz = _rwkv_pack_params(decay_w0[l], decay_up[l], iclr_a0[l], iclr_up[l],
                                   k_k[l], k_a[l], r_k[l], lnx_g[l], lnx_b[l])
        ya = _rwkv(pa.reshape(bsz, seq, A_PROJ), tshift_mu[l], pp, wz)
        yb = _moba(pb.reshape(bsz, seq, B_PROJ), moba_q_g[l], moba_k_g[l])
        yc = _dilated(pc.reshape(bsz, seq, C_PROJ), dil_q_g[l], dil_k_g[l])
        out_args = (ya.reshape(bsz * seq, A_WIDTH), yb.reshape(bsz * seq, B_WIDTH),
                    yc.reshape(bsz * seq, C_WIDTH), gate, w_out_bf[l])
        if l + 1 < depth:
            x2d, pa, pb, pc, gate = _proj(x2d, out_args, (norm_g[l + 1], w_in_bf[l + 1]))
        else:
            (x2d,) = _proj(x2d, out_args)
    return x2d.reshape(bsz, seq, d_model)
```

```python
import functools

import jax
import jax.numpy as jnp
from jax import lax
from jax.experimental import pallas as pl
from jax.experimental.pallas import tpu as pltpu

F32 = jnp.float32
BF16 = jnp.bfloat16

HEAD_DIM = 64
LANES = 128
A_HEADS, B_HEADS, C_HEADS = 6, 4, 6
A_WIDTH, B_WIDTH, C_WIDTH = A_HEADS * HEAD_DIM, B_HEADS * HEAD_DIM, C_HEADS * HEAD_DIM
LORA = 64
A_PROJ = 3 * A_WIDTH + 2 * LORA
B_PROJ = 3 * B_WIDTH
C_PROJ = 3 * C_WIDTH
MIX_WIDTH = A_WIDTH + B_WIDTH + C_WIDTH
PROJ_WIDTH = A_PROJ + B_PROJ + C_PROJ + MIX_WIDTH
MOBA_BLOCK = 256
MOBA_TOPK = 3
MOBA_TILES = 4
DILATIONS = (1, 4, 16)
DIL_RATIO = 4
BAND = 128
DIL_GROUP = 16
RMS_EPS = 1e-6
LNX_EPS = HEAD_DIM * 1e-5
ATTN_SCALE = HEAD_DIM ** -0.5
RWKV_CHUNK = 64
NEG = -1e30
MAX_SAFE_BOUND = 40.0
VMEM_LIMIT = 56 * 1024 * 1024
BF16_SUBLANES = 16
ROW_TILE = 512


def _layer_block(shape, layer, n_grid_axes):
    index = (layer,) + (0,) * len(shape)
    if n_grid_axes == 1:
        return pl.BlockSpec((None,) + tuple(shape), lambda i: index)
    return pl.BlockSpec((None,) + tuple(shape), lambda b, j: index)


def _params(n_axes):
    return pltpu.CompilerParams(dimension_semantics=("arbitrary",) * n_axes,
                                vmem_limit_bytes=VMEM_LIMIT)


_NN = (((1,), (0,)), ((), ()))
_NT = (((1,), (1,)), ((), ()))
_TN = (((0,), (0,)), ((), ()))


def _mm(a, b, dims=_NN):
    return lax.dot_general(a, b, dims, preferred_element_type=F32)


def _split(x):
    hi = x.astype(BF16)
    lo = (x - hi.astype(F32)).astype(BF16)
    return hi, lo


def _mm3(a, b, dims=_NN):
    ah, al = _split(a)
    bh, bl = _split(b)
    return _mm(ah, bh, dims) + (_mm(ah, bl, dims) + _mm(al, bh, dims))


def _mm1(a, b, dims=_NN):
    return _mm(a.astype(BF16), b.astype(BF16), dims)


def _half_sum(x, is_a):
    sa = jnp.sum(jnp.where(is_a, x, 0.0), axis=-1, keepdims=True)
    sb = jnp.sum(jnp.where(is_a, 0.0, x), axis=-1, keepdims=True)
    return jnp.where(is_a, sa, sb)


def _half_sum_mxu(x):
    row = lax.broadcasted_iota(jnp.int32, (LANES, LANES), 0) // HEAD_DIM
    col = lax.broadcasted_iota(jnp.int32, (LANES, LANES), 1) // HEAD_DIM
    ones_bd = (row == col).astype(BF16)
    hi, lo = _split(x)
    return _mm(hi, ones_bd) + _mm(lo, ones_bd)


def _round_robin(chains):
    results = [None] * len(chains)
    live = list(range(len(chains)))
    while live:
        for i in list(live):
            try:
                next(chains[i])
            except StopIteration as done:
                results[i] = done.value
                live.remove(i)
    return results


def _lane_is_a(shape):
    return lax.broadcasted_iota(jnp.int32, shape, len(shape) - 1) < HEAD_DIM


_PROJ_WIDTHS = (A_PROJ, B_PROJ, C_PROJ, MIX_WIDTH)
_PROJ_DTYPES = (F32, BF16, BF16, BF16)


def _proj_kernel(*refs, has_out, has_in):
    refs = list(refs)
    x = refs.pop(0)[...]
    if has_out:
        ya_ref, yb_ref, yc_ref, gt_ref, w_out_ref = (refs.pop(0) for _ in range(5))
    if has_in:
        g_ref, w_in_ref = refs.pop(0), refs.pop(0)
    if has_out:
        gated = []
        lo = 0
        for y_ref in (ya_ref, yb_ref, yc_ref):
            hi = lo + y_ref.shape[-1]
            g = gt_ref[:, lo:hi].astype(F32)
            gated.append((y_ref[...] * (g * jax.nn.sigmoid(g))).astype(BF16))
            lo = hi
        half = MIX_WIDTH // 2
        split = half - gated[0].shape[-1]
        x = x + _mm(jnp.concatenate([gated[0], gated[1][:, :split]], axis=1), w_out_ref[:half, :])
        x = x + _mm(jnp.concatenate([gated[1][:, split:], gated[2]], axis=1), w_out_ref[half:, :])
        refs.pop(0)[...] = x
    if has_in:
        ms = jnp.mean(x * x, axis=-1, keepdims=True)
        h = (x * lax.rsqrt(ms + RMS_EPS) * g_ref[...]).astype(BF16)
        lo = 0
        for ref in refs:
            hi = lo + ref.shape[-1]
            ref[...] = _mm(h, w_in_ref[:, lo:hi]).astype(ref.dtype)
            lo = hi


def _proj(x2d, out_args=None, in_args=None, tm=ROW_TILE):
    m, d = x2d.shape
    row = lambda w: pl.BlockSpec((tm, w), lambda i: (i, 0))
    args, in_specs, out_specs, out_shape = [x2d], [row(d)], [], []
    if out_args is not None:
        *ys_and_gate, w_out, layer = out_args
        args += ys_and_gate + [w_out]
        in_specs += [row(A_WIDTH), row(B_WIDTH), row(C_WIDTH), row(MIX_WIDTH),
                     _layer_block((MIX_WIDTH, d), layer, 1)]
        out_specs.append(row(d))
        out_shape.append(jax.ShapeDtypeStruct((m, d), F32))
    if in_args is not None:
        gain, w_in, layer = in_args
        args += [gain, w_in]
        in_specs += [_layer_block((1, d), layer, 1), _layer_block((d, PROJ_WIDTH), layer, 1)]
        out_specs += [row(w) for w in _PROJ_WIDTHS]
        out_shape += [jax.ShapeDtypeStruct((m, w), dt) for w, dt in zip(_PROJ_WIDTHS, _PROJ_DTYPES)]
    return pl.pallas_call(
        functools.partial(_proj_kernel, has_out=out_args is not None, has_in=in_args is not None),
        grid=(m // tm,),
        in_specs=in_specs, out_specs=out_specs, out_shape=out_shape,
        compiler_params=_params(1),
        name="proj",
    )(*args)


_P_W0, _P_A0, _P_KK, _P_KA, _P_RK, _P_LNG, _P_LNB = range(7)
_P_ROWS = 8


def _rwkv_kernel(pa_ref, mu_ref, pp_ref, wz_ref, o_ref, st_ref, prev_ref):
    n_rows, tile = pa_ref.shape[0], pa_ref.shape[1]
    c_len = RWKV_CHUNK
    two_c = 2 * c_len
    n_chunks = tile // c_len
    n_pairs = A_HEADS // 2
    is_a = _lane_is_a((c_len, LANES))

    row_c = lax.broadcasted_iota(jnp.int32, (c_len, A_PROJ), 0)
    ri = lax.broadcasted_iota(jnp.int32, (c_len, c_len), 0)
    ci = lax.broadcasted_iota(jnp.int32, (c_len, c_len), 1)
    tril_c = (ri >= ci).astype(BF16)
    r2 = lax.broadcasted_iota(jnp.int32, (c_len, two_c), 0)
    c2 = lax.broadcasted_iota(jnp.int32, (c_len, two_c), 1) % c_len
    m_strict = r2 > c2
    m_incl = r2 >= c2
    eye2 = (r2 == c2).astype(F32)

    def stack(x):
        return jnp.concatenate([jnp.where(is_a, x, 0.0), jnp.where(is_a, 0.0, x)], axis=0)

    @pl.when(pl.program_id(1) == 0)
    def _():
        st_ref[...] = jnp.zeros_like(st_ref)
        prev_ref[...] = jnp.zeros_like(prev_ref)

    def pair_chunk(b, j, r, k, v, lora_w, lora_a):
        pp = pp_ref[j]
        prow = lambda i: pp[i:i + 1, :]
        w0, a0, k_k, k_a, r_k = prow(_P_W0), prow(_P_A0), prow(_P_KK), prow(_P_KA), prow(_P_RK)
        ln_g, ln_b = prow(_P_LNG), prow(_P_LNB)
        w = -jax.nn.softplus(-(w0 + lora_w)) - 0.5
        lw = -jnp.exp(w)
        a = jax.nn.sigmoid(a0 + lora_a)
        kk = k * k_k
        kk = kk * lax.rsqrt(_half_sum(kk * kk, is_a) + 1e-12)
        k2 = k * (1.0 + (a - 1.0) * k_a)
        kka = kk * a

        l1 = lw.astype(BF16)
        rem = lw - l1.astype(F32)
        l2 = rem.astype(BF16)
        l3 = (rem - l2.astype(F32)).astype(BF16)
        g = _mm(tril_c, l1) + (_mm(tril_c, l2) + _mm(tril_c, l3))
        yield
        g_end = g[c_len - 1:c_len, :]
        e_pos = jnp.exp(g)
        e_neg = jnp.exp(-g)
        e_prev = jnp.exp(g - lw)
        e_tail = jnp.exp(g_end - g)

        ab = -kk * e_prev
        rb = r * e_pos
        ab2 = stack(ab)
        bt2 = stack(kka * e_neg)
        kt2 = stack(k2 * e_neg)
        bp2 = stack(kka * e_tail)
        kp2 = stack(k2 * e_tail)
        v2 = stack(v)

        mm = _mm1(jnp.concatenate([ab, rb], axis=0), jnp.concatenate([bt2, kt2], axis=0), _NT)
        yield
        l_b = jnp.where(m_strict, mm[:c_len, :two_c], 0.0)
        l_k = jnp.where(m_strict, mm[:c_len, two_c:], 0.0)
        r_b = jnp.where(m_incl, mm[c_len:, :two_c], 0.0)
        r_k2 = jnp.where(m_incl, mm[c_len:, two_c:], 0.0)

        t_inv = eye2 + l_b
        p = _mm1(l_b, stack(l_b))
        kv = _mm1(l_k, v2)
        yield
        steps = c_len.bit_length() - 2
        for i in range(steps):
            if i + 1 < steps:
                tp = _mm1(jnp.concatenate([t_inv, p], axis=0), stack(p))
                yield
                t_inv = t_inv + tp[:c_len]
                p = tp[c_len:]
            else:
                tp = _mm1(t_inv, stack(p))
                yield
                t_inv = t_inv + tp

        tw = _mm1(t_inv, jnp.concatenate([ab2, stack(kv)], axis=1))
        yield
        sv = st_ref[b, j]
        ws = _mm1(jnp.concatenate([tw[:, :LANES], rb], axis=0), sv, _NT)
        yield
        uv = jnp.concatenate([stack(ws[:c_len] + tw[:, LANES:]), v2], axis=0)
        y = ws[c_len:] + _mm1(jnp.concatenate([r_b, r_k2], axis=1), uv)
        st_ref[b, j] = sv * jnp.exp(g_end) + _mm1(uv, jnp.concatenate([bp2, kp2], axis=0), _TN)
        yield

        mean = _half_sum(y, is_a) * (1.0 / HEAD_DIM)
        yc = y - mean
        var = _half_sum(yc * yc, is_a) * (1.0 / HEAD_DIM)
        y = yc * lax.rsqrt(var + LNX_EPS) * ln_g + ln_b
        return y + _half_sum(r * k2 * r_k, is_a) * v

    mu = mu_ref[...]
    wz = wz_ref[...]

    def body(c, prev_rows):
        t0 = pl.multiple_of(c * c_len, c_len)
        chains, last_rows = [], []
        for b in range(n_rows):
            x = pa_ref[b, pl.ds(t0, c_len), :]
            prev = jnp.where(row_c == 0, prev_rows[b], pltpu.roll(x, 1, axis=0))
            xs = x + (prev - x) * mu
            slab = lambda i, xs=xs: xs[:, i * LANES:(i + 1) * LANES]
            z = slab(3 * n_pairs)
            lora = _mm3(jnp.where(is_a, jnp.tanh(z), z), wz)
            chains += [pair_chunk(b, j, slab(j), slab(n_pairs + j), slab(2 * n_pairs + j),
                                  lora[:, 2 * j * LANES:(2 * j + 1) * LANES],
                                  lora[:, (2 * j + 1) * LANES:(2 * j + 2) * LANES])
                       for j in range(n_pairs)]
            last_rows.append(x[c_len - 1:c_len, :])
        for i, y in enumerate(_round_robin(chains)):
            b, j = divmod(i, n_pairs)
            o_ref[b, pl.ds(t0, c_len), j * LANES:(j + 1) * LANES] = y
        return tuple(last_rows)

    last = lax.fori_loop(0, n_chunks, body, tuple(prev_ref[b, 0:1, :] for b in range(n_rows)), unroll=2)
    for b in range(n_rows):
        prev_ref[b, 0:1, :] = last[b]


def _rwkv(pa, mu, pp, wz, layer, tile=ROW_TILE):
    bsz, seq, _ = pa.shape
    rows = next(r for r in (4, 2, 1) if bsz % r == 0)
    n_pairs = A_HEADS // 2
    return pl.pallas_call(
        _rwkv_kernel,
        grid=(bsz // rows, seq // tile),
        in_specs=[pl.BlockSpec((rows, tile, A_PROJ), lambda b, s: (b, s, 0)),
                  _layer_block((1, A_PROJ), layer, 2), _layer_block((n_pairs, _P_ROWS, LANES), layer, 2),
                  _layer_block((LANES, n_pairs * 2 * LANES), layer, 2)],
        out_specs=pl.BlockSpec((rows, tile, A_WIDTH), lambda b, s: (b, s, 0)),
        out_shape=jax.ShapeDtypeStruct((bsz, seq, A_WIDTH), F32),
        scratch_shapes=[pltpu.VMEM((rows, n_pairs, LANES, LANES), F32), pltpu.VMEM((rows, 8, A_PROJ), F32)],
        compiler_params=_params(2),
        name="rwkv",
    )(pa, mu, pp, wz)


def _rwkv_pack_params(w0, w_up, a0, a_up, k_k, k_a, r_k, ln_g, ln_b):
    depth = w0.shape[0]
    n_pairs = A_HEADS // 2
    pair = lambda t, j: t[..., j * LANES:(j + 1) * LANES]
    rows = {_P_W0: w0, _P_A0: a0, _P_KK: k_k, _P_KA: k_a, _P_RK: r_k.reshape(depth, -1),
            _P_LNG: ln_g, _P_LNB: ln_b}
    zero_row = jnp.zeros((depth, LANES), F32)
    zeros = jnp.zeros((depth, LORA, LANES), F32)
    pps, wzs = [], []
    for j in range(n_pairs):
        pps.append(jnp.stack([pair(rows[i], j) if i in rows else zero_row for i in range(_P_ROWS)], axis=1))
        wzs.append(jnp.concatenate([pair(w_up, j), zeros], axis=1))
        wzs.append(jnp.concatenate([zeros, pair(a_up, j)], axis=1))
    return jnp.stack(pps, axis=1), jnp.concatenate(wzs, axis=2)


def _qk_norm(x, gain):
    ms = _half_sum_mxu(x * x) * (1.0 / HEAD_DIM)
    return x * lax.rsqrt(ms + RMS_EPS) * gain


def _score_bound(gq, gk):
    return (HEAD_DIM * ATTN_SCALE) * jnp.max(jnp.abs(gq)) * jnp.max(jnp.abs(gk))


def _moba_kernel(q_ref, k_ref, v_ref, gq_ref, gk_ref, o_ref,
                 qn_ref, kb_ref, qs_ref, km_ref, vt_ref, bias_ref):
    seq = q_ref.shape[1]
    blk = MOBA_BLOCK
    nb = seq // blk
    ones_rows = vt_ref.shape[2] - HEAD_DIM
    is_a = _lane_is_a((blk, LANES))
    gq, gk = gq_ref[...], gk_ref[...]

    def prepare(i, _):
        for n in (2 * i, 2 * i + 1):
            rows = pl.ds(pl.multiple_of(n * blk, blk), blk)
            qn = _qk_norm(q_ref[0, rows, :].astype(F32), gq)
            qn_ref[rows, :] = qn
            qs = qn * ATTN_SCALE
            qs_ref[0, rows, :] = jnp.where(is_a, qs, 0.0).astype(BF16)
            qs_ref[1, rows, :] = jnp.where(is_a, 0.0, qs).astype(BF16)
            kn = _qk_norm(k_ref[0, rows, :].astype(F32), gk)
            kb_ref[rows, :] = kn.astype(BF16)
            km = jnp.mean(kn, axis=0, keepdims=True)
            km_ref[0, pl.ds(n, 1), :] = jnp.where(is_a[:1], km, 0.0)
            km_ref[1, pl.ds(n, 1), :] = jnp.where(is_a[:1], 0.0, km)
            vt = v_ref[0, rows, :].astype(F32).T.astype(BF16)
            ones = jnp.ones((ones_rows, blk), BF16)
            for h in range(2):
                vt_ref[h, n, :HEAD_DIM, :] = vt[h * HEAD_DIM:(h + 1) * HEAD_DIM]
                vt_ref[h, n, HEAD_DIM:, :] = ones
        return 0

    lax.fori_loop(0, nb // 2, prepare, 0)

    bound = _score_bound(gq, gk)
    bound_is_safe = bound <= MAX_SAFE_BOUND
    bound_row = jnp.full((1, blk), bound, F32)
    blk_row = lax.broadcasted_iota(jnp.int32, (nb, blk), 0)
    key_pos = lax.broadcasted_iota(jnp.int32, (blk, blk), 0)
    qry_pos = lax.broadcasted_iota(jnp.int32, (blk, blk), 1)
    causal = key_pos <= qry_pos

    n_tiles = MOBA_TILES
    combos = [(t, h) for t in range(n_tiles) for h in range(2)]

    def select_blocks(i, slot):
        for t, h in combos:
            qt = n_tiles * i + t
            gate = _mm3(km_ref[h], qn_ref[pl.ds(pl.multiple_of(qt * blk, blk), blk), :], _NT)
            gate = jnp.where(blk_row < qt, gate, -jnp.inf)
            bias = jnp.full((nb, blk), NEG, F32)
            for _ in range(MOBA_TOPK):
                top = jnp.max(gate, axis=0, keepdims=True)
                hit = (gate == top) & (top > -jnp.inf)
                first = jnp.min(jnp.where(hit, blk_row, nb), axis=0, keepdims=True)
                pick = blk_row == first
                bias = jnp.where(pick, 0.0, bias)
                gate = jnp.where(pick, -jnp.inf, gate)
            bias_ref[slot, t, h] = bias

    select_blocks(0, 0)

    def q_tile_group(i, _):
        tiles = [n_tiles * i + t for t in range(n_tiles)]
        rows = [pl.ds(pl.multiple_of(qt * blk, blk), blk) for qt in tiles]
        slot = i % 2

        qs = {(t, h): qs_ref[h, rows[t], :] for t, h in combos}

        def scores(n, t, h, own, minus=0.0):
            s = _mm(kb_ref[pl.ds(pl.multiple_of(n * blk, blk), blk), :], qs[t, h], _NT)
            if own:
                return jnp.where(causal, s - minus, NEG)
            return s + (bias_ref[slot, t, h, pl.ds(n, 1), :] - minus)

        tail = [(tiles[n], t, n == t) for t in range(n_tiles) for n in range(t + 1)]
        n_past_pairs = (n_tiles // 2) * i

        def exact_max():
            def past_pair(j, m):
                s = {(k, t, h): scores(2 * j + k, t, h, False) for k in range(2) for t, h in combos}
                return {(t, h): jnp.maximum(m[t, h], jnp.max(jnp.maximum(s[0, t, h], s[1, t, h]),
                                                                 axis=0, keepdims=True)) for t, h in combos}

            m = {(t, h): jnp.full((1, blk), NEG, F32) for t, h in combos}
            for n, t, own in tail:
                for h in range(2):
                    m[t, h] = jnp.maximum(m[t, h], jnp.max(scores(n, t, h, own), axis=0, keepdims=True))
            return lax.fori_loop(0, n_past_pairs, past_pair, m)

        m = lax.cond(bound_is_safe, lambda: {c: bound_row for c in combos}, exact_max)

        def weighted_v(s, n, h):
            return _mm(vt_ref[h, n], jnp.exp(s).astype(BF16))

        def past_pair(j, pv):
            s = {(k, t, h): scores(2 * j + k, t, h, False, m[t, h]) for k in range(2) for t, h in combos}
            return {(t, h): pv[t, h] + weighted_v(s[0, t, h], 2 * j, h) + weighted_v(s[1, t, h], 2 * j + 1, h)
                    for t, h in combos}

        select_blocks(jnp.minimum(i + 1, nb // n_tiles - 1), 1 - slot)
        s = {(k, h): scores(n, t, h, own, m[t, h]) for k, (n, t, own) in enumerate(tail) for h in range(2)}
        pv = {(t, h): sum(weighted_v(s[k, h], n, h) for k, (n, tt, _) in enumerate(tail) if tt == t)
              for t, h in combos}
        pv = lax.fori_loop(0, n_past_pairs, past_pair, pv)
        for t in range(n_tiles):
            o_t = jnp.concatenate([pv[t, h][:HEAD_DIM] / pv[t, h][HEAD_DIM:HEAD_DIM + 1] for h in range(2)],
                                  axis=0)
            o_ref[0, rows[t], :] = o_t.T
        return 0

    lax.fori_loop(0, nb // n_tiles, q_tile_group, 0)


def _moba(pb, gq, gk, layer):
    bsz, seq, _ = pb.shape
    n_pairs = B_HEADS // 2
    nb = seq // MOBA_BLOCK
    ones_rows = BF16_SUBLANES
    slab = lambda off: pl.BlockSpec((1, seq, LANES), lambda b, j: (b, 0, off + j))
    gain = _layer_block((1, LANES), layer, 2)
    return pl.pallas_call(
        _moba_kernel,
        grid=(bsz, n_pairs),
        in_specs=[slab(0), slab(n_pairs), slab(2 * n_pairs), gain, gain],
        out_specs=pl.BlockSpec((1, seq, LANES), lambda b, j: (b, 0, j)),
        out_shape=jax.ShapeDtypeStruct((bsz, seq, B_WIDTH), F32),
        scratch_shapes=[pltpu.VMEM((seq, LANES), F32), pltpu.VMEM((seq, LANES), BF16),
                        pltpu.VMEM((2, seq, LANES), BF16), pltpu.VMEM((2, nb, LANES), F32),
                        pltpu.VMEM((2, nb, HEAD_DIM + ones_rows, MOBA_BLOCK), BF16),
                        pltpu.VMEM((2, MOBA_TILES, 2, nb, MOBA_BLOCK), F32)],
        compiler_params=_params(2),
        name="moba",
    )(pb, pb, pb, gq, gk)


def _dilated_kernel(q_ref, k_ref, v_ref, gq_ref, gk_ref, o_ref,
                    qn_ref, kn_ref, vv_ref, q1_ref, k1_ref, v1_ref, qs_ref, kb_ref, vt_ref, og_ref, lse_ref):
    seq = q_ref.shape[1]
    n_blocks = seq // BAND
    ones_rows = vt_ref.shape[2] - HEAD_DIM
    tile = ROW_TILE

    def normalise(i, _):
        rows = pl.ds(pl.multiple_of(i * tile, tile), tile)
        qn_ref[rows, :] = _qk_norm(q_ref[0, rows, :].astype(F32), gq_ref[...] * ATTN_SCALE)
        kn_ref[rows, :] = _qk_norm(k_ref[0, rows, :].astype(F32), gk_ref[...])
        vv_ref[rows, :] = v_ref[0, rows, :].astype(F32)
        return 0

    lax.fori_loop(0, seq // tile, normalise, 0)
    bound = _score_bound(gq_ref[...], gk_ref[...])
    bound_is_safe = bound <= MAX_SAFE_BOUND
    bound_row = jnp.full((1, BAND), bound, F32)

    is_a = _lane_is_a((BAND, LANES))
    key_j = lax.broadcasted_iota(jnp.int32, (2 * BAND, BAND), 0)
    qry_i = lax.broadcasted_iota(jnp.int32, (2 * BAND, BAND), 1)
    band_bias = jnp.where((key_j >= qry_i) & (key_j <= qry_i + BAND), 0.0, NEG)
    band_bias = jnp.concatenate([band_bias, band_bias], axis=1)
    band_bias_less_bound = band_bias - bound
    ones = jnp.ones((ones_rows, BAND), BF16)
    vt_rows = HEAD_DIM + ones_rows

    def own_rows(gb):
        return pl.ds(pl.multiple_of(gb * BAND, BAND), BAND)

    def rows_in_previous_level(g, gb):
        per_residue = n_blocks // DILATIONS[g]
        segment, c = gb // per_residue, gb % per_residue
        start = (segment // DIL_RATIO) * (seq // DILATIONS[g - 1]) + segment % DIL_RATIO
        return pl.ds(start + c * (BAND * DIL_RATIO), BAND, stride=DIL_RATIO)

    for g, dil in enumerate(DILATIONS):
        per_residue = n_blocks // dil
        sources = (qn_ref, kn_ref, vv_ref) if g <= 1 else (q1_ref, k1_ref, v1_ref)

        def gather(i, _, g=g, sources=sources):
            for gb in [DIL_GROUP * i + j for j in range(DIL_GROUP)]:
                rows = own_rows(gb) if g == 0 else rows_in_previous_level(g, gb)
                dst = own_rows(gb)
                q, k, v = (ref[rows, :] for ref in sources)
                if g == 1:
                    q1_ref[dst, :], k1_ref[dst, :], v1_ref[dst, :] = q, k, v
                qs_ref[0, dst, :] = jnp.where(is_a, q, 0.0).astype(BF16)
                qs_ref[1, dst, :] = jnp.where(is_a, 0.0, q).astype(BF16)
                kb_ref[dst, :] = k.astype(BF16)
                vt = v.T.astype(BF16)
                for h in range(2):
                    vt_ref[h, gb, :HEAD_DIM, :] = vt[h * HEAD_DIM:(h + 1) * HEAD_DIM]
                    vt_ref[h, gb, HEAD_DIM:, :] = ones
            return 0

        lax.fori_loop(0, n_blocks // DIL_GROUP, gather, 0)

        def band_block(gb, use_bound, per_residue=per_residue, g=g):
            prev = jnp.maximum(gb - 1, 0)
            first = jnp.where(gb % per_residue == 0, NEG, 0.0)
            kb = jnp.concatenate([kb_ref[own_rows(prev), :], kb_ref[own_rows(gb), :]], axis=0)

            def scores(bias):
                q_both = jnp.concatenate([qs_ref[0, own_rows(gb), :], qs_ref[1, own_rows(gb), :]], axis=0)
                s = _mm(kb, q_both, _NT) + bias
                return jnp.concatenate([s[:BAND] + first, s[BAND:]], axis=0)

            if use_bound:
                m = jnp.concatenate([bound_row, bound_row], axis=1)
                s = scores(band_bias_less_bound)
                yield
                p = jnp.exp(s)
            else:
                m = jnp.max(scores(band_bias), axis=0, keepdims=True)
                yield
                s = scores(band_bias)
                yield
                p = jnp.exp(s - m)
            vt_both = jnp.concatenate([jnp.concatenate([vt_ref[h, prev], vt_ref[h, gb]], axis=1)
                                       for h in range(2)], axis=0)
            pv = _mm(vt_both, p.astype(BF16))
            yield
            pv = [pv[h * vt_rows:(h + 1) * vt_rows, h * BAND:(h + 1) * BAND] for h in range(2)]
            l = [x[HEAD_DIM:HEAD_DIM + 1] for x in pv]
            o_t = jnp.concatenate([pv[h][:HEAD_DIM] / l[h] for h in range(2)], axis=0)
            lse_t = jnp.concatenate([jnp.broadcast_to(m[:, h * BAND:(h + 1) * BAND] + jnp.log(l[h]),
                                                      (HEAD_DIM, BAND)) for h in range(2)], axis=0)
            out_rows = own_rows(gb) if g <= 1 else rows_in_previous_level(g, gb)
            og_ref[g, out_rows, :] = o_t.T
            lse_ref[g, out_rows, :] = lse_t.T

        def group(i, _, use_bound, band_block=band_block):
            _round_robin([band_block(DIL_GROUP * i + j, use_bound) for j in range(DIL_GROUP)])
            return 0

        for use_bound in (True, False):
            @pl.when(bound_is_safe == use_bound)
            def _(use_bound=use_bound, group=group):
                lax.fori_loop(0, n_blocks // DIL_GROUP, functools.partial(group, use_bound=use_bound), 0)

    segment_len = seq // DILATIONS[1]

    def mix(i, _):
        rows = pl.ds(pl.multiple_of(i * tile, tile), tile)
        start = i * tile
        tokens = pl.ds(start // segment_len + DIL_RATIO * (start % segment_len), tile, stride=DIL_RATIO)
        where = (tokens, rows, rows)
        lse = [lse_ref[g, where[g], :] for g in range(len(DILATIONS))]
        top = functools.reduce(jnp.maximum, lse)
        w = [jnp.exp(x - top) for x in lse]
        o_ref[0, tokens, :] = sum(w[g] * og_ref[g, where[g], :] for g in range(len(DILATIONS))) / sum(w)
        return 0

    lax.fori_loop(0, seq // tile, mix, 0)


def _dilated(pc, gq, gk, layer):
    bsz, seq, _ = pc.shape
    n_pairs = C_HEADS // 2
    n_pat = len(DILATIONS)
    ones_rows = BF16_SUBLANES
    slab = lambda off: pl.BlockSpec((1, seq, LANES), lambda b, j: (b, 0, off + j))
    gain = _layer_block((1, LANES), layer, 2)
    return pl.pallas_call(
        _dilated_kernel,
        grid=(bsz, n_pairs),
        in_specs=[slab(0), slab(n_pairs), slab(2 * n_pairs), gain, gain],
        out_specs=pl.BlockSpec((1, seq, LANES), lambda b, j: (b, 0, j)),
        out_shape=jax.ShapeDtypeStruct((bsz, seq, C_WIDTH), F32),
        scratch_shapes=[pltpu.VMEM((seq, LANES), F32)] * 6 + [
                        pltpu.VMEM((2, seq, LANES), BF16), pltpu.VMEM((seq, LANES), BF16),
                        pltpu.VMEM((2, seq // BAND, HEAD_DIM + ones_rows, BAND), BF16),
                        pltpu.VMEM((n_pat, seq, LANES), F32), pltpu.VMEM((n_pat, seq, LANES), F32)],
        compiler_params=_params(2),
        name="dilated",
    )(pc, pc, pc, gq, gk)


def kernel(x, norm_g, w_in, w_out, tshift_mu, decay_w0, decay_up, iclr_a0, iclr_up,
           k_k, k_a, r_k, lnx_g, lnx_b, moba_q_g, moba_k_g, dil_q_g, dil_k_g):
    bsz, seq, d_model = x.shape
    depth = norm_g.shape[0]
    assert all(b == DIL_RATIO * a for a, b in zip(DILATIONS, DILATIONS[1:])) and len(DILATIONS) == 3
    assert seq % (DIL_GROUP * BAND) == 0 and seq % (max(DILATIONS) * BAND) == 0, "dilated tiling"
    assert seq % (MOBA_TILES * MOBA_BLOCK) == 0 and MOBA_TILES % 2 == 0, "moba tiling"
    assert seq % ROW_TILE == 0 and ROW_TILE % RWKV_CHUNK == 0 and (bsz * seq) % ROW_TILE == 0, "row tiling"
    x2d = x.reshape(bsz * seq, d_model)
    w_in_bf = w_in.astype(BF16)
    w_out_bf = w_out.astype(BF16)
    gains = norm_g.reshape(depth, 1, d_model)
    pp, wz = _rwkv_pack_params(decay_w0, decay_up, iclr_a0, iclr_up, k_k, k_a, r_k, lnx_g, lnx_b)
    mu = tshift_mu.reshape(depth, 1, A_PROJ)
    two_heads = lambda g: jnp.tile(g, (1, 2)).reshape(depth, 1, LANES)
    moba_gq, moba_gk, dil_gq, dil_gk = map(two_heads, (moba_q_g, moba_k_g, dil_q_g, dil_k_g))
    pa, pb, pc, gate = _proj(x2d, in_args=(gains, w_in_bf, 0))
    for l in range(depth):
        ya = _rwkv(pa.reshape(bsz, seq, A_PROJ), mu, pp, wz, l)
        yb = _moba(pb.reshape(bsz, seq, B_PROJ), moba_gq, moba_gk, l)
        yc = _dilated(pc.reshape(bsz, seq, C_PROJ), dil_gq, dil_gk, l)
        out_args = (ya.reshape(bsz * seq, A_WIDTH), yb.reshape(bsz * seq, B_WIDTH),
                    yc.reshape(bsz * seq, C_WIDTH), gate, w_out_bf, l)
        if l + 1 < depth:
            x2d, pa, pb, pc, gate = _proj(x2d, out_args, (gains, w_in_bf, l + 1))
        else:
            (x2d,) = _proj(x2d, out_args)
    return x2d.reshape(bsz, seq, d_model)
```

```python
import functools

import jax
import jax.numpy as jnp
from jax import lax
from jax.experimental import pallas as pl
from jax.experimental.pallas import tpu as pltpu

F32 = jnp.float32
BF16 = jnp.bfloat16

HEAD_DIM = 64
LANES = 128
A_HEADS, B_HEADS, C_HEADS = 6, 4, 6
A_WIDTH, B_WIDTH, C_WIDTH = A_HEADS * HEAD_DIM, B_HEADS * HEAD_DIM, C_HEADS * HEAD_DIM
LORA = 64
A_PROJ = 3 * A_WIDTH + 2 * LORA
B_PROJ = 3 * B_WIDTH
C_PROJ = 3 * C_WIDTH
MIX_WIDTH = A_WIDTH + B_WIDTH + C_WIDTH
PROJ_WIDTH = A_PROJ + B_PROJ + C_PROJ + MIX_WIDTH
MOBA_BLOCK = 256
MOBA_TOPK = 3
MOBA_TILES = 8
DILATIONS = (1, 4, 16)
DIL_RATIO = 4
BAND = 128
DIL_GROUP = 32
RMS_EPS = 1e-6
LNX_EPS = HEAD_DIM * 1e-5
ATTN_SCALE = HEAD_DIM ** -0.5
RWKV_CHUNK = 64
NEG = -1e30
MAX_SAFE_BOUND = 40.0
VMEM_LIMIT = 56 * 1024 * 1024
BF16_SUBLANES = 16
ROW_TILE = 512


def _layer_block(shape, layer, n_grid_axes):
    index = (layer,) + (0,) * len(shape)
    if n_grid_axes == 1:
        return pl.BlockSpec((None,) + tuple(shape), lambda i: index)
    return pl.BlockSpec((None,) + tuple(shape), lambda b, j: index)


def _params(n_axes):
    return pltpu.CompilerParams(dimension_semantics=("arbitrary",) * n_axes,
                                vmem_limit_bytes=VMEM_LIMIT)


_NN = (((1,), (0,)), ((), ()))
_NT = (((1,), (1,)), ((), ()))
_TN = (((0,), (0,)), ((), ()))


def _mm(a, b, dims=_NN):
    return lax.dot_general(a, b, dims, preferred_element_type=F32)


def _split(x):
    hi = x.astype(BF16)
    lo = (x - hi.astype(F32)).astype(BF16)
    return hi, lo


def _mm3(a, b, dims=_NN):
    ah, al = _split(a)
    bh, bl = _split(b)
    return _mm(ah, bh, dims) + (_mm(ah, bl, dims) + _mm(al, bh, dims))


def _mm1(a, b, dims=_NN):
    return _mm(a.astype(BF16), b.astype(BF16), dims)


def _half_sum(x, is_a):
    sa = jnp.sum(jnp.where(is_a, x, 0.0), axis=-1, keepdims=True)
    sb = jnp.sum(jnp.where(is_a, 0.0, x), axis=-1, keepdims=True)
    return jnp.where(is_a, sa, sb)


def _half_sum_mxu(x):
    row = lax.broadcasted_iota(jnp.int32, (LANES, LANES), 0) // HEAD_DIM
    col = lax.broadcasted_iota(jnp.int32, (LANES, LANES), 1) // HEAD_DIM
    ones_bd = (row == col).astype(BF16)
    hi, lo = _split(x)
    return _mm(hi, ones_bd) + _mm(lo, ones_bd)


def _round_robin(chains):
    results = [None] * len(chains)
    live = list(range(len(chains)))
    while live:
        for i in list(live):
            try:
                next(chains[i])
            except StopIteration as done:
                results[i] = done.value
                live.remove(i)
    return results


def _lane_is_a(shape):
    return lax.broadcasted_iota(jnp.int32, shape, len(shape) - 1) < HEAD_DIM


_PROJ_WIDTHS = (A_PROJ, B_PROJ, C_PROJ, MIX_WIDTH)
_PROJ_DTYPES = (F32, BF16, BF16, BF16)


def _proj_kernel(*refs, has_out, has_in):
    refs = list(refs)
    x = refs.pop(0)[...]
    if has_out:
        ya_ref, yb_ref, yc_ref, gt_ref, w_out_ref = (refs.pop(0) for _ in range(5))
    if has_in:
        g_ref, w_in_ref = refs.pop(0), refs.pop(0)
    if has_out:
        gated = []
        lo = 0
        for y_ref in (ya_ref, yb_ref, yc_ref):
            hi = lo + y_ref.shape[-1]
            g = gt_ref[:, lo:hi].astype(F32)
            gated.append((y_ref[...] * (g * jax.nn.sigmoid(g))).astype(BF16))
            lo = hi
        half = MIX_WIDTH // 2
        split = half - gated[0].shape[-1]
        x = x + _mm(jnp.concatenate([gated[0], gated[1][:, :split]], axis=1), w_out_ref[:half, :])
        x = x + _mm(jnp.concatenate([gated[1][:, split:], gated[2]], axis=1), w_out_ref[half:, :])
        refs.pop(0)[...] = x
    if has_in:
        ms = jnp.mean(x * x, axis=-1, keepdims=True)
        h = (x * lax.rsqrt(ms + RMS_EPS) * g_ref[...]).astype(BF16)
        lo = 0
        for ref in refs:
            hi = lo + ref.shape[-1]
            ref[...] = _mm(h, w_in_ref[:, lo:hi]).astype(ref.dtype)
            lo = hi


def _proj(x2d, out_args=None, in_args=None, tm=ROW_TILE):
    m, d = x2d.shape
    row = lambda w: pl.BlockSpec((tm, w), lambda i: (i, 0))
    args, in_specs, out_specs, out_shape = [x2d], [row(d)], [], []
    if out_args is not None:
        *ys_and_gate, w_out, layer = out_args
        args += ys_and_gate + [w_out]
        in_specs += [row(A_WIDTH), row(B_WIDTH), row(C_WIDTH), row(MIX_WIDTH),
                     _layer_block((MIX_WIDTH, d), layer, 1)]
        out_specs.append(row(d))
        out_shape.append(jax.ShapeDtypeStruct((m, d), F32))
    if in_args is not None:
        gain, w_in, layer = in_args
        args += [gain, w_in]
        in_specs += [_layer_block((1, d), layer, 1), _layer_block((d, PROJ_WIDTH), layer, 1)]
        out_specs += [row(w) for w in _PROJ_WIDTHS]
        out_shape += [jax.ShapeDtypeStruct((m, w), dt) for w, dt in zip(_PROJ_WIDTHS, _PROJ_DTYPES)]
    return pl.pallas_call(
        functools.partial(_proj_kernel, has_out=out_args is not None, has_in=in_args is not None),
        grid=(m // tm,),
        in_specs=in_specs, out_specs=out_specs, out_shape=out_shape,
        compiler_params=_params(1),
        name="proj",
    )(*args)


_P_W0, _P_A0, _P_KK, _P_KA, _P_RK, _P_LNG, _P_LNB = range(7)
_P_ROWS = 8


def _rwkv_kernel(pa_ref, mu_ref, pp_ref, wz_ref, o_ref, st_ref, prev_ref):
    n_rows, tile = pa_ref.shape[0], pa_ref.shape[1]
    c_len = RWKV_CHUNK
    two_c = 2 * c_len
    n_chunks = tile // c_len
    n_pairs = A_HEADS // 2
    is_a = _lane_is_a((c_len, LANES))

    row_c = lax.broadcasted_iota(jnp.int32, (c_len, A_PROJ), 0)
    ri = lax.broadcasted_iota(jnp.int32, (c_len, c_len), 0)
    ci = lax.broadcasted_iota(jnp.int32, (c_len, c_len), 1)
    tril_c = (ri >= ci).astype(BF16)
    r2 = lax.broadcasted_iota(jnp.int32, (c_len, two_c), 0)
    c2 = lax.broadcasted_iota(jnp.int32, (c_len, two_c), 1) % c_len
    m_strict = r2 > c2
    m_incl = r2 >= c2
    eye2 = (r2 == c2).astype(F32)

    def stack(x):
        return jnp.concatenate([jnp.where(is_a, x, 0.0), jnp.where(is_a, 0.0, x)], axis=0)

    @pl.when(pl.program_id(1) == 0)
    def _():
        st_ref[...] = jnp.zeros_like(st_ref)
        prev_ref[...] = jnp.zeros_like(prev_ref)

    def pair_chunk(b, j, r, k, v, lora_w, lora_a):
        pp = pp_ref[j]
        prow = lambda i: pp[i:i + 1, :]
        w0, a0, k_k, k_a, r_k = prow(_P_W0), prow(_P_A0), prow(_P_KK), prow(_P_KA), prow(_P_RK)
        ln_g, ln_b = prow(_P_LNG), prow(_P_LNB)
        w = -jax.nn.softplus(-(w0 + lora_w)) - 0.5
        lw = -jnp.exp(w)
        a = jax.nn.sigmoid(a0 + lora_a)
        kk = k * k_k
        kk = kk * lax.rsqrt(_half_sum(kk * kk, is_a) + 1e-12)
        k2 = k * (1.0 + (a - 1.0) * k_a)
        kka = kk * a

        l1 = lw.astype(BF16)
        rem = lw - l1.astype(F32)
        l2 = rem.astype(BF16)
        l3 = (rem - l2.astype(F32)).astype(BF16)
        g = _mm(tril_c, l1) + (_mm(tril_c, l2) + _mm(tril_c, l3))
        yield
        g_end = g[c_len - 1:c_len, :]
        e_pos = jnp.exp(g)
        e_neg = jnp.exp(-g)
        e_prev = jnp.exp(g - lw)
        e_tail = jnp.exp(g_end - g)

        ab = -kk * e_prev
        rb = r * e_pos
        ab2 = stack(ab)
        bt2 = stack(kka * e_neg)
        kt2 = stack(k2 * e_neg)
        bp2 = stack(kka * e_tail)
        kp2 = stack(k2 * e_tail)
        v2 = stack(v)

        mm = _mm1(jnp.concatenate([ab, rb], axis=0), jnp.concatenate([bt2, kt2], axis=0), _NT)
        yield
        l_b = jnp.where(m_strict, mm[:c_len, :two_c], 0.0)
        l_k = jnp.where(m_strict, mm[:c_len, two_c:], 0.0)
        r_b = jnp.where(m_incl, mm[c_len:, :two_c], 0.0)
        r_k2 = jnp.where(m_incl, mm[c_len:, two_c:], 0.0)

        t_inv = eye2 + l_b
        p = _mm1(l_b, stack(l_b))
        kv = _mm1(l_k, v2)
        yield
        steps = c_len.bit_length() - 2
        for i in range(steps):
            if i + 1 < steps:
                tp = _mm1(jnp.concatenate([t_inv, p], axis=0), stack(p))
                yield
                t_inv = t_inv + tp[:c_len]
                p = tp[c_len:]
            else:
                tp = _mm1(t_inv, stack(p))
                yield
                t_inv = t_inv + tp

        tw = _mm1(t_inv, jnp.concatenate([ab2, stack(kv)], axis=1))
        yield
        sv = st_ref[b, j]
        ws = _mm1(jnp.concatenate([tw[:, :LANES], rb], axis=0), sv, _NT)
        yield
        uv = jnp.concatenate([stack(ws[:c_len] + tw[:, LANES:]), v2], axis=0)
        y = ws[c_len:] + _mm1(jnp.concatenate([r_b, r_k2], axis=1), uv)
        st_ref[b, j] = sv * jnp.exp(g_end) + _mm1(uv, jnp.concatenate([bp2, kp2], axis=0), _TN)
        yield

        mean = _half_sum(y, is_a) * (1.0 / HEAD_DIM)
        yc = y - mean
        var = _half_sum(yc * yc, is_a) * (1.0 / HEAD_DIM)
        y = yc * lax.rsqrt(var + LNX_EPS) * ln_g + ln_b
        return y + _half_sum(r * k2 * r_k, is_a) * v

    mu = mu_ref[...]
    wz = wz_ref[...]

    def body(c, prev_rows):
        t0 = pl.multiple_of(c * c_len, c_len)
        chains, last_rows = [], []
        for b in range(n_rows):
            x = pa_ref[b, pl.ds(t0, c_len), :]
            prev = jnp.where(row_c == 0, prev_rows[b], pltpu.roll(x, 1, axis=0))
            xs = x + (prev - x) * mu
            slab = lambda i, xs=xs: xs[:, i * LANES:(i + 1) * LANES]
            z = slab(3 * n_pairs)
            lora = _mm3(jnp.where(is_a, jnp.tanh(z), z), wz)
            chains += [pair_chunk(b, j, slab(j), slab(n_pairs + j), slab(2 * n_pairs + j),
                                  lora[:, 2 * j * LANES:(2 * j + 1) * LANES],
                                  lora[:, (2 * j + 1) * LANES:(2 * j + 2) * LANES])
                       for j in range(n_pairs)]
            last_rows.append(x[c_len - 1:c_len, :])
        for i, y in enumerate(_round_robin(chains)):
            b, j = divmod(i, n_pairs)
            o_ref[b, pl.ds(t0, c_len), j * LANES:(j + 1) * LANES] = y
        return tuple(last_rows)

    last = lax.fori_loop(0, n_chunks, body, tuple(prev_ref[b, 0:1, :] for b in range(n_rows)), unroll=2)
    for b in range(n_rows):
        prev_ref[b, 0:1, :] = last[b]


def _rwkv(pa, mu, pp, wz, layer, tile=ROW_TILE):
    bsz, seq, _ = pa.shape
    rows = next(r for r in (4, 2, 1) if bsz % r == 0)
    n_pairs = A_HEADS // 2
    return pl.pallas_call(
        _rwkv_kernel,
        grid=(bsz // rows, seq // tile),
        in_specs=[pl.BlockSpec((rows, tile, A_PROJ), lambda b, s: (b, s, 0)),
                  _layer_block((1, A_PROJ), layer, 2), _layer_block((n_pairs, _P_ROWS, LANES), layer, 2),
                  _layer_block((LANES, n_pairs * 2 * LANES), layer, 2)],
        out_specs=pl.BlockSpec((rows, tile, A_WIDTH), lambda b, s: (b, s, 0)),
        out_shape=jax.ShapeDtypeStruct((bsz, seq, A_WIDTH), F32),
        scratch_shapes=[pltpu.VMEM((rows, n_pairs, LANES, LANES), F32), pltpu.VMEM((rows, 8, A_PROJ), F32)],
        compiler_params=_params(2),
        name="rwkv",
    )(pa, mu, pp, wz)


def _rwkv_pack_params(w0, w_up, a0, a_up, k_k, k_a, r_k, ln_g, ln_b):
    depth = w0.shape[0]
    n_pairs = A_HEADS // 2
    pair = lambda t, j: t[..., j * LANES:(j + 1) * LANES]
    rows = {_P_W0: w0, _P_A0: a0, _P_KK: k_k, _P_KA: k_a, _P_RK: r_k.reshape(depth, -1),
            _P_LNG: ln_g, _P_LNB: ln_b}
    zero_row = jnp.zeros((depth, LANES), F32)
    zeros = jnp.zeros((depth, LORA, LANES), F32)
    pps, wzs = [], []
    for j in range(n_pairs):
        pps.append(jnp.stack([pair(rows[i], j) if i in rows else zero_row for i in range(_P_ROWS)], axis=1))
        wzs.append(jnp.concatenate([pair(w_up, j), zeros], axis=1))
        wzs.append(jnp.concatenate([zeros, pair(a_up, j)], axis=1))
    return jnp.stack(pps, axis=1), jnp.concatenate(wzs, axis=2)


def _qk_norm(x, gain):
    ms = _half_sum_mxu(x * x) * (1.0 / HEAD_DIM)
    return x * lax.rsqrt(ms + RMS_EPS) * gain


def _score_bound(gq, gk):
    return (HEAD_DIM * ATTN_SCALE) * jnp.max(jnp.abs(gq)) * jnp.max(jnp.abs(gk))


def _moba_kernel(q_ref, k_ref, v_ref, gq_ref, gk_ref, o_ref,
                 qn_ref, kb_ref, qs_ref, km_ref, vt_ref, bias_ref):
    seq = q_ref.shape[1]
    blk = MOBA_BLOCK
    nb = seq // blk
    ones_rows = vt_ref.shape[2] - HEAD_DIM
    is_a = _lane_is_a((blk, LANES))
    gq, gk = gq_ref[...], gk_ref[...]

    def prepare(i, _):
        for n in (2 * i, 2 * i + 1):
            rows = pl.ds(pl.multiple_of(n * blk, blk), blk)
            qn = _qk_norm(q_ref[0, rows, :].astype(F32), gq)
            qn_ref[rows, :] = qn
            qs = qn * ATTN_SCALE
            qs_ref[0, rows, :] = jnp.where(is_a, qs, 0.0).astype(BF16)
            qs_ref[1, rows, :] = jnp.where(is_a, 0.0, qs).astype(BF16)
            kn = _qk_norm(k_ref[0, rows, :].astype(F32), gk)
            kb_ref[rows, :] = kn.astype(BF16)
            km = jnp.mean(kn, axis=0, keepdims=True)
            km_ref[0, pl.ds(n, 1), :] = jnp.where(is_a[:1], km, 0.0)
            km_ref[1, pl.ds(n, 1), :] = jnp.where(is_a[:1], 0.0, km)
            vt = v_ref[0, rows, :].astype(F32).T.astype(BF16)
            ones = jnp.ones((ones_rows, blk), BF16)
            for h in range(2):
                vt_ref[h, n, :HEAD_DIM, :] = vt[h * HEAD_DIM:(h + 1) * HEAD_DIM]
                vt_ref[h, n, HEAD_DIM:, :] = ones
        return 0

    lax.fori_loop(0, nb // 2, prepare, 0)

    bound = _score_bound(gq, gk)
    bound_is_safe = bound <= MAX_SAFE_BOUND
    bound_row = jnp.full((1, blk), bound, F32)
    blk_row = lax.broadcasted_iota(jnp.int32, (nb, blk), 0)
    key_pos = lax.broadcasted_iota(jnp.int32, (blk, blk), 0)
    qry_pos = lax.broadcasted_iota(jnp.int32, (blk, blk), 1)
    causal = key_pos <= qry_pos

    n_tiles = MOBA_TILES
    combos = [(t, h) for t in range(n_tiles) for h in range(2)]

    def select_blocks(i, slot):
        for t, h in combos:
            qt = n_tiles * i + t
            gate = _mm3(km_ref[h], qn_ref[pl.ds(pl.multiple_of(qt * blk, blk), blk), :], _NT)
            gate = jnp.where(blk_row < qt, gate, -jnp.inf)
            bias = jnp.full((nb, blk), NEG, F32)
            for _ in range(MOBA_TOPK):
                top = jnp.max(gate, axis=0, keepdims=True)
                hit = (gate == top) & (top > -jnp.inf)
                first = jnp.min(jnp.where(hit, blk_row, nb), axis=0, keepdims=True)
                pick = blk_row == first
                bias = jnp.where(pick, 0.0, bias)
                gate = jnp.where(pick, -jnp.inf, gate)
            bias_ref[slot, t, h] = bias

    select_blocks(0, 0)

    def q_tile_group(i, _):
        tiles = [n_tiles * i + t for t in range(n_tiles)]
        rows = [pl.ds(pl.multiple_of(qt * blk, blk), blk) for qt in tiles]
        slot = i % 2

        qs = {(t, h): qs_ref[h, rows[t], :] for t, h in combos}

        def scores(n, t, h, own, minus=0.0):
            s = _mm(kb_ref[pl.ds(pl.multiple_of(n * blk, blk), blk), :], qs[t, h], _NT)
            if own:
                return jnp.where(causal, s - minus, NEG)
            return s + (bias_ref[slot, t, h, pl.ds(n, 1), :] - minus)

        tail = [(tiles[n], t, n == t) for t in range(n_tiles) for n in range(t + 1)]
        n_past_pairs = (n_tiles // 2) * i

        def exact_max():
            def past_pair(j, m):
                s = {(k, t, h): scores(2 * j + k, t, h, False) for k in range(2) for t, h in combos}
                return {(t, h): jnp.maximum(m[t, h], jnp.max(jnp.maximum(s[0, t, h], s[1, t, h]),
                                                                 axis=0, keepdims=True)) for t, h in combos}

            m = {(t, h): jnp.full((1, blk), NEG, F32) for t, h in combos}
            for n, t, own in tail:
                for h in range(2):
                    m[t, h] = jnp.maximum(m[t, h], jnp.max(scores(n, t, h, own), axis=0, keepdims=True))
            return lax.fori_loop(0, n_past_pairs, past_pair, m)

        m = lax.cond(bound_is_safe, lambda: {c: bound_row for c in combos}, exact_max)

        def weighted_v(s, n, h):
            return _mm(vt_ref[h, n], jnp.exp(s).astype(BF16))

        def past_pair(j, pv):
            s = {(k, t, h): scores(2 * j + k, t, h, False, m[t, h]) for k in range(2) for t, h in combos}
            return {(t, h): pv[t, h] + weighted_v(s[0, t, h], 2 * j, h) + weighted_v(s[1, t, h], 2 * j + 1, h)
                    for t, h in combos}

        select_blocks(jnp.minimum(i + 1, nb // n_tiles - 1), 1 - slot)
        s = {(k, h): scores(n, t, h, own, m[t, h]) for k, (n, t, own) in enumerate(tail) for h in range(2)}
        pv = {(t, h): sum(weighted_v(s[k, h], n, h) for k, (n, tt, _) in enumerate(tail) if tt == t)
              for t, h in combos}
        pv = lax.fori_loop(0, n_past_pairs, past_pair, pv)
        for t in range(n_tiles):
            o_t = jnp.concatenate([pv[t, h][:HEAD_DIM] / pv[t, h][HEAD_DIM:HEAD_DIM + 1] for h in range(2)],
                                  axis=0)
            o_ref[0, rows[t], :] = o_t.T
        return 0

    lax.fori_loop(0, nb // n_tiles, q_tile_group, 0)


def _moba(pb, gq, gk, layer):
    bsz, seq, _ = pb.shape
    n_pairs = B_HEADS // 2
    nb = seq // MOBA_BLOCK
    ones_rows = BF16_SUBLANES
    slab = lambda off: pl.BlockSpec((1, seq, LANES), lambda b, j: (b, 0, off + j))
    gain = _layer_block((1, LANES), layer, 2)
    return pl.pallas_call(
        _moba_kernel,
        grid=(bsz, n_pairs),
        in_specs=[slab(0), slab(n_pairs), slab(2 * n_pairs), gain, gain],
        out_specs=pl.BlockSpec((1, seq, LANES), lambda b, j: (b, 0, j)),
        out_shape=jax.ShapeDtypeStruct((bsz, seq, B_WIDTH), F32),
        scratch_shapes=[pltpu.VMEM((seq, LANES), F32), pltpu.VMEM((seq, LANES), BF16),
                        pltpu.VMEM((2, seq, LANES), BF16), pltpu.VMEM((2, nb, LANES), F32),
                        pltpu.VMEM((2, nb, HEAD_DIM + ones_rows, MOBA_BLOCK), BF16),
                        pltpu.VMEM((2, MOBA_TILES, 2, nb, MOBA_BLOCK), F32)],
        compiler_params=_params(2),
        name="moba",
    )(pb, pb, pb, gq, gk)


def _dilated_kernel(q_ref, k_ref, v_ref, gq_ref, gk_ref, o_ref,
                    qn_ref, kn_ref, vv_ref, q1_ref, k1_ref, v1_ref, qs_ref, kb_ref, vt_ref, og_ref, lse_ref):
    seq = q_ref.shape[1]
    n_blocks = seq // BAND
    ones_rows = vt_ref.shape[2] - HEAD_DIM
    tile = ROW_TILE

    def normalise(i, _):
        rows = pl.ds(pl.multiple_of(i * tile, tile), tile)
        qn_ref[rows, :] = _qk_norm(q_ref[0, rows, :].astype(F32), gq_ref[...] * ATTN_SCALE)
        kn_ref[rows, :] = _qk_norm(k_ref[0, rows, :].astype(F32), gk_ref[...])
        vv_ref[rows, :] = v_ref[0, rows, :].astype(F32)
        return 0

    lax.fori_loop(0, seq // tile, normalise, 0)
    bound = _score_bound(gq_ref[...], gk_ref[...])
    bound_is_safe = bound <= MAX_SAFE_BOUND
    bound_row = jnp.full((1, BAND), bound, F32)

    is_a = _lane_is_a((BAND, LANES))
    key_j = lax.broadcasted_iota(jnp.int32, (2 * BAND, BAND), 0)
    qry_i = lax.broadcasted_iota(jnp.int32, (2 * BAND, BAND), 1)
    band_bias = jnp.where((key_j >= qry_i) & (key_j <= qry_i + BAND), 0.0, NEG)
    band_bias = jnp.concatenate([band_bias, band_bias], axis=1)
    band_bias_less_bound = band_bias - bound
    ones = jnp.ones((ones_rows, BAND), BF16)
    vt_rows = HEAD_DIM + ones_rows

    def own_rows(gb):
        return pl.ds(pl.multiple_of(gb * BAND, BAND), BAND)

    def rows_in_previous_level(g, gb):
        per_residue = n_blocks // DILATIONS[g]
        segment, c = gb // per_residue, gb % per_residue
        start = (segment // DIL_RATIO) * (seq // DILATIONS[g - 1]) + segment % DIL_RATIO
        return pl.ds(start + c * (BAND * DIL_RATIO), BAND, stride=DIL_RATIO)

    for g, dil in enumerate(DILATIONS):
        per_residue = n_blocks // dil
        sources = (qn_ref, kn_ref, vv_ref) if g <= 1 else (q1_ref, k1_ref, v1_ref)

        def gather(i, _, g=g, sources=sources):
            for gb in [DIL_GROUP * i + j for j in range(DIL_GROUP)]:
                rows = own_rows(gb) if g == 0 else rows_in_previous_level(g, gb)
                dst = own_rows(gb)
                q, k, v = (ref[rows, :] for ref in sources)
                if g == 1:
                    q1_ref[dst, :], k1_ref[dst, :], v1_ref[dst, :] = q, k, v
                qs_ref[0, dst, :] = jnp.where(is_a, q, 0.0).astype(BF16)
                qs_ref[1, dst, :] = jnp.where(is_a, 0.0, q).astype(BF16)
                kb_ref[dst, :] = k.astype(BF16)
                vt = v.T.astype(BF16)
                for h in range(2):
                    vt_ref[h, gb, :HEAD_DIM, :] = vt[h * HEAD_DIM:(h + 1) * HEAD_DIM]
                    vt_ref[h, gb, HEAD_DIM:, :] = ones
            return 0

        lax.fori_loop(0, n_blocks // DIL_GROUP, gather, 0)

        def band_block(gb, use_bound, per_residue=per_residue, g=g):
            prev = jnp.maximum(gb - 1, 0)
            first = jnp.where(gb % per_residue == 0, NEG, 0.0)
            kb = jnp.concatenate([kb_ref[own_rows(prev), :], kb_ref[own_rows(gb), :]], axis=0)

            def scores(bias):
                q_both = jnp.concatenate([qs_ref[0, own_rows(gb), :], qs_ref[1, own_rows(gb), :]], axis=0)
                s = _mm(kb, q_both, _NT) + bias
                return jnp.concatenate([s[:BAND] + first, s[BAND:]], axis=0)

            if use_bound:
                m = jnp.concatenate([bound_row, bound_row], axis=1)
                s = scores(band_bias_less_bound)
                yield
                p = jnp.exp(s)
            else:
                m = jnp.max(scores(band_bias), axis=0, keepdims=True)
                yield
                s = scores(band_bias)
                yield
                p = jnp.exp(s - m)
            vt_both = jnp.concatenate([jnp.concatenate([vt_ref[h, prev], vt_ref[h, gb]], axis=1)
                                       for h in range(2)], axis=0)
            pv = _mm(vt_both, p.astype(BF16))
            yield
            pv = [pv[h * vt_rows:(h + 1) * vt_rows, h * BAND:(h + 1) * BAND] for h in range(2)]
            l = [x[HEAD_DIM:HEAD_DIM + 1] for x in pv]
            o_t = jnp.concatenate([pv[h][:HEAD_DIM] / l[h] for h in range(2)], axis=0)
            lse_t = jnp.concatenate([jnp.broadcast_to(m[:, h * BAND:(h + 1) * BAND] + jnp.log(l[h]),
                                                      (HEAD_DIM, BAND)) for h in range(2)], axis=0)
            out_rows = own_rows(gb) if g <= 1 else rows_in_previous_level(g, gb)
            og_ref[g, out_rows, :] = o_t.T
            lse_ref[g, out_rows, :] = lse_t.T

        def group(i, _, use_bound, band_block=band_block):
            _round_robin([band_block(DIL_GROUP * i + j, use_bound) for j in range(DIL_GROUP)])
            return 0

        for use_bound in (True, False):
            @pl.when(bound_is_safe == use_bound)
            def _(use_bound=use_bound, group=group):
                lax.fori_loop(0, n_blocks // DIL_GROUP, functools.partial(group, use_bound=use_bound), 0)

    segment_len = seq // DILATIONS[1]

    def mix(i, _):
        rows = pl.ds(pl.multiple_of(i * tile, tile), tile)
        start = i * tile
        tokens = pl.ds(start // segment_len + DIL_RATIO * (start % segment_len), tile, stride=DIL_RATIO)
        where = (tokens, rows, rows)
        lse = [lse_ref[g, where[g], :] for g in range(len(DILATIONS))]
        top = functools.reduce(jnp.maximum, lse)
        w = [jnp.exp(x - top) for x in lse]
        o_ref[0, tokens, :] = sum(w[g] * og_ref[g, where[g], :] for g in range(len(DILATIONS))) / sum(w)
        return 0

    lax.fori_loop(0, seq // tile, mix, 0)


def _dilated(pc, gq, gk, layer):
    bsz, seq, _ = pc.shape
    n_pairs = C_HEADS // 2
    n_pat = len(DILATIONS)
    ones_rows = BF16_SUBLANES
    slab = lambda off: pl.BlockSpec((1, seq, LANES), lambda b, j: (b, 0, off + j))
    gain = _layer_block((1, LANES), layer, 2)
    return pl.pallas_call(
        _dilated_kernel,
        grid=(bsz, n_pairs),
        in_specs=[slab(0), slab(n_pairs), slab(2 * n_pairs), gain, gain],
        out_specs=pl.BlockSpec((1, seq, LANES), lambda b, j: (b, 0, j)),
        out_shape=jax.ShapeDtypeStruct((bsz, seq, C_WIDTH), F32),
        scratch_shapes=[pltpu.VMEM((seq, LANES), F32)] * 6 + [
                        pltpu.VMEM((2, seq, LANES), BF16), pltpu.VMEM((seq, LANES), BF16),
                        pltpu.VMEM((2, seq // BAND, HEAD_DIM + ones_rows, BAND), BF16),
                        pltpu.VMEM((n_pat, seq, LANES), F32), pltpu.VMEM((n_pat, seq, LANES), F32)],
        compiler_params=_params(2),
        name="dilated",
    )(pc, pc, pc, gq, gk)


def kernel(x, norm_g, w_in, w_out, tshift_mu, decay_w0, decay_up, iclr_a0, iclr_up,
           k_k, k_a, r_k, lnx_g, lnx_b, moba_q_g, moba_k_g, dil_q_g, dil_k_g):
    bsz, seq, d_model = x.shape
    depth = norm_g.shape[0]
    assert all(b == DIL_RATIO * a for a, b in zip(DILATIONS, DILATIONS[1:])) and len(DILATIONS) == 3
    assert seq % (DIL_GROUP * BAND) == 0 and seq % (max(DILATIONS) * BAND) == 0, "dilated tiling"
    assert seq % (MOBA_TILES * MOBA_BLOCK) == 0 and MOBA_TILES % 2 == 0, "moba tiling"
    assert seq % ROW_TILE == 0 and ROW_TILE % RWKV_CHUNK == 0 and (bsz * seq) % ROW_TILE == 0, "row tiling"
    x2d = x.reshape(bsz * seq, d_model)
    w_in_bf = w_in.astype(BF16)
    w_out_bf = w_out.astype(BF16)
    gains = norm_g.reshape(depth, 1, d_model)
    pp, wz = _rwkv_pack_params(decay_w0, decay_up, iclr_a0, iclr_up, k_k, k_a, r_k, lnx_g, lnx_b)
    mu = tshift_mu.reshape(depth, 1, A_PROJ)
    two_heads = lambda g: jnp.tile(g, (1, 2)).reshape(depth, 1, LANES)
    moba_gq, moba_gk, dil_gq, dil_gk = map(two_heads, (moba_q_g, moba_k_g, dil_q_g, dil_k_g))
    pa, pb, pc, gate = _proj(x2d, in_args=(gains, w_in_bf, 0))
    for l in range(depth):
        ya = _rwkv(pa.reshape(bsz, seq, A_PROJ), mu, pp, wz, l)
        yb = _moba(pb.reshape(bsz, seq, B_PROJ), moba_gq, moba_gk, l)
        yc = _dilated(pc.reshape(bsz, seq, C_PROJ), dil_gq, dil_gk, l)
        out_args = (ya.reshape(bsz * seq, A_WIDTH), yb.reshape(bsz * seq, B_WIDTH),
                    yc.reshape(bsz * seq, C_WIDTH), gate, w_out_bf, l)
        if l + 1 < depth:
            x2d, pa, pb, pc, gate = _proj(x2d, out_args, (gains, w_in_bf, l + 1))
        else:
            (x2d,) = _proj(x2d, out_args)
    return x2d.reshape(bsz, seq, d_model)
```

```python
import functools

import jax
import jax.numpy as jnp
from jax import lax
from jax.experimental import pallas as pl
from jax.experimental.pallas import tpu as pltpu

F32 = jnp.float32
BF16 = jnp.bfloat16

HEAD_DIM = 64
LANES = 128
A_HEADS, B_HEADS, C_HEADS = 6, 4, 6
A_WIDTH, B_WIDTH, C_WIDTH = A_HEADS * HEAD_DIM, B_HEADS * HEAD_DIM, C_HEADS * HEAD_DIM
LORA = 64
A_PROJ = 3 * A_WIDTH + 2 * LORA
B_PROJ = 3 * B_WIDTH
C_PROJ = 3 * C_WIDTH
MIX_WIDTH = A_WIDTH + B_WIDTH + C_WIDTH
PROJ_WIDTH = A_PROJ + B_PROJ + C_PROJ + MIX_WIDTH
MOBA_BLOCK = 256
MOBA_TOPK = 3
MOBA_TILES = 8
DILATIONS = (1, 4, 16)
DIL_RATIO = 4
BAND = 128
DIL_GROUP = 32
RMS_EPS = 1e-6
LNX_EPS = HEAD_DIM * 1e-5
ATTN_SCALE = HEAD_DIM ** -0.5
RWKV_CHUNK = 64
NEG = -1e30
MAX_SAFE_BOUND = 40.0
VMEM_LIMIT = 56 * 1024 * 1024
BF16_SUBLANES = 16
ROW_TILE = 512


def _layer_block(shape, layer, n_grid_axes):
    index = (layer,) + (0,) * len(shape)
    if n_grid_axes == 1:
        return pl.BlockSpec((None,) + tuple(shape), lambda i: index)
    return pl.BlockSpec((None,) + tuple(shape), lambda b, j: index)


def _params(n_axes):
    return pltpu.CompilerParams(dimension_semantics=("arbitrary",) * n_axes,
                                vmem_limit_bytes=VMEM_LIMIT)


_NN = (((1,), (0,)), ((), ()))
_NT = (((1,), (1,)), ((), ()))
_TN = (((0,), (0,)), ((), ()))


def _mm(a, b, dims=_NN):
    return lax.dot_general(a, b, dims, preferred_element_type=F32)


def _split(x):
    hi = x.astype(BF16)
    lo = (x - hi.astype(F32)).astype(BF16)
    return hi, lo


def _mm3(a, b, dims=_NN):
    ah, al = _split(a)
    bh, bl = _split(b)
    return _mm(ah, bh, dims) + (_mm(ah, bl, dims) + _mm(al, bh, dims))


def _mm1(a, b, dims=_NN):
    return _mm(a.astype(BF16), b.astype(BF16), dims)


def _half_sum(x, is_a):
    sa = jnp.sum(jnp.where(is_a, x, 0.0), axis=-1, keepdims=True)
    sb = jnp.sum(jnp.where(is_a, 0.0, x), axis=-1, keepdims=True)
    return jnp.where(is_a, sa, sb)


def _half_sum_mxu(x):
    row = lax.broadcasted_iota(jnp.int32, (LANES, LANES), 0) // HEAD_DIM
    col = lax.broadcasted_iota(jnp.int32, (LANES, LANES), 1) // HEAD_DIM
    ones_bd = (row == col).astype(BF16)
    hi, lo = _split(x)
    return _mm(hi, ones_bd) + _mm(lo, ones_bd)


def _round_robin(chains):
    results = [None] * len(chains)
    live = list(range(len(chains)))
    while live:
        for i in list(live):
            try:
                next(chains[i])
            except StopIteration as done:
                results[i] = done.value
                live.remove(i)
    return results


def _lane_is_a(shape):
    return lax.broadcasted_iota(jnp.int32, shape, len(shape) - 1) < HEAD_DIM


_PROJ_WIDTHS = (A_PROJ, B_PROJ, C_PROJ, MIX_WIDTH)
_PROJ_DTYPES = (F32, BF16, BF16, BF16)


def _proj_kernel(*refs, has_out, has_in):
    refs = list(refs)
    x = refs.pop(0)[...]
    if has_out:
        ya_ref, yb_ref, yc_ref, gt_ref, w_out_ref = (refs.pop(0) for _ in range(5))
    if has_in:
        g_ref, w_in_ref = refs.pop(0), refs.pop(0)
    if has_out:
        gated = []
        lo = 0
        for y_ref in (ya_ref, yb_ref, yc_ref):
            hi = lo + y_ref.shape[-1]
            g = gt_ref[:, lo:hi].astype(F32)
            gated.append((y_ref[...] * (g * jax.nn.sigmoid(g))).astype(BF16))
            lo = hi
        half = MIX_WIDTH // 2
        split = half - gated[0].shape[-1]
        x = x + _mm(jnp.concatenate([gated[0], gated[1][:, :split]], axis=1), w_out_ref[:half, :])
        x = x + _mm(jnp.concatenate([gated[1][:, split:], gated[2]], axis=1), w_out_ref[half:, :])
        refs.pop(0)[...] = x
    if has_in:
        ms = jnp.mean(x * x, axis=-1, keepdims=True)
        h = (x * lax.rsqrt(ms + RMS_EPS) * g_ref[...]).astype(BF16)
        lo = 0
        for ref in refs:
            hi = lo + ref.shape[-1]
            ref[...] = _mm(h, w_in_ref[:, lo:hi]).astype(ref.dtype)
            lo = hi


def _proj(x2d, out_args=None, in_args=None, tm=ROW_TILE):
    m, d = x2d.shape
    row = lambda w: pl.BlockSpec((tm, w), lambda i: (i, 0))
    args, in_specs, out_specs, out_shape = [x2d], [row(d)], [], []
    if out_args is not None:
        *ys_and_gate, w_out, layer = out_args
        args += ys_and_gate + [w_out]
        in_specs += [row(A_WIDTH), row(B_WIDTH), row(C_WIDTH), row(MIX_WIDTH),
                     _layer_block((MIX_WIDTH, d), layer, 1)]
        out_specs.append(row(d))
        out_shape.append(jax.ShapeDtypeStruct((m, d), F32))
    if in_args is not None:
        gain, w_in, layer = in_args
        args += [gain, w_in]
        in_specs += [_layer_block((1, d), layer, 1), _layer_block((d, PROJ_WIDTH), layer, 1)]
        out_specs += [row(w) for w in _PROJ_WIDTHS]
        out_shape += [jax.ShapeDtypeStruct((m, w), dt) for w, dt in zip(_PROJ_WIDTHS, _PROJ_DTYPES)]
    return pl.pallas_call(
        functools.partial(_proj_kernel, has_out=out_args is not None, has_in=in_args is not None),
        grid=(m // tm,),
        in_specs=in_specs, out_specs=out_specs, out_shape=out_shape,
        compiler_params=_params(1),
        name="proj",
    )(*args)


_P_W0, _P_A0, _P_KK, _P_KA, _P_RK, _P_LNG, _P_LNB = range(7)
_P_ROWS = 8


def _rwkv_kernel(pa_ref, mu_ref, pp_ref, wz_ref, o_ref, st_ref, prev_ref):
    n_rows, tile = pa_ref.shape[0], pa_ref.shape[1]
    c_len = RWKV_CHUNK
    two_c = 2 * c_len
    n_chunks = tile // c_len
    n_pairs = A_HEADS // 2
    is_a = _lane_is_a((c_len, LANES))

    row_c = lax.broadcasted_iota(jnp.int32, (c_len, A_PROJ), 0)
    ri = lax.broadcasted_iota(jnp.int32, (c_len, c_len), 0)
    ci = lax.broadcasted_iota(jnp.int32, (c_len, c_len), 1)
    tril_c = (ri >= ci).astype(BF16)
    r2 = lax.broadcasted_iota(jnp.int32, (c_len, two_c), 0)
    c2 = lax.broadcasted_iota(jnp.int32, (c_len, two_c), 1) % c_len
    m_strict = r2 > c2
    m_incl = r2 >= c2
    eye2 = (r2 == c2).astype(F32)

    def stack(x):
        return jnp.concatenate([jnp.where(is_a, x, 0.0), jnp.where(is_a, 0.0, x)], axis=0)

    @pl.when(pl.program_id(1) == 0)
    def _():
        st_ref[...] = jnp.zeros_like(st_ref)
        prev_ref[...] = jnp.zeros_like(prev_ref)

    def pair_chunk(b, j, r, k, v, lora_w, lora_a):
        pp = pp_ref[j]
        prow = lambda i: pp[i:i + 1, :]
        w0, a0, k_k, k_a, r_k = prow(_P_W0), prow(_P_A0), prow(_P_KK), prow(_P_KA), prow(_P_RK)
        ln_g, ln_b = prow(_P_LNG), prow(_P_LNB)
        w = -jax.nn.softplus(-(w0 + lora_w)) - 0.5
        lw = -jnp.exp(w)
        a = jax.nn.sigmoid(a0 + lora_a)
        kk = k * k_k
        kk = kk * lax.rsqrt(_half_sum(kk * kk, is_a) + 1e-12)
        k2 = k * (1.0 + (a - 1.0) * k_a)
        kka = kk * a

        l1 = lw.astype(BF16)
        rem = lw - l1.astype(F32)
        l2 = rem.astype(BF16)
        l3 = (rem - l2.astype(F32)).astype(BF16)
        g = _mm(tril_c, l1) + (_mm(tril_c, l2) + _mm(tril_c, l3))
        yield
        g_end = g[c_len - 1:c_len, :]
        e_pos = jnp.exp(g)
        e_neg = jnp.exp(-g)
        e_prev = jnp.exp(g - lw)
        e_tail = jnp.exp(g_end - g)

        ab = -kk * e_prev
        rb = r * e_pos
        ab2 = stack(ab)
        bt2 = stack(kka * e_neg)
        kt2 = stack(k2 * e_neg)
        bp2 = stack(kka * e_tail)
        kp2 = stack(k2 * e_tail)
        v2 = stack(v)

        mm = _mm1(jnp.concatenate([ab, rb], axis=0), jnp.concatenate([bt2, kt2], axis=0), _NT)
        yield
        l_b = jnp.where(m_strict, mm[:c_len, :two_c], 0.0)
        l_k = jnp.where(m_strict, mm[:c_len, two_c:], 0.0)
        r_b = jnp.where(m_incl, mm[c_len:, :two_c], 0.0)
        r_k2 = jnp.where(m_incl, mm[c_len:, two_c:], 0.0)

        t_inv = eye2 + l_b
        p = _mm1(l_b, stack(l_b))
        kv = _mm1(l_k, v2)
        yield
        steps = c_len.bit_length() - 2
        for i in range(steps):
            if i + 1 < steps:
                tp = _mm1(jnp.concatenate([t_inv, p], axis=0), stack(p))
                yield
                t_inv = t_inv + tp[:c_len]
                p = tp[c_len:]
            else:
                tp = _mm1(t_inv, stack(p))
                yield
                t_inv = t_inv + tp

        tw = _mm1(t_inv, jnp.concatenate([ab2, stack(kv)], axis=1))
        yield
        sv = st_ref[b, j]
        ws = _mm1(jnp.concatenate([tw[:, :LANES], rb], axis=0), sv, _NT)
        yield
        uv = jnp.concatenate([stack(ws[:c_len] + tw[:, LANES:]), v2], axis=0)
        y = ws[c_len:] + _mm1(jnp.concatenate([r_b, r_k2], axis=1), uv)
        st_ref[b, j] = sv * jnp.exp(g_end) + _mm1(uv, jnp.concatenate([bp2, kp2], axis=0), _TN)
        yield

        mean = _half_sum(y, is_a) * (1.0 / HEAD_DIM)
        yc = y - mean
        var = _half_sum(yc * yc, is_a) * (1.0 / HEAD_DIM)
        y = yc * lax.rsqrt(var + LNX_EPS) * ln_g + ln_b
        return y + _half_sum(r * k2 * r_k, is_a) * v

    mu = mu_ref[...]
    wz = wz_ref[...]

    def body(c, prev_rows):
        t0 = pl.multiple_of(c * c_len, c_len)
        chains, last_rows = [], []
        for b in range(n_rows):
            x = pa_ref[b, pl.ds(t0, c_len), :]
            prev = jnp.where(row_c == 0, prev_rows[b], pltpu.roll(x, 1, axis=0))
            xs = x + (prev - x) * mu
            slab = lambda i, xs=xs: xs[:, i * LANES:(i + 1) * LANES]
            z = slab(3 * n_pairs)
            lora = _mm3(jnp.where(is_a, jnp.tanh(z), z), wz)
            chains += [pair_chunk(b, j, slab(j), slab(n_pairs + j), slab(2 * n_pairs + j),
                                  lora[:, 2 * j * LANES:(2 * j + 1) * LANES],
                                  lora[:, (2 * j + 1) * LANES:(2 * j + 2) * LANES])
                       for j in range(n_pairs)]
            last_rows.append(x[c_len - 1:c_len, :])
        for i, y in enumerate(_round_robin(chains)):
            b, j = divmod(i, n_pairs)
            o_ref[b, pl.ds(t0, c_len), j * LANES:(j + 1) * LANES] = y
        return tuple(last_rows)

    last = lax.fori_loop(0, n_chunks, body, tuple(prev_ref[b, 0:1, :] for b in range(n_rows)), unroll=2)
    for b in range(n_rows):
        prev_ref[b, 0:1, :] = last[b]


def _rwkv(pa, mu, pp, wz, layer, tile=ROW_TILE):
    bsz, seq, _ = pa.shape
    rows = next(r for r in (4, 2, 1) if bsz % r == 0)
    n_pairs = A_HEADS // 2
    return pl.pallas_call(
        _rwkv_kernel,
        grid=(bsz // rows, seq // tile),
        in_specs=[pl.BlockSpec((rows, tile, A_PROJ), lambda b, s: (b, s, 0)),
                  _layer_block((1, A_PROJ), layer, 2), _layer_block((n_pairs, _P_ROWS, LANES), layer, 2),
                  _layer_block((LANES, n_pairs * 2 * LANES), layer, 2)],
        out_specs=pl.BlockSpec((rows, tile, A_WIDTH), lambda b, s: (b, s, 0)),
        out_shape=jax.ShapeDtypeStruct((bsz, seq, A_WIDTH), F32),
        scratch_shapes=[pltpu.VMEM((rows, n_pairs, LANES, LANES), F32), pltpu.VMEM((rows, 8, A_PROJ), F32)],
        compiler_params=_params(2),
        name="rwkv",
    )(pa, mu, pp, wz)


def _rwkv_pack_params(w0, w_up, a0, a_up, k_k, k_a, r_k, ln_g, ln_b):
    depth = w0.shape[0]
    n_pairs = A_HEADS // 2
    pair = lambda t, j: t[..., j * LANES:(j + 1) * LANES]
    rows = {_P_W0: w0, _P_A0: a0, _P_KK: k_k, _P_KA: k_a, _P_RK: r_k.reshape(depth, -1),
            _P_LNG: ln_g, _P_LNB: ln_b}
    zero_row = jnp.zeros((depth, LANES), F32)
    zeros = jnp.zeros((depth, LORA, LANES), F32)
    pps, wzs = [], []
    for j in range(n_pairs):
        pps.append(jnp.stack([pair(rows[i], j) if i in rows else zero_row for i in range(_P_ROWS)], axis=1))
        wzs.append(jnp.concatenate([pair(w_up, j), zeros], axis=1))
        wzs.append(jnp.concatenate([zeros, pair(a_up, j)], axis=1))
    return jnp.stack(pps, axis=1), jnp.concatenate(wzs, axis=2)


def _qk_norm(x, gain):
    ms = _half_sum_mxu(x * x) * (1.0 / HEAD_DIM)
    return x * lax.rsqrt(ms + RMS_EPS) * gain


def _score_bound(gq, gk):
    return (HEAD_DIM * ATTN_SCALE) * jnp.max(jnp.abs(gq)) * jnp.max(jnp.abs(gk))


def _moba_kernel(q_ref, k_ref, v_ref, gq_ref, gk_ref, o_ref,
                 qn_ref, kb_ref, qs_ref, km_ref, vt_ref, bias_ref):
    seq = q_ref.shape[1]
    blk = MOBA_BLOCK
    nb = seq // blk
    ones_rows = vt_ref.shape[2] - HEAD_DIM
    is_a = _lane_is_a((blk, LANES))
    gq, gk = gq_ref[...], gk_ref[...]

    def prepare(i, _):
        for n in (2 * i, 2 * i + 1):
            rows = pl.ds(pl.multiple_of(n * blk, blk), blk)
            qn = _qk_norm(q_ref[0, rows, :].astype(F32), gq)
            qn_ref[rows, :] = qn
            qs = qn * ATTN_SCALE
            qs_ref[0, rows, :] = jnp.where(is_a, qs, 0.0).astype(BF16)
            qs_ref[1, rows, :] = jnp.where(is_a, 0.0, qs).astype(BF16)
            kn = _qk_norm(k_ref[0, rows, :].astype(F32), gk)
            kb_ref[rows, :] = kn.astype(BF16)
            km = jnp.mean(kn, axis=0, keepdims=True)
            km_ref[0, pl.ds(n, 1), :] = jnp.where(is_a[:1], km, 0.0)
            km_ref[1, pl.ds(n, 1), :] = jnp.where(is_a[:1], 0.0, km)
            vt = v_ref[0, rows, :].T
            ones = jnp.ones((ones_rows, blk), BF16)
            for h in range(2):
                vt_ref[h, n, :HEAD_DIM, :] = vt[h * HEAD_DIM:(h + 1) * HEAD_DIM]
                vt_ref[h, n, HEAD_DIM:, :] = ones
        return 0

    lax.fori_loop(0, nb // 2, prepare, 0)

    bound = _score_bound(gq, gk)
    bound_is_safe = bound <= MAX_SAFE_BOUND
    bound_row = jnp.full((1, blk), bound, F32)
    blk_row = lax.broadcasted_iota(jnp.int32, (nb, blk), 0)
    key_pos = lax.broadcasted_iota(jnp.int32, (blk, blk), 0)
    qry_pos = lax.broadcasted_iota(jnp.int32, (blk, blk), 1)
    causal = key_pos <= qry_pos

    n_tiles = MOBA_TILES
    combos = [(t, h) for t in range(n_tiles) for h in range(2)]

    def select_blocks(i, slot):
        for t, h in combos:
            qt = n_tiles * i + t
            gate = _mm3(km_ref[h], qn_ref[pl.ds(pl.multiple_of(qt * blk, blk), blk), :], _NT)
            gate = jnp.where(blk_row < qt, gate, -jnp.inf)
            bias = jnp.full((nb, blk), NEG, F32)
            for _ in range(MOBA_TOPK):
                top = jnp.max(gate, axis=0, keepdims=True)
                hit = (gate == top) & (top > -jnp.inf)
                first = jnp.min(jnp.where(hit, blk_row, nb), axis=0, keepdims=True)
                pick = blk_row == first
                bias = jnp.where(pick, 0.0, bias)
                gate = jnp.where(pick, -jnp.inf, gate)
            bias_ref[slot, t, h] = bias

    select_blocks(0, 0)

    def q_tile_group(i, _):
        tiles = [n_tiles * i + t for t in range(n_tiles)]
        rows = [pl.ds(pl.multiple_of(qt * blk, blk), blk) for qt in tiles]
        slot = i % 2

        qs = {(t, h): qs_ref[h, rows[t], :] for t, h in combos}

        def scores(n, t, h, own, minus=0.0):
            s = _mm(kb_ref[pl.ds(pl.multiple_of(n * blk, blk), blk), :], qs[t, h], _NT)
            if own:
                return jnp.where(causal, s - minus, NEG)
            return s + (bias_ref[slot, t, h, pl.ds(n, 1), :] - minus)

        tail = [(tiles[n], t, n == t) for t in range(n_tiles) for n in range(t + 1)]
        n_past_pairs = (n_tiles // 2) * i

        def exact_max():
            def past_pair(j, m):
                s = {(k, t, h): scores(2 * j + k, t, h, False) for k in range(2) for t, h in combos}
                return {(t, h): jnp.maximum(m[t, h], jnp.max(jnp.maximum(s[0, t, h], s[1, t, h]),
                                                                 axis=0, keepdims=True)) for t, h in combos}

            m = {(t, h): jnp.full((1, blk), NEG, F32) for t, h in combos}
            for n, t, own in tail:
                for h in range(2):
                    m[t, h] = jnp.maximum(m[t, h], jnp.max(scores(n, t, h, own), axis=0, keepdims=True))
            return lax.fori_loop(0, n_past_pairs, past_pair, m)

        m = lax.cond(bound_is_safe, lambda: {c: bound_row for c in combos}, exact_max)

        def weighted_v(s, n, h):
            return _mm(vt_ref[h, n], jnp.exp(s).astype(BF16))

        def past_pair(j, pv):
            s = {(k, t, h): scores(2 * j + k, t, h, False, m[t, h]) for k in range(2) for t, h in combos}
            return {(t, h): pv[t, h] + weighted_v(s[0, t, h], 2 * j, h) + weighted_v(s[1, t, h], 2 * j + 1, h)
                    for t, h in combos}

        select_blocks(jnp.minimum(i + 1, nb // n_tiles - 1), 1 - slot)
        s = {(k, h): scores(n, t, h, own, m[t, h]) for k, (n, t, own) in enumerate(tail) for h in range(2)}
        pv = {(t, h): sum(weighted_v(s[k, h], n, h) for k, (n, tt, _) in enumerate(tail) if tt == t)
              for t, h in combos}
        pv = lax.fori_loop(0, n_past_pairs, past_pair, pv)
        for t in range(n_tiles):
            o_t = jnp.concatenate([pv[t, h][:HEAD_DIM] / pv[t, h][HEAD_DIM:HEAD_DIM + 1] for h in range(2)],
                                  axis=0)
            o_ref[0, rows[t], :] = o_t.T
        return 0

    lax.fori_loop(0, nb // n_tiles, q_tile_group, 0)


def _moba(pb, gq, gk, layer):
    bsz, seq, _ = pb.shape
    n_pairs = B_HEADS // 2
    nb = seq // MOBA_BLOCK
    ones_rows = BF16_SUBLANES
    slab = lambda off: pl.BlockSpec((1, seq, LANES), lambda b, j: (b, 0, off + j))
    gain = _layer_block((1, LANES), layer, 2)
    return pl.pallas_call(
        _moba_kernel,
        grid=(bsz, n_pairs),
        in_specs=[slab(0), slab(n_pairs), slab(2 * n_pairs), gain, gain],
        out_specs=pl.BlockSpec((1, seq, LANES), lambda b, j: (b, 0, j)),
        out_shape=jax.ShapeDtypeStruct((bsz, seq, B_WIDTH), F32),
        scratch_shapes=[pltpu.VMEM((seq, LANES), F32), pltpu.VMEM((seq, LANES), BF16),
                        pltpu.VMEM((2, seq, LANES), BF16), pltpu.VMEM((2, nb, LANES), F32),
                        pltpu.VMEM((2, nb, HEAD_DIM + ones_rows, MOBA_BLOCK), BF16),
                        pltpu.VMEM((2, MOBA_TILES, 2, nb, MOBA_BLOCK), F32)],
        compiler_params=_params(2),
        name="moba",
    )(pb, pb, pb, gq, gk)


def _dilated_kernel(q_ref, k_ref, v_ref, gq_ref, gk_ref, o_ref,
                    qn_ref, kn_ref, vv_ref, q1_ref, k1_ref, v1_ref, qs_ref, kb_ref, vt_ref, og_ref, lse_ref):
    seq = q_ref.shape[1]
    n_blocks = seq // BAND
    ones_rows = vt_ref.shape[2] - HEAD_DIM
    tile = ROW_TILE

    def normalise(i, _):
        rows = pl.ds(pl.multiple_of(i * tile, tile), tile)
        qn_ref[rows, :] = _qk_norm(q_ref[0, rows, :].astype(F32), gq_ref[...] * ATTN_SCALE)
        kn_ref[rows, :] = _qk_norm(k_ref[0, rows, :].astype(F32), gk_ref[...])
        vv_ref[rows, :] = v_ref[0, rows, :].astype(F32)
        return 0

    lax.fori_loop(0, seq // tile, normalise, 0)
    bound = _score_bound(gq_ref[...], gk_ref[...])
    bound_is_safe = bound <= MAX_SAFE_BOUND
    bound_row = jnp.full((1, BAND), bound, F32)

    is_a = _lane_is_a((BAND, LANES))
    key_j = lax.broadcasted_iota(jnp.int32, (2 * BAND, BAND), 0)
    qry_i = lax.broadcasted_iota(jnp.int32, (2 * BAND, BAND), 1)
    band_bias = jnp.where((key_j >= qry_i) & (key_j <= qry_i + BAND), 0.0, NEG)
    band_bias = jnp.concatenate([band_bias, band_bias], axis=1)
    band_bias_less_bound = band_bias - bound
    ones = jnp.ones((ones_rows, BAND), BF16)
    vt_rows = HEAD_DIM + ones_rows

    def own_rows(gb):
        return pl.ds(pl.multiple_of(gb * BAND, BAND), BAND)

    def rows_in_previous_level(g, gb):
        per_residue = n_blocks // DILATIONS[g]
        segment, c = gb // per_residue, gb % per_residue
        start = (segment // DIL_RATIO) * (seq // DILATIONS[g - 1]) + segment % DIL_RATIO
        return pl.ds(start + c * (BAND * DIL_RATIO), BAND, stride=DIL_RATIO)

    for g, dil in enumerate(DILATIONS):
        per_residue = n_blocks // dil
        sources = (qn_ref, kn_ref, vv_ref) if g <= 1 else (q1_ref, k1_ref, v1_ref)

        def gather(i, _, g=g, sources=sources):
            for gb in [DIL_GROUP * i + j for j in range(DIL_GROUP)]:
                rows = own_rows(gb) if g == 0 else rows_in_previous_level(g, gb)
                dst = own_rows(gb)
                q, k, v = (ref[rows, :] for ref in sources)
                if g == 1:
                    q1_ref[dst, :], k1_ref[dst, :], v1_ref[dst, :] = q, k, v
                qs_ref[0, dst, :] = jnp.where(is_a, q, 0.0).astype(BF16)
                qs_ref[1, dst, :] = jnp.where(is_a, 0.0, q).astype(BF16)
                kb_ref[dst, :] = k.astype(BF16)
                vt = v.astype(BF16).T
                for h in range(2):
                    vt_ref[h, gb, :HEAD_DIM, :] = vt[h * HEAD_DIM:(h + 1) * HEAD_DIM]
                    vt_ref[h, gb, HEAD_DIM:, :] = ones
            return 0

        lax.fori_loop(0, n_blocks // DIL_GROUP, gather, 0)

        def band_block(gb, use_bound, per_residue=per_residue, g=g):
            prev = jnp.maximum(gb - 1, 0)
            first = jnp.where(gb % per_residue == 0, NEG, 0.0)
            kb = jnp.concatenate([kb_ref[own_rows(prev), :], kb_ref[own_rows(gb), :]], axis=0)

            def scores(bias):
                q_both = jnp.concatenate([qs_ref[0, own_rows(gb), :], qs_ref[1, own_rows(gb), :]], axis=0)
                s = _mm(kb, q_both, _NT) + bias
                return jnp.concatenate([s[:BAND] + first, s[BAND:]], axis=0)

            if use_bound:
                m = jnp.concatenate([bound_row, bound_row], axis=1)
                s = scores(band_bias_less_bound)
                yield
                p = jnp.exp(s)
            else:
                m = jnp.max(scores(band_bias), axis=0, keepdims=True)
                yield
                s = scores(band_bias)
                yield
                p = jnp.exp(s - m)
            vt_both = jnp.concatenate([jnp.concatenate([vt_ref[h, prev], vt_ref[h, gb]], axis=1)
                                       for h in range(2)], axis=0)
            pv = _mm(vt_both, p.astype(BF16))
            yield
            pv = [pv[h * vt_rows:(h + 1) * vt_rows, h * BAND:(h + 1) * BAND] for h in range(2)]
            l = [x[HEAD_DIM:HEAD_DIM + 1] for x in pv]
            o_t = jnp.concatenate([pv[h][:HEAD_DIM] / l[h] for h in range(2)], axis=0)
            lse_t = jnp.concatenate([jnp.broadcast_to(m[:, h * BAND:(h + 1) * BAND] + jnp.log(l[h]),
                                                      (HEAD_DIM, BAND)) for h in range(2)], axis=0)
            out_rows = own_rows(gb) if g <= 1 else rows_in_previous_level(g, gb)
            og_ref[g, out_rows, :] = o_t.T
            lse_ref[g, out_rows, :] = lse_t.T

        def group(i, _, use_bound, band_block=band_block):
            _round_robin([band_block(DIL_GROUP * i + j, use_bound) for j in range(DIL_GROUP)])
            return 0

        for use_bound in (True, False):
            @pl.when(bound_is_safe == use_bound)
            def _(use_bound=use_bound, group=group):
                lax.fori_loop(0, n_blocks // DIL_GROUP, functools.partial(group, use_bound=use_bound), 0)

    segment_len = seq // DILATIONS[1]

    def mix(i, _):
        rows = pl.ds(pl.multiple_of(i * tile, tile), tile)
        start = i * tile
        tokens = pl.ds(start // segment_len + DIL_RATIO * (start % segment_len), tile, stride=DIL_RATIO)
        where = (tokens, rows, rows)
        lse = [lse_ref[g, where[g], :] for g in range(len(DILATIONS))]
        top = functools.reduce(jnp.maximum, lse)
        w = [jnp.exp(x - top) for x in lse]
        o_ref[0, tokens, :] = sum(w[g] * og_ref[g, where[g], :] for g in range(len(DILATIONS))) / sum(w)
        return 0

    lax.fori_loop(0, seq // tile, mix, 0)


def _dilated(pc, gq, gk, layer):
    bsz, seq, _ = pc.shape
    n_pairs = C_HEADS // 2
    n_pat = len(DILATIONS)
    ones_rows = BF16_SUBLANES
    slab = lambda off: pl.BlockSpec((1, seq, LANES), lambda b, j: (b, 0, off + j))
    gain = _layer_block((1, LANES), layer, 2)
    return pl.pallas_call(
        _dilated_kernel,
        grid=(bsz, n_pairs),
        in_specs=[slab(0), slab(n_pairs), slab(2 * n_pairs), gain, gain],
        out_specs=pl.BlockSpec((1, seq, LANES), lambda b, j: (b, 0, j)),
        out_shape=jax.ShapeDtypeStruct((bsz, seq, C_WIDTH), F32),
        scratch_shapes=[pltpu.VMEM((seq, LANES), F32)] * 6 + [
                        pltpu.VMEM((2, seq, LANES), BF16), pltpu.VMEM((seq, LANES), BF16),
                        pltpu.VMEM((2, seq // BAND, HEAD_DIM + ones_rows, BAND), BF16),
                        pltpu.VMEM((n_pat, seq, LANES), F32), pltpu.VMEM((n_pat, seq, LANES), F32)],
        compiler_params=_params(2),
        name="dilated",
    )(pc, pc, pc, gq, gk)


def kernel(x, norm_g, w_in, w_out, tshift_mu, decay_w0, decay_up, iclr_a0, iclr_up,
           k_k, k_a, r_k, lnx_g, lnx_b, moba_q_g, moba_k_g, dil_q_g, dil_k_g):
    bsz, seq, d_model = x.shape
    depth = norm_g.shape[0]
    assert all(b == DIL_RATIO * a for a, b in zip(DILATIONS, DILATIONS[1:])) and len(DILATIONS) == 3
    assert seq % (DIL_GROUP * BAND) == 0 and seq % (max(DILATIONS) * BAND) == 0, "dilated tiling"
    assert seq % (MOBA_TILES * MOBA_BLOCK) == 0 and MOBA_TILES % 2 == 0, "moba tiling"
    assert seq % ROW_TILE == 0 and ROW_TILE % RWKV_CHUNK == 0 and (bsz * seq) % ROW_TILE == 0, "row tiling"
    x2d = x.reshape(bsz * seq, d_model)
    w_in_bf = w_in.astype(BF16)
    w_out_bf = w_out.astype(BF16)
    gains = norm_g.reshape(depth, 1, d_model)
    pp, wz = _rwkv_pack_params(decay_w0, decay_up, iclr_a0, iclr_up, k_k, k_a, r_k, lnx_g, lnx_b)
    mu = tshift_mu.reshape(depth, 1, A_PROJ)
    two_heads = lambda g: jnp.tile(g, (1, 2)).reshape(depth, 1, LANES)
    moba_gq, moba_gk, dil_gq, dil_gk = map(two_heads, (moba_q_g, moba_k_g, dil_q_g, dil_k_g))
    pa, pb, pc, gate = _proj(x2d, in_args=(gains, w_in_bf, 0))
    for l in range(depth):
        ya = _rwkv(pa.reshape(bsz, seq, A_PROJ), mu, pp, wz, l)
        yb = _moba(pb.reshape(bsz, seq, B_PROJ), moba_gq, moba_gk, l)
        yc = _dilated(pc.reshape(bsz, seq, C_PROJ), dil_gq, dil_gk, l)
        out_args = (ya.reshape(bsz * seq, A_WIDTH), yb.reshape(bsz * seq, B_WIDTH),
                    yc.reshape(bsz * seq, C_WIDTH), gate, w_out_bf, l)
        if l + 1 < depth:
            x2d, pa, pb, pc, gate = _proj(x2d, out_args, (gains, w_in_bf, l + 1))
        else:
            (x2d,) = _proj(x2d, out_args)
    return x2d.reshape(bsz, seq, d_model)
```

```python
import functools

import jax
import jax.numpy as jnp
from jax import lax
from jax.experimental import pallas as pl
from jax.experimental.pallas import tpu as pltpu

F32 = jnp.float32
BF16 = jnp.bfloat16

HEAD_DIM = 64
LANES = 128
A_HEADS, B_HEADS, C_HEADS = 6, 4, 6
A_WIDTH, B_WIDTH, C_WIDTH = A_HEADS * HEAD_DIM, B_HEADS * HEAD_DIM, C_HEADS * HEAD_DIM
LORA = 64
A_PROJ = 3 * A_WIDTH + 2 * LORA
B_PROJ = 3 * B_WIDTH
C_PROJ = 3 * C_WIDTH
MIX_WIDTH = A_WIDTH + B_WIDTH + C_WIDTH
PROJ_WIDTH = A_PROJ + B_PROJ + C_PROJ + MIX_WIDTH
MOBA_BLOCK = 256
MOBA_TOPK = 3
MOBA_TILES = 8
DILATIONS = (1, 4, 16)
DIL_RATIO = 4
BAND = 128
DIL_GROUP = 32
RMS_EPS = 1e-6
LNX_EPS = HEAD_DIM * 1e-5
ATTN_SCALE = HEAD_DIM ** -0.5
RWKV_CHUNK = 64
NEG = -1e30
MAX_SAFE_BOUND = 40.0
VMEM_LIMIT = 56 * 1024 * 1024
BF16_SUBLANES = 16
ROW_TILE = 512


def _layer_block(shape, layer, n_grid_axes):
    index = (layer,) + (0,) * len(shape)
    if n_grid_axes == 1:
        return pl.BlockSpec((None,) + tuple(shape), lambda i: index)
    return pl.BlockSpec((None,) + tuple(shape), lambda b, j: index)


def _params(n_axes):
    return pltpu.CompilerParams(dimension_semantics=("arbitrary",) * n_axes,
                                vmem_limit_bytes=VMEM_LIMIT)


_NN = (((1,), (0,)), ((), ()))
_NT = (((1,), (1,)), ((), ()))
_TN = (((0,), (0,)), ((), ()))


def _mm(a, b, dims=_NN):
    return lax.dot_general(a, b, dims, preferred_element_type=F32)


def _split(x):
    hi = x.astype(BF16)
    lo = (x - hi.astype(F32)).astype(BF16)
    return hi, lo


def _mm3(a, b, dims=_NN):
    ah, al = _split(a)
    bh, bl = _split(b)
    return _mm(ah, bh, dims) + (_mm(ah, bl, dims) + _mm(al, bh, dims))


def _mm1(a, b, dims=_NN):
    return _mm(a.astype(BF16), b.astype(BF16), dims)


def _half_sum(x, is_a):
    sa = jnp.sum(jnp.where(is_a, x, 0.0), axis=-1, keepdims=True)
    sb = jnp.sum(jnp.where(is_a, 0.0, x), axis=-1, keepdims=True)
    return jnp.where(is_a, sa, sb)


def _half_sum_mxu(x):
    row = lax.broadcasted_iota(jnp.int32, (LANES, LANES), 0) // HEAD_DIM
    col = lax.broadcasted_iota(jnp.int32, (LANES, LANES), 1) // HEAD_DIM
    ones_bd = (row == col).astype(BF16)
    hi, lo = _split(x)
    return _mm(hi, ones_bd) + _mm(lo, ones_bd)


def _round_robin(chains):
    results = [None] * len(chains)
    live = list(range(len(chains)))
    while live:
        for i in list(live):
            try:
                next(chains[i])
            except StopIteration as done:
                results[i] = done.value
                live.remove(i)
    return results


def _lane_is_a(shape):
    return lax.broadcasted_iota(jnp.int32, shape, len(shape) - 1) < HEAD_DIM


_PROJ_WIDTHS = (A_PROJ, B_PROJ, C_PROJ, MIX_WIDTH)
_PROJ_DTYPES = (F32, BF16, BF16, BF16)


def _proj_kernel(*refs, has_out, has_in):
    refs = list(refs)
    x = refs.pop(0)[...]
    if has_out:
        ya_ref, yb_ref, yc_ref, gt_ref, w_out_ref = (refs.pop(0) for _ in range(5))
    if has_in:
        g_ref, w_in_ref = refs.pop(0), refs.pop(0)
    if has_out:
        gated = []
        lo = 0
        for y_ref in (ya_ref, yb_ref, yc_ref):
            hi = lo + y_ref.shape[-1]
            g = gt_ref[:, lo:hi].astype(F32)
            gated.append((y_ref[...] * (g * jax.nn.sigmoid(g))).astype(BF16))
            lo = hi
        half = MIX_WIDTH // 2
        split = half - gated[0].shape[-1]
        x = x + _mm(jnp.concatenate([gated[0], gated[1][:, :split]], axis=1), w_out_ref[:half, :])
        x = x + _mm(jnp.concatenate([gated[1][:, split:], gated[2]], axis=1), w_out_ref[half:, :])
        refs.pop(0)[...] = x
    if has_in:
        ms = jnp.mean(x * x, axis=-1, keepdims=True)
        h = (x * lax.rsqrt(ms + RMS_EPS) * g_ref[...]).astype(BF16)
        lo = 0
        for ref in refs:
            hi = lo + ref.shape[-1]
            ref[...] = _mm(h, w_in_ref[:, lo:hi]).astype(ref.dtype)
            lo = hi


def _proj(x2d, out_args=None, in_args=None, tm=ROW_TILE):
    m, d = x2d.shape
    row = lambda w: pl.BlockSpec((tm, w), lambda i: (i, 0))
    args, in_specs, out_specs, out_shape = [x2d], [row(d)], [], []
    if out_args is not None:
        *ys_and_gate, w_out, layer = out_args
        args += ys_and_gate + [w_out]
        in_specs += [row(A_WIDTH), row(B_WIDTH), row(C_WIDTH), row(MIX_WIDTH),
                     _layer_block((MIX_WIDTH, d), layer, 1)]
        out_specs.append(row(d))
        out_shape.append(jax.ShapeDtypeStruct((m, d), F32))
    if in_args is not None:
        gain, w_in, layer = in_args
        args += [gain, w_in]
        in_specs += [_layer_block((1, d), layer, 1), _layer_block((d, PROJ_WIDTH), layer, 1)]
        out_specs += [row(w) for w in _PROJ_WIDTHS]
        out_shape += [jax.ShapeDtypeStruct((m, w), dt) for w, dt in zip(_PROJ_WIDTHS, _PROJ_DTYPES)]
    return pl.pallas_call(
        functools.partial(_proj_kernel, has_out=out_args is not None, has_in=in_args is not None),
        grid=(m // tm,),
        in_specs=in_specs, out_specs=out_specs, out_shape=out_shape,
        compiler_params=_params(1),
        name="proj",
    )(*args)


_P_W0, _P_A0, _P_KK, _P_KA, _P_RK, _P_LNG, _P_LNB = range(7)
_P_ROWS = 8


def _rwkv_kernel(pa_ref, mu_ref, pp_ref, wz_ref, o_ref, st_ref, prev_ref):
    n_rows, tile = pa_ref.shape[0], pa_ref.shape[1]
    c_len = RWKV_CHUNK
    two_c = 2 * c_len
    n_chunks = tile // c_len
    n_pairs = A_HEADS // 2
    is_a = _lane_is_a((c_len, LANES))

    row_c = lax.broadcasted_iota(jnp.int32, (c_len, A_PROJ), 0)
    ri = lax.broadcasted_iota(jnp.int32, (c_len, c_len), 0)
    ci = lax.broadcasted_iota(jnp.int32, (c_len, c_len), 1)
    tril_c = (ri >= ci).astype(BF16)
    r2 = lax.broadcasted_iota(jnp.int32, (c_len, two_c), 0)
    c2 = lax.broadcasted_iota(jnp.int32, (c_len, two_c), 1) % c_len
    m_strict = r2 > c2
    m_incl = r2 >= c2
    eye2 = (r2 == c2).astype(F32)

    def stack(x):
        return jnp.concatenate([jnp.where(is_a, x, 0.0), jnp.where(is_a, 0.0, x)], axis=0)

    @pl.when(pl.program_id(1) == 0)
    def _():
        st_ref[...] = jnp.zeros_like(st_ref)
        prev_ref[...] = jnp.zeros_like(prev_ref)

    def pair_chunk(b, j, r, k, v, lora_w, lora_a):
        pp = pp_ref[j]
        prow = lambda i: pp[i:i + 1, :]
        w0, a0, k_k, k_a, r_k = prow(_P_W0), prow(_P_A0), prow(_P_KK), prow(_P_KA), prow(_P_RK)
        ln_g, ln_b = prow(_P_LNG), prow(_P_LNB)
        w = -jax.nn.softplus(-(w0 + lora_w)) - 0.5
        lw = -jnp.exp(w)
        a = jax.nn.sigmoid(a0 + lora_a)
        kk = k * k_k
        kk = kk * lax.rsqrt(_half_sum(kk * kk, is_a) + 1e-12)
        k2 = k * (1.0 + (a - 1.0) * k_a)
        kka = kk * a

        l1 = lw.astype(BF16)
        rem = lw - l1.astype(F32)
        l2 = rem.astype(BF16)
        l3 = (rem - l2.astype(F32)).astype(BF16)
        g = _mm(tril_c, l1) + (_mm(tril_c, l2) + _mm(tril_c, l3))
        yield
        g_end = g[c_len - 1:c_len, :]
        e_pos = jnp.exp(g)
        e_neg = jnp.exp(-g)
        e_prev = jnp.exp(g - lw)
        e_tail = jnp.exp(g_end - g)

        ab = -kk * e_prev
        rb = r * e_pos
        ab2 = stack(ab)
        bt2 = stack(kka * e_neg)
        kt2 = stack(k2 * e_neg)
        bp2 = stack(kka * e_tail)
        kp2 = stack(k2 * e_tail)
        v2 = stack(v)

        mm = _mm1(jnp.concatenate([ab, rb], axis=0), jnp.concatenate([bt2, kt2], axis=0), _NT)
        yield
        l_b = jnp.where(m_strict, mm[:c_len, :two_c], 0.0)
        l_k = jnp.where(m_strict, mm[:c_len, two_c:], 0.0)
        r_b = jnp.where(m_incl, mm[c_len:, :two_c], 0.0)
        r_k2 = jnp.where(m_incl, mm[c_len:, two_c:], 0.0)

        t_inv = eye2 + l_b
        p = _mm1(l_b, stack(l_b))
        kv = _mm1(l_k, v2)
        yield
        steps = c_len.bit_length() - 2
        for i in range(steps):
            if i + 1 < steps:
                tp = _mm1(jnp.concatenate([t_inv, p], axis=0), stack(p))
                yield
                t_inv = t_inv + tp[:c_len]
                p = tp[c_len:]
            else:
                tp = _mm1(t_inv, stack(p))
                yield
                t_inv = t_inv + tp

        tw = _mm1(t_inv, jnp.concatenate([ab2, stack(kv)], axis=1))
        yield
        sv = st_ref[b, j]
        ws = _mm1(jnp.concatenate([tw[:, :LANES], rb], axis=0), sv, _NT)
        yield
        uv = jnp.concatenate([stack(ws[:c_len] + tw[:, LANES:]), v2], axis=0)
        y = ws[c_len:] + _mm1(jnp.concatenate([r_b, r_k2], axis=1), uv)
        st_ref[b, j] = sv * jnp.exp(g_end) + _mm1(uv, jnp.concatenate([bp2, kp2], axis=0), _TN)
        yield

        mean = _half_sum(y, is_a) * (1.0 / HEAD_DIM)
        yc = y - mean
        var = _half_sum(yc * yc, is_a) * (1.0 / HEAD_DIM)
        y = yc * lax.rsqrt(var + LNX_EPS) * ln_g + ln_b
        return y + _half_sum(r * k2 * r_k, is_a) * v

    mu = mu_ref[...]
    wz = wz_ref[...]

    def body(c, prev_rows):
        t0 = pl.multiple_of(c * c_len, c_len)
        chains, last_rows = [], []
        for b in range(n_rows):
            x = pa_ref[b, pl.ds(t0, c_len), :]
            prev = jnp.where(row_c == 0, prev_rows[b], pltpu.roll(x, 1, axis=0))
            xs = x + (prev - x) * mu
            slab = lambda i, xs=xs: xs[:, i * LANES:(i + 1) * LANES]
            z = slab(3 * n_pairs)
            lora = _mm3(jnp.where(is_a, jnp.tanh(z), z), wz)
            chains += [pair_chunk(b, j, slab(j), slab(n_pairs + j), slab(2 * n_pairs + j),
                                  lora[:, 2 * j * LANES:(2 * j + 1) * LANES],
                                  lora[:, (2 * j + 1) * LANES:(2 * j + 2) * LANES])
                       for j in range(n_pairs)]
            last_rows.append(x[c_len - 1:c_len, :])
        for i, y in enumerate(_round_robin(chains)):
            b, j = divmod(i, n_pairs)
            o_ref[b, pl.ds(t0, c_len), j * LANES:(j + 1) * LANES] = y.astype(o_ref.dtype)
        return tuple(last_rows)

    last = lax.fori_loop(0, n_chunks, body, tuple(prev_ref[b, 0:1, :] for b in range(n_rows)), unroll=2)
    for b in range(n_rows):
        prev_ref[b, 0:1, :] = last[b]


def _rwkv(pa, mu, pp, wz, layer, tile=ROW_TILE):
    bsz, seq, _ = pa.shape
    rows = next(r for r in (4, 2, 1) if bsz % r == 0)
    n_pairs = A_HEADS // 2
    return pl.pallas_call(
        _rwkv_kernel,
        grid=(bsz // rows, seq // tile),
        in_specs=[pl.BlockSpec((rows, tile, A_PROJ), lambda b, s: (b, s, 0)),
                  _layer_block((1, A_PROJ), layer, 2), _layer_block((n_pairs, _P_ROWS, LANES), layer, 2),
                  _layer_block((LANES, n_pairs * 2 * LANES), layer, 2)],
        out_specs=pl.BlockSpec((rows, tile, A_WIDTH), lambda b, s: (b, s, 0)),
        out_shape=jax.ShapeDtypeStruct((bsz, seq, A_WIDTH), BF16),
        scratch_shapes=[pltpu.VMEM((rows, n_pairs, LANES, LANES), F32), pltpu.VMEM((rows, 8, A_PROJ), F32)],
        compiler_params=_params(2),
        name="rwkv",
    )(pa, mu, pp, wz)


def _rwkv_pack_params(w0, w_up, a0, a_up, k_k, k_a, r_k, ln_g, ln_b):
    depth = w0.shape[0]
    n_pairs = A_HEADS // 2
    pair = lambda t, j: t[..., j * LANES:(j + 1) * LANES]
    rows = {_P_W0: w0, _P_A0: a0, _P_KK: k_k, _P_KA: k_a, _P_RK: r_k.reshape(depth, -1),
            _P_LNG: ln_g, _P_LNB: ln_b}
    zero_row = jnp.zeros((depth, LANES), F32)
    zeros = jnp.zeros((depth, LORA, LANES), F32)
    pps, wzs = [], []
    for j in range(n_pairs):
        pps.append(jnp.stack([pair(rows[i], j) if i in rows else zero_row for i in range(_P_ROWS)], axis=1))
        wzs.append(jnp.concatenate([pair(w_up, j), zeros], axis=1))
        wzs.append(jnp.concatenate([zeros, pair(a_up, j)], axis=1))
    return jnp.stack(pps, axis=1), jnp.concatenate(wzs, axis=2)


def _qk_norm(x, gain):
    ms = _half_sum_mxu(x * x) * (1.0 / HEAD_DIM)
    return x * lax.rsqrt(ms + RMS_EPS) * gain


def _score_bound(gq, gk):
    return (HEAD_DIM * ATTN_SCALE) * jnp.max(jnp.abs(gq)) * jnp.max(jnp.abs(gk))


def _moba_kernel(q_ref, k_ref, v_ref, gq_ref, gk_ref, o_ref,
                 qn_ref, kb_ref, qs_ref, km_ref, vt_ref, bias_ref):
    seq = q_ref.shape[1]
    blk = MOBA_BLOCK
    nb = seq // blk
    ones_rows = vt_ref.shape[2] - HEAD_DIM
    is_a = _lane_is_a((blk, LANES))
    gq, gk = gq_ref[...], gk_ref[...]

    def prepare(i, _):
        for n in (2 * i, 2 * i + 1):
            rows = pl.ds(pl.multiple_of(n * blk, blk), blk)
            qn = _qk_norm(q_ref[0, rows, :].astype(F32), gq)
            qn_ref[rows, :] = qn
            qs = qn * ATTN_SCALE
            qs_ref[0, rows, :] = jnp.where(is_a, qs, 0.0).astype(BF16)
            qs_ref[1, rows, :] = jnp.where(is_a, 0.0, qs).astype(BF16)
            kn = _qk_norm(k_ref[0, rows, :].astype(F32), gk)
            kb_ref[rows, :] = kn.astype(BF16)
            km = jnp.mean(kn, axis=0, keepdims=True)
            km_ref[0, pl.ds(n, 1), :] = jnp.where(is_a[:1], km, 0.0)
            km_ref[1, pl.ds(n, 1), :] = jnp.where(is_a[:1], 0.0, km)
            vt = v_ref[0, rows, :].T
            ones = jnp.ones((ones_rows, blk), BF16)
            for h in range(2):
                vt_ref[h, n, :HEAD_DIM, :] = vt[h * HEAD_DIM:(h + 1) * HEAD_DIM]
                vt_ref[h, n, HEAD_DIM:, :] = ones
        return 0

    lax.fori_loop(0, nb // 2, prepare, 0)

    bound = _score_bound(gq, gk)
    bound_is_safe = bound <= MAX_SAFE_BOUND
    bound_row = jnp.full((1, blk), bound, F32)
    blk_row = lax.broadcasted_iota(jnp.int32, (nb, blk), 0)
    key_pos = lax.broadcasted_iota(jnp.int32, (blk, blk), 0)
    qry_pos = lax.broadcasted_iota(jnp.int32, (blk, blk), 1)
    causal = key_pos <= qry_pos

    n_tiles = MOBA_TILES
    combos = [(t, h) for t in range(n_tiles) for h in range(2)]

    def select_blocks(i, slot):
        for t, h in combos:
            qt = n_tiles * i + t
            gate = _mm3(km_ref[h], qn_ref[pl.ds(pl.multiple_of(qt * blk, blk), blk), :], _NT)
            gate = jnp.where(blk_row < qt, gate, -jnp.inf)
            bias = jnp.full((nb, blk), NEG, F32)
            for _ in range(MOBA_TOPK):
                top = jnp.max(gate, axis=0, keepdims=True)
                hit = (gate == top) & (top > -jnp.inf)
                first = jnp.min(jnp.where(hit, blk_row, nb), axis=0, keepdims=True)
                pick = blk_row == first
                bias = jnp.where(pick, 0.0, bias)
                gate = jnp.where(pick, -jnp.inf, gate)
            bias_ref[slot, t, h] = bias

    select_blocks(0, 0)

    def q_tile_group(i, _):
        tiles = [n_tiles * i + t for t in range(n_tiles)]
        rows = [pl.ds(pl.multiple_of(qt * blk, blk), blk) for qt in tiles]
        slot = i % 2

        qs = {(t, h): qs_ref[h, rows[t], :] for t, h in combos}

        def scores(n, t, h, own, minus=0.0):
            s = _mm(kb_ref[pl.ds(pl.multiple_of(n * blk, blk), blk), :], qs[t, h], _NT)
            if own:
                return jnp.where(causal, s - minus, NEG)
            return s + (bias_ref[slot, t, h, pl.ds(n, 1), :] - minus)

        tail = [(tiles[n], t, n == t) for t in range(n_tiles) for n in range(t + 1)]
        n_past_pairs = (n_tiles // 2) * i

        def exact_max():
            def past_pair(j, m):
                s = {(k, t, h): scores(2 * j + k, t, h, False) for k in range(2) for t, h in combos}
                return {(t, h): jnp.maximum(m[t, h], jnp.max(jnp.maximum(s[0, t, h], s[1, t, h]),
                                                                 axis=0, keepdims=True)) for t, h in combos}

            m = {(t, h): jnp.full((1, blk), NEG, F32) for t, h in combos}
            for n, t, own in tail:
                for h in range(2):
                    m[t, h] = jnp.maximum(m[t, h], jnp.max(scores(n, t, h, own), axis=0, keepdims=True))
            return lax.fori_loop(0, n_past_pairs, past_pair, m)

        m = lax.cond(bound_is_safe, lambda: {c: bound_row for c in combos}, exact_max)

        def weighted_v(s, n, h):
            return _mm(vt_ref[h, n], jnp.exp(s).astype(BF16))

        def past_pair(j, pv):
            s = {(k, t, h): scores(2 * j + k, t, h, False, m[t, h]) for k in range(2) for t, h in combos}
            return {(t, h): pv[t, h] + weighted_v(s[0, t, h], 2 * j, h) + weighted_v(s[1, t, h], 2 * j + 1, h)
                    for t, h in combos}

        select_blocks(jnp.minimum(i + 1, nb // n_tiles - 1), 1 - slot)
        s = {(k, h): scores(n, t, h, own, m[t, h]) for k, (n, t, own) in enumerate(tail) for h in range(2)}
        pv = {(t, h): sum(weighted_v(s[k, h], n, h) for k, (n, tt, _) in enumerate(tail) if tt == t)
              for t, h in combos}
        pv = lax.fori_loop(0, n_past_pairs, past_pair, pv)
        for t in range(n_tiles):
            o_t = jnp.concatenate([pv[t, h][:HEAD_DIM] / pv[t, h][HEAD_DIM:HEAD_DIM + 1] for h in range(2)],
                                  axis=0)
            o_ref[0, rows[t], :] = o_t.T.astype(o_ref.dtype)
        return 0

    lax.fori_loop(0, nb // n_tiles, q_tile_group, 0)


def _moba(pb, gq, gk, layer):
    bsz, seq, _ = pb.shape
    n_pairs = B_HEADS // 2
    nb = seq // MOBA_BLOCK
    ones_rows = BF16_SUBLANES
    slab = lambda off: pl.BlockSpec((1, seq, LANES), lambda b, j: (b, 0, off + j))
    gain = _layer_block((1, LANES), layer, 2)
    return pl.pallas_call(
        _moba_kernel,
        grid=(bsz, n_pairs),
        in_specs=[slab(0), slab(n_pairs), slab(2 * n_pairs), gain, gain],
        out_specs=pl.BlockSpec((1, seq, LANES), lambda b, j: (b, 0, j)),
        out_shape=jax.ShapeDtypeStruct((bsz, seq, B_WIDTH), BF16),
        scratch_shapes=[pltpu.VMEM((seq, LANES), F32), pltpu.VMEM((seq, LANES), BF16),
                        pltpu.VMEM((2, seq, LANES), BF16), pltpu.VMEM((2, nb, LANES), F32),
                        pltpu.VMEM((2, nb, HEAD_DIM + ones_rows, MOBA_BLOCK), BF16),
                        pltpu.VMEM((2, MOBA_TILES, 2, nb, MOBA_BLOCK), F32)],
        compiler_params=_params(2),
        name="moba",
    )(pb, pb, pb, gq, gk)


def _dilated_kernel(q_ref, k_ref, v_ref, gq_ref, gk_ref, o_ref,
                    qn_ref, kn_ref, vv_ref, q1_ref, k1_ref, v1_ref, qs_ref, kb_ref, vt_ref, og_ref, lse_ref):
    seq = q_ref.shape[1]
    n_blocks = seq // BAND
    ones_rows = vt_ref.shape[2] - HEAD_DIM
    tile = ROW_TILE

    def normalise(i, _):
        rows = pl.ds(pl.multiple_of(i * tile, tile), tile)
        qn_ref[rows, :] = _qk_norm(q_ref[0, rows, :].astype(F32), gq_ref[...] * ATTN_SCALE)
        kn_ref[rows, :] = _qk_norm(k_ref[0, rows, :].astype(F32), gk_ref[...])
        vv_ref[rows, :] = v_ref[0, rows, :].astype(F32)
        return 0

    lax.fori_loop(0, seq // tile, normalise, 0)
    bound = _score_bound(gq_ref[...], gk_ref[...])
    bound_is_safe = bound <= MAX_SAFE_BOUND
    bound_row = jnp.full((1, BAND), bound, F32)

    is_a = _lane_is_a((BAND, LANES))
    key_j = lax.broadcasted_iota(jnp.int32, (2 * BAND, BAND), 0)
    qry_i = lax.broadcasted_iota(jnp.int32, (2 * BAND, BAND), 1)
    band_bias = jnp.where((key_j >= qry_i) & (key_j <= qry_i + BAND), 0.0, NEG)
    band_bias = jnp.concatenate([band_bias, band_bias], axis=1)
    band_bias_less_bound = band_bias - bound
    ones = jnp.ones((ones_rows, BAND), BF16)
    vt_rows = HEAD_DIM + ones_rows

    def own_rows(gb):
        return pl.ds(pl.multiple_of(gb * BAND, BAND), BAND)

    def rows_in_previous_level(g, gb):
        per_residue = n_blocks // DILATIONS[g]
        segment, c = gb // per_residue, gb % per_residue
        start = (segment // DIL_RATIO) * (seq // DILATIONS[g - 1]) + segment % DIL_RATIO
        return pl.ds(start + c * (BAND * DIL_RATIO), BAND, stride=DIL_RATIO)

    for g, dil in enumerate(DILATIONS):
        per_residue = n_blocks // dil
        sources = (qn_ref, kn_ref, vv_ref) if g <= 1 else (q1_ref, k1_ref, v1_ref)

        def gather(i, _, g=g, sources=sources):
            for gb in [DIL_GROUP * i + j for j in range(DIL_GROUP)]:
                rows = own_rows(gb) if g == 0 else rows_in_previous_level(g, gb)
                dst = own_rows(gb)
                q, k, v = (ref[rows, :] for ref in sources)
                if g == 1:
                    q1_ref[dst, :], k1_ref[dst, :], v1_ref[dst, :] = q, k, v
                qs_ref[0, dst, :] = jnp.where(is_a, q, 0.0).astype(BF16)
                qs_ref[1, dst, :] = jnp.where(is_a, 0.0, q).astype(BF16)
                kb_ref[dst, :] = k.astype(BF16)
                vt = v.astype(BF16).T
                for h in range(2):
                    vt_ref[h, gb, :HEAD_DIM, :] = vt[h * HEAD_DIM:(h + 1) * HEAD_DIM]
                    vt_ref[h, gb, HEAD_DIM:, :] = ones
            return 0

        lax.fori_loop(0, n_blocks // DIL_GROUP, gather, 0)

        def band_block(gb, use_bound, per_residue=per_residue, g=g):
            prev = jnp.maximum(gb - 1, 0)
            first = jnp.where(gb % per_residue == 0, NEG, 0.0)
            kb = jnp.concatenate([kb_ref[own_rows(prev), :], kb_ref[own_rows(gb), :]], axis=0)

            def scores(bias):
                q_both = jnp.concatenate([qs_ref[0, own_rows(gb), :], qs_ref[1, own_rows(gb), :]], axis=0)
                s = _mm(kb, q_both, _NT) + bias
                return jnp.concatenate([s[:BAND] + first, s[BAND:]], axis=0)

            if use_bound:
                m = jnp.concatenate([bound_row, bound_row], axis=1)
                s = scores(band_bias_less_bound)
                yield
                p = jnp.exp(s)
            else:
                m = jnp.max(scores(band_bias), axis=0, keepdims=True)
                yield
                s = scores(band_bias)
                yield
                p = jnp.exp(s - m)
            vt_both = jnp.concatenate([jnp.concatenate([vt_ref[h, prev], vt_ref[h, gb]], axis=1)
                                       for h in range(2)], axis=0)
            pv = _mm(vt_both, p.astype(BF16))
            yield
            pv = [pv[h * vt_rows:(h + 1) * vt_rows, h * BAND:(h + 1) * BAND] for h in range(2)]
            l = [x[HEAD_DIM:HEAD_DIM + 1] for x in pv]
            o_t = jnp.concatenate([pv[h][:HEAD_DIM] / l[h] for h in range(2)], axis=0)
            lse_t = jnp.concatenate([jnp.broadcast_to(m[:, h * BAND:(h + 1) * BAND] + jnp.log(l[h]),
                                                      (HEAD_DIM, BAND)) for h in range(2)], axis=0)
            out_rows = own_rows(gb) if g <= 1 else rows_in_previous_level(g, gb)
            og_ref[g, out_rows, :] = o_t.T
            lse_ref[g, out_rows, :] = lse_t.T

        def group(i, _, use_bound, band_block=band_block):
            _round_robin([band_block(DIL_GROUP * i + j, use_bound) for j in range(DIL_GROUP)])
            return 0

        for use_bound in (True, False):
            @pl.when(bound_is_safe == use_bound)
            def _(use_bound=use_bound, group=group):
                lax.fori_loop(0, n_blocks // DIL_GROUP, functools.partial(group, use_bound=use_bound), 0)

    segment_len = seq // DILATIONS[1]

    def mix(i, _):
        rows = pl.ds(pl.multiple_of(i * tile, tile), tile)
        start = i * tile
        tokens = pl.ds(start // segment_len + DIL_RATIO * (start % segment_len), tile, stride=DIL_RATIO)
        where = (tokens, rows, rows)
        lse = [lse_ref[g, where[g], :] for g in range(len(DILATIONS))]
        top = functools.reduce(jnp.maximum, lse)
        w = [jnp.exp(x - top) for x in lse]
        o_ref[0, tokens, :] = sum(w[g] * og_ref[g, where[g], :] for g in range(len(DILATIONS))) / sum(w)
        return 0

    lax.fori_loop(0, seq // tile, mix, 0)


def _dilated(pc, gq, gk, layer):
    bsz, seq, _ = pc.shape
    n_pairs = C_HEADS // 2
    n_pat = len(DILATIONS)
    ones_rows = BF16_SUBLANES
    slab = lambda off: pl.BlockSpec((1, seq, LANES), lambda b, j: (b, 0, off + j))
    gain = _layer_block((1, LANES), layer, 2)
    return pl.pallas_call(
        _dilated_kernel,
        grid=(bsz, n_pairs),
        in_specs=[slab(0), slab(n_pairs), slab(2 * n_pairs), gain, gain],
        out_specs=pl.BlockSpec((1, seq, LANES), lambda b, j: (b, 0, j)),
        out_shape=jax.ShapeDtypeStruct((bsz, seq, C_WIDTH), F32),
        scratch_shapes=[pltpu.VMEM((seq, LANES), F32)] * 6 + [
                        pltpu.VMEM((2, seq, LANES), BF16), pltpu.VMEM((seq, LANES), BF16),
                        pltpu.VMEM((2, seq // BAND, HEAD_DIM + ones_rows, BAND), BF16),
                        pltpu.VMEM((n_pat, seq, LANES), F32), pltpu.VMEM((n_pat, seq, LANES), F32)],
        compiler_params=_params(2),
        name="dilated",
    )(pc, pc, pc, gq, gk)


def kernel(x, norm_g, w_in, w_out, tshift_mu, decay_w0, decay_up, iclr_a0, iclr_up,
           k_k, k_a, r_k, lnx_g, lnx_b, moba_q_g, moba_k_g, dil_q_g, dil_k_g):
    bsz, seq, d_model = x.shape
    depth = norm_g.shape[0]
    assert all(b == DIL_RATIO * a for a, b in zip(DILATIONS, DILATIONS[1:])) and len(DILATIONS) == 3
    assert seq % (DIL_GROUP * BAND) == 0 and seq % (max(DILATIONS) * BAND) == 0, "dilated tiling"
    assert seq % (MOBA_TILES * MOBA_BLOCK) == 0 and MOBA_TILES % 2 == 0, "moba tiling"
    assert seq % ROW_TILE == 0 and ROW_TILE % RWKV_CHUNK == 0 and (bsz * seq) % ROW_TILE == 0, "row tiling"
    x2d = x.reshape(bsz * seq, d_model)
    w_in_bf = w_in.astype(BF16)
    w_out_bf = w_out.astype(BF16)
    gains = norm_g.reshape(depth, 1, d_model)
    pp, wz = _rwkv_pack_params(decay_w0, decay_up, iclr_a0, iclr_up, k_k, k_a, r_k, lnx_g, lnx_b)
    mu = tshift_mu.reshape(depth, 1, A_PROJ)
    two_heads = lambda g: jnp.tile(g, (1, 2)).reshape(depth, 1, LANES)
    moba_gq, moba_gk, dil_gq, dil_gk = map(two_heads, (moba_q_g, moba_k_g, dil_q_g, dil_k_g))
    pa, pb, pc, gate = _proj(x2d, in_args=(gains, w_in_bf, 0))
    for l in range(depth):
        ya = _rwkv(pa.reshape(bsz, seq, A_PROJ), mu, pp, wz, l)
        yb = _moba(pb.reshape(bsz, seq, B_PROJ), moba_gq, moba_gk, l)
        yc = _dilated(pc.reshape(bsz, seq, C_PROJ), dil_gq, dil_gk, l)
        out_args = (ya.reshape(bsz * seq, A_WIDTH), yb.reshape(bsz * seq, B_WIDTH),
                    yc.reshape(bsz * seq, C_WIDTH), gate, w_out_bf, l)
        if l + 1 < depth:
            x2d, pa, pb, pc, gate = _proj(x2d, out_args, (gains, w_in_bf, l + 1))
        else:
            (x2d,) = _proj(x2d, out_args)
    return x2d.reshape(bsz, seq, d_model)
```
